```python
import math
import jax, jax.numpy as jnp
from jax import lax
import numpy as np

D_MODEL = 1024
BATCH = 8
SEQ = 2048
DEPTH = 4

A_HEAD_DIM = 64
A_HEADS = D_MODEL // A_HEAD_DIM
D_A = A_HEADS * A_HEAD_DIM
LORA_W = 64
LORA_A = 64
RWKV_GN_EPS = 64e-5
POOL_WINDOWS = (2, 4, 8, 16)
POOL_GROUPS = 4
D_B = 3 * D_MODEL // 4
POOL_CH = D_B // POOL_GROUPS
C_HEAD_DIM = 64
D_C = 3 * D_MODEL // 4
C_HEADS = D_C // C_HEAD_DIM
DILATION_GROUPS = ((128, 1), (512, 4), (2048, 16))
C_HEADS_PER_GROUP = C_HEADS // len(DILATION_GROUPS)
D_C_OUT = C_HEADS_PER_GROUP * C_HEAD_DIM
ROPE_THETA = 10000.0
Q_BLOCK = 128
N_SHIFT = 3 * D_A + 2 * LORA_W + 2 * LORA_A
SPLIT_SIZES = (N_SHIFT, D_A, D_B, D_B, D_C, D_C, D_C, D_C_OUT, D_MODEL, D_MODEL, D_MODEL)
N_IN = sum(SPLIT_SIZES)
DEEPNORM_ALPHA = (2 * DEPTH) ** 0.25
DEEPNORM_BETA = (8 * DEPTH) ** -0.25
LN_EPS = 1e-5

kernel_name = "hybrid_rwkv7_pool_dilated_attn_encoder"


def _split(t, sizes):
    out, start = [], 0
    for s in sizes:
        out.append(t[..., start:start + s])
        start += s
    return out


def _normalize(t, eps):
    t = t.astype(jnp.float32)
    mu = jnp.mean(t, axis=-1, keepdims=True)
    var = jnp.mean(jnp.square(t - mu), axis=-1, keepdims=True)
    return (t - mu) * lax.rsqrt(var + eps)


def _orient(t):
    return jnp.concatenate([t[:1], jnp.flip(t[1:], axis=2)], axis=0)


def rwkv7_branch(u, z, mu, w0, w_up, a0, a_up, k_k, k_a, r_k, gn_g, gn_b):
    f32 = jnp.float32
    u = u.astype(f32)
    B, S, _ = u.shape
    prev = jnp.pad(u, ((0, 0), (1, 0), (0, 0)))[:, :-1]
    nxt = jnp.pad(u, ((0, 0), (0, 1), (0, 0)))[:, 1:]
    u = u + mu[0] * (prev - u) + mu[1] * (nxt - u)
    r, k, v, wd_f, wd_b, ad_f, ad_b = _split(u, (D_A, D_A, D_A, LORA_W, LORA_W, LORA_A, LORA_A))
    wd = jnp.stack([wd_f, wd_b])
    ad = jnp.stack([ad_f, ad_b])
    w_log = -jax.nn.softplus(-(w0[:, None, None, :] + jnp.einsum('zbsl,zld->zbsd', jnp.tanh(wd), w_up))) - 0.5
    decay = jnp.exp(-jnp.exp(w_log))
    a = jax.nn.sigmoid(a0[:, None, None, :] + jnp.einsum('zbsl,zld->zbsd', ad, a_up))
    heads = lambda t: t.reshape(*t.shape[:-1], A_HEADS, A_HEAD_DIM)
    kk = heads(k * k_k)
    kk = kk * lax.rsqrt(jnp.sum(kk * kk, axis=-1, keepdims=True) + 1e-12)
    k_dir = heads(k[None] * (1.0 + (a - 1.0) * k_a))
    a_h = heads(a)
    r_h, v_h = heads(r), heads(v)
    both = lambda t: jnp.broadcast_to(t[None], (2,) + t.shape)
    inputs = (both(r_h), heads(decay), k_dir, both(v_h), both(-kk), kk[None] * a_h)
    xs = tuple(jnp.moveaxis(_orient(t), 2, 0) for t in inputs)

    def step(state, inp):
        r_t, w_t, k_t, v_t, nkk_t, b_t = inp
        sa = jnp.einsum('zbhij,zbhj->zbhi', state, nkk_t)
        state = state * w_t[..., None, :] + sa[..., None] * b_t[..., None, :] + v_t[..., None] * k_t[..., None, :]
        return state, jnp.einsum('zbhij,zbhj->zbhi', state, r_t)

    state0 = jnp.zeros((2, B, A_HEADS, A_HEAD_DIM, A_HEAD_DIM), f32)
    _, ys = lax.scan(step, state0, xs)
    y = _orient(jnp.moveaxis(ys, 0, 2))
    y = y[0] + y[1]
    y = _normalize(y, RWKV_GN_EPS).reshape(B, S, D_A) * gn_g + gn_b
    bonus = jnp.sum(r_h * heads(k) * r_k, axis=-1, keepdims=True) * v_h
    y = y + bonus.reshape(B, S, D_A)
    return y * jax.nn.silu(z.astype(f32))


def pool_branch(p, z, w_g, b_g, scale):
    f32 = jnp.float32
    B, S, _ = p.shape
    pg = p.astype(f32).reshape(B, S, POOL_GROUPS, POOL_CH)
    csum = jnp.pad(jnp.cumsum(pg, axis=1), ((0, 0), (1, 0), (0, 0), (0, 0)))
    half = jnp.array([w // 2 for w in POOL_WINDOWS], dtype=jnp.int32)
    pos = jnp.arange(S, dtype=jnp.int32)[:, None]
    lo = jnp.clip(pos - half[None, :], 0, S - 1)
    hi = jnp.clip(pos + half[None, :], 0, S - 1)
    g_idx = jnp.arange(POOL_GROUPS, dtype=jnp.int32)[None, :]
    window_sum = csum[:, hi + 1, g_idx] - csum[:, lo, g_idx]
    count = (hi - lo + 1).astype(f32)[..., None]
    mixed = window_sum / count - pg
    y = jnp.einsum('bsgc,gcd->bsgd', mixed, w_g).reshape(B, S, D_B) + b_g
    return y * scale * jax.nn.silu(z.astype(f32))


def rope_tables(S):
    inv = jnp.power(ROPE_THETA, -jnp.arange(0, C_HEAD_DIM, 2, dtype=jnp.float32) / C_HEAD_DIM)
    ang = jnp.arange(S, dtype=jnp.float32)[:, None] * inv[None, :]
    ang = jnp.concatenate([ang, ang], axis=-1)
    return jnp.cos(ang), jnp.sin(ang)


def apply_rope(t, cos, sin):
    t1, t2 = jnp.split(t, 2, axis=-1)
    rot = jnp.concatenate([-t2, t1], axis=-1)
    return t * cos[None, :, None, :] + rot * sin[None, :, None, :]


def dilated_band_attention(q, k, v, dilation, half_span):
    f32 = jnp.float32
    B, S, H, Dh = q.shape
    L = S // dilation
    n = half_span // dilation
    qb = math.gcd(Q_BLOCK, L)
    nb = L // qb
    kw = qb + 2 * n
    sub = lambda t: t.reshape(B, L, dilation, H, Dh).transpose(0, 2, 3, 1, 4)
    qs = sub(q).reshape(B, dilation, H, nb, qb, Dh)
    pad = ((0, 0), (0, 0), (0, 0), (n, n), (0, 0))
    ks = jnp.pad(sub(k), pad)
    vs = jnp.pad(sub(v), pad)
    kidx = jnp.arange(nb)[:, None] * qb + jnp.arange(kw)[None, :]
    kb = ks[:, :, :, kidx]
    vb = vs[:, :, :, kidx]
    s = jnp.einsum('bdhnqc,bdhnkc->bdhnqk', qs.astype(f32), kb.astype(f32)) * (Dh ** -0.5)
    rel = jnp.arange(kw)[None, :] - n - jnp.arange(qb)[:, None]
    key_pos = kidx[:, None, :] - n
    valid = (jnp.abs(rel)[None] <= n) & (key_pos >= 0) & (key_pos < L)
    s = jnp.where(valid, s, -jnp.inf)
    m = jnp.max(s, axis=-1, keepdims=True)
    e = jnp.exp(s - m)
    den = jnp.sum(e, axis=-1, keepdims=True)
    o = jnp.einsum('bdhnqk,bdhnkc->bdhnqc', e / den, vb.astype(f32))
    lse = (m + jnp.log(den))[..., 0]
    o = o.reshape(B, dilation, H, L, Dh).transpose(0, 3, 1, 2, 4).reshape(B, S, H, Dh)
    lse = lse.reshape(B, dilation, H, L).transpose(0, 3, 1, 2).reshape(B, S, H)
    return o, lse


def attention_branch(q, k, v, z, cos, sin):
    B, S, _ = q.shape
    hd = lambda t: t.reshape(B, S, C_HEADS, C_HEAD_DIM)
    q = apply_rope(hd(q), cos, sin)
    k = apply_rope(hd(k), cos, sin)
    v = hd(v)
    outs, lses = [], []
    for g, (window, dilation) in enumerate(DILATION_GROUPS):
        sl = slice(g * C_HEADS_PER_GROUP, (g + 1) * C_HEADS_PER_GROUP)
        o, lse = dilated_band_attention(q[:, :, sl], k[:, :, sl], v[:, :, sl], dilation, window // 2)
        outs.append(o)
        lses.append(lse)
    wts = jax.nn.softmax(jnp.stack(lses), axis=0)
    o = jnp.sum(wts[..., None] * jnp.stack(outs), axis=0).reshape(B, S, D_C_OUT)
    return o * jax.nn.silu(z.astype(jnp.float32))


def setup_inputs(seed: int = 0) -> dict:
    key = jax.random.key(seed)
    ks = jax.random.split(key, 24)
    f32 = jnp.float32
    L = DEPTH
    nrm = lambda kk, shape, s: jax.random.normal(kk, shape, f32) * s
    return {
        "x": nrm(ks[0], (BATCH, SEQ, D_MODEL), 1.0),
        "w_in": nrm(ks[1], (L, D_MODEL, N_IN), D_MODEL ** -0.5),
        "b_in": nrm(ks[2], (L, N_IN), 0.02),
        "rwkv_mu": jax.random.uniform(ks[3], (L, 2, N_SHIFT), f32, 0.0, 0.5),
        "rwkv_w0": jax.random.uniform(ks[4], (L, 2, D_A), f32, -6.0, 1.0),
        "rwkv_w_up": nrm(ks[5], (L, 2, LORA_W, D_A), 0.1),
        "rwkv_a0": nrm(ks[6], (L, 2, D_A), 0.5),
        "rwkv_a_up": nrm(ks[7], (L, 2, LORA_A, D_A), 0.1),
        "rwkv_k_k": 0.85 + nrm(ks[8], (L, D_A), 0.05),
        "rwkv_k_a": 1.0 + nrm(ks[9], (L, D_A), 0.05),
        "rwkv_r_k": nrm(ks[10], (L, A_HEADS, A_HEAD_DIM), 0.1),
        "rwkv_gn_g": 1.0 + nrm(ks[11], (L, D_A), 0.05),
        "rwkv_gn_b": nrm(ks[12], (L, D_A), 0.02),
        "pool_w": nrm(ks[13], (L, POOL_GROUPS, POOL_CH, POOL_CH), POOL_CH ** -0.5),
        "pool_b": nrm(ks[14], (L, D_B), 0.02),
        "pool_scale": 1.0 + nrm(ks[15], (L, D_B), 0.05),
        "proj_a": nrm(ks[16], (L, D_A, D_MODEL), DEEPNORM_BETA * D_A ** -0.5),
        "proj_b": nrm(ks[17], (L, D_B, D_MODEL), DEEPNORM_BETA * D_B ** -0.5),
        "proj_c": nrm(ks[18], (L, D_C_OUT, D_MODEL), DEEPNORM_BETA * D_C_OUT ** -0.5),
        "w_out": nrm(ks[19], (L, D_MODEL, D_MODEL), DEEPNORM_BETA * D_MODEL ** -0.5),
        "ln_g": 1.0 + nrm(ks[20], (L, D_MODEL), 0.05),
        "ln_b": nrm(ks[21], (L, D_MODEL), 0.02),
    }


def reference(x, w_in, b_in, rwkv_mu, rwkv_w0, rwkv_w_up, rwkv_a0, rwkv_a_up, rwkv_k_k, rwkv_k_a,
              rwkv_r_k, rwkv_gn_g, rwkv_gn_b, pool_w, pool_b, pool_scale, proj_a, proj_b, proj_c,
              w_out, ln_g, ln_b):
    S = x.shape[1]
    cos, sin = rope_tables(S)
    for l in range(DEPTH):
        h = jnp.einsum('bsd,dn->bsn', x, w_in[l]) + b_in[l]
        (u_a, z_a, p_b, z_b, q_c, k_c, v_c, z_c, g_a, g_b, g_c) = _split(h, SPLIT_SIZES)
        o_a = rwkv7_branch(u_a, z_a, rwkv_mu[l], rwkv_w0[l], rwkv_w_up[l], rwkv_a0[l], rwkv_a_up[l],
                           rwkv_k_k[l], rwkv_k_a[l], rwkv_r_k[l], rwkv_gn_g[l], rwkv_gn_b[l])
        o_b = pool_branch(p_b, z_b, pool_w[l], pool_b[l], pool_scale[l])
        o_c = attention_branch(q_c, k_c, v_c, z_c, cos, sin)
        merged = (jax.nn.sigmoid(g_a) * jnp.einsum('bsc,cd->bsd', o_a, proj_a[l])
                  + jax.nn.sigmoid(g_b) * jnp.einsum('bsc,cd->bsd', o_b, proj_b[l])
                  + jax.nn.sigmoid(g_c) * jnp.einsum('bsc,cd->bsd', o_c, proj_c[l]))
        out = jnp.einsum('bsd,de->bse', merged, w_out[l])
        x = (_normalize(DEEPNORM_ALPHA * x + out, LN_EPS) * ln_g[l] + ln_b[l]).astype(x.dtype)
    return x
```

```python
import functools

import jax
import jax.numpy as jnp
from jax import lax
from jax.experimental import pallas as pl
from jax.experimental.pallas import tpu as pltpu

F32 = jnp.float32
BF16 = jnp.bfloat16

D_MODEL = 1024
DEPTH = 4
HEAD_DIM = 64
LANES = 128
D_A = 1024
LORA = 64
N_SHIFT = 3 * D_A + 4 * LORA
RWKV_GN_EPS = 64e-5
POOL_HALF = (1, 2, 4, 8)
POOL_CH = 192
D_B = 768
D_C = 768
D_C_OUT = 256
ATTN_GROUPS = ((64, 1), (256, 4), (1024, 16))
ROPE_THETA = 10000.0
DEEPNORM_ALPHA = (2 * DEPTH) ** 0.25
LN_EPS = 1e-5
CHUNK = 64
Q_TILE = 128
NEG_BIG = -1e30
VMEM_LIMIT = 52 * 1024 * 1024


def _cparams(sem):
    return pltpu.CompilerParams(dimension_semantics=sem, vmem_limit_bytes=VMEM_LIMIT)


def _bdot(a, b):
    return jnp.dot(a.astype(BF16), b.astype(BF16), preferred_element_type=F32)


def _bdot_nt(a, b):
    return lax.dot_general(a.astype(BF16), b.astype(BF16), (((1,), (1,)), ((), ())),
                           preferred_element_type=F32)


def _split_dot(a, b_exact, terms):
    acc = None
    rem = a
    for _ in range(terms):
        piece = rem.astype(BF16)
        part = jnp.dot(piece, b_exact, preferred_element_type=F32)
        acc = part if acc is None else acc + part
        rem = rem - piece.astype(F32)
    return acc


def _head_ones(n):
    r = lax.broadcasted_iota(jnp.int32, (n, n), 0) // HEAD_DIM
    c = lax.broadcasted_iota(jnp.int32, (n, n), 1) // HEAD_DIM
    return (r == c).astype(BF16)


def _sigmoid(x):
    return 1.0 / (1.0 + jnp.exp(-x))


def _linear_body(x_ref, w_ref, b_ref, o_ref, *, act):
    acc = jnp.dot(x_ref[...].astype(BF16), w_ref[...], preferred_element_type=F32) + b_ref[...]
    if act == "silu":
        acc = acc * _sigmoid(acc)
    elif act == "sigmoid":
        acc = _sigmoid(acc)
    o_ref[...] = acc


def _pick_tile(n, cap):
    best = LANES
    for t in range(LANES, cap + 1, LANES):
        if n % t == 0:
            best = t
    return best


def _linear(x, w, b, act=None):
    m, k = x.shape
    n = w.shape[1]
    tm = min(512, m)
    tn = _pick_tile(n, 1664)
    return pl.pallas_call(
        functools.partial(_linear_body, act=act),
        grid=(n // tn, m // tm),
        in_specs=[pl.BlockSpec((tm, k), lambda j, i: (i, 0)),
                  pl.BlockSpec((k, tn), lambda j, i: (0, j)),
                  pl.BlockSpec((1, tn), lambda j, i: (0, j))],
        out_specs=pl.BlockSpec((tm, tn), lambda j, i: (i, j)),
        out_shape=jax.ShapeDtypeStruct((m, n), F32),
        compiler_params=_cparams(("parallel", "parallel")),
        name="linear_" + (act or "id"),
    )(x, w, b)


def _prep_body(u_ref, up_ref, un_ref, mu_ref, w0_ref, a0_ref, wup_ref, aup_ref, kk_ref, rk_ref,
               r_o, k_o, v_o, kkn_o, lw_o, a_o, bonus_o, *, tm, seq):
    i = pl.program_id(0)
    u = u_ref[...]
    row = lax.broadcasted_iota(jnp.int32, (tm, 1), 0)
    t0 = (i * tm) % seq
    prev_edge = jnp.where(t0 == 0, 0.0, up_ref[7:8, :])
    next_edge = jnp.where(t0 + tm == seq, 0.0, un_ref[0:1, :])
    prev = jnp.where(row == 0, prev_edge, pltpu.roll(u, 1, 0))
    nxt = jnp.where(row == tm - 1, next_edge, pltpu.roll(u, tm - 1, 0))
    u = u + mu_ref[0:1, :] * (prev - u) + mu_ref[1:2, :] * (nxt - u)

    r = u[:, 0:D_A]
    k = u[:, D_A:2 * D_A]
    v = u[:, 2 * D_A:3 * D_A]
    wd = jnp.tanh(u[:, 3 * D_A:3 * D_A + 2 * LORA])
    ad = u[:, 3 * D_A + 2 * LORA:N_SHIFT]
    r_o[...] = r
    k_o[...] = k
    v_o[...] = v
    for z in range(2):
        warg = w0_ref[z:z + 1, :] + _bdot(wd, wup_ref[z])
        neg = -warg
        softplus = jnp.maximum(neg, 0.0) + jnp.log(1.0 + jnp.exp(-jnp.abs(neg)))
        lw_o[z] = -jnp.exp(-softplus - 0.5)
        a_o[z] = _sigmoid(a0_ref[z:z + 1, :] + _bdot(ad, aup_ref[z]))

    ones = _head_ones(LANES)
    for cb in range(D_A // LANES):
        sl = slice(cb * LANES, (cb + 1) * LANES)
        kk = k[:, sl] * kk_ref[:, sl]
        ss = _split_dot(kk * kk, ones, 2)
        kkn_o[:, sl] = kk * lax.rsqrt(ss + 1e-12)
        rk = _split_dot(r[:, sl] * k[:, sl] * rk_ref[:, sl], ones, 2)
        bonus_o[:, sl] = rk * v[:, sl]


def _rwkv_prep(u, mu, w0, a0, wup, aup, k_k, r_k, seq):
    m = u.shape[0]
    tm = min(256, seq)
    nb8 = m // 8
    full = lambda shape: pl.BlockSpec(shape, lambda i: (0,) * len(shape))
    row_spec = pl.BlockSpec((tm, D_A), lambda i: (i, 0))
    dir_spec = pl.BlockSpec((2, tm, D_A), lambda i: (0, i, 0))
    act = jax.ShapeDtypeStruct((m, D_A), F32)
    act2 = jax.ShapeDtypeStruct((2, m, D_A), F32)
    return pl.pallas_call(
        functools.partial(_prep_body, tm=tm, seq=seq),
        grid=(m // tm,),
        in_specs=[pl.BlockSpec((tm, N_SHIFT), lambda i: (i, 0)),
                  pl.BlockSpec((8, N_SHIFT), lambda i: (jnp.maximum(i * (tm // 8) - 1, 0), 0)),
                  pl.BlockSpec((8, N_SHIFT), lambda i: (jnp.minimum((i + 1) * (tm // 8), nb8 - 1), 0)),
                  full((2, N_SHIFT)), full((2, D_A)), full((2, D_A)),
                  full((2, LANES, D_A)), full((2, LANES, D_A)), full((1, D_A)), full((1, D_A))],
        out_specs=[row_spec, row_spec, row_spec, row_spec, dir_spec, dir_spec, row_spec],
        out_shape=[act, act, act, act, act2, act2, act],
        compiler_params=_cparams(("parallel",)),
        name="rwkv_prep",
    )(u, u, u, mu, w0, a0, wup, aup, k_k, r_k)


def _stack_heads(t):
    lane = lax.broadcasted_iota(jnp.int32, t.shape, 1)
    first = lane < HEAD_DIM
    return jnp.concatenate([jnp.where(first, t, 0.0), jnp.where(first, 0.0, t)], axis=0)


def _chunk_step(r, k, v, kk, lw, a, k_a, st_ref, rev):
    c = r.shape[0]
    n2 = 2 * c
    ri = lax.broadcasted_iota(jnp.int32, (c, c), 0)
    ci = lax.broadcasted_iota(jnp.int32, (c, c), 1)
    tri = ((ci >= ri) if rev else (ci <= ri)).astype(BF16)
    cum = _split_dot_left(tri, lw)
    total = cum[0:1, :] if rev else cum[c - 1:c, :]
    kd = k * (1.0 + (a - 1.0) * k_a)
    bb = kk * a
    e_neg = jnp.exp(-cum)
    e_rem = jnp.exp(total - cum)
    xn = _stack_heads(-kk * jnp.exp(cum - lw))
    xr = _stack_heads(r * jnp.exp(cum))
    vd = _stack_heads(v)
    b2d = _stack_heads(bb * e_rem)
    k2d = _stack_heads(kd * e_rem)
    bt = bb * e_neg
    kt = kd * e_neg
    bd = jnp.concatenate([bt, bt], axis=0)
    kdd = jnp.concatenate([kt, kt], axis=0)

    rr = lax.broadcasted_iota(jnp.int32, (n2, n2), 0)
    cc = lax.broadcasted_iota(jnp.int32, (n2, n2), 1)
    same = (rr // c) == (cc // c)
    tt = rr % c
    ss = cc % c
    strict = same & ((ss > tt) if rev else (ss < tt))
    incl = same & ((ss >= tt) if rev else (ss <= tt))

    aa = _bdot_nt(jnp.concatenate([xn, xr], axis=0), jnp.concatenate([bd, kdd], axis=0))
    a_ab = jnp.where(strict, aa[:n2, :n2], 0.0)
    a_ak = jnp.where(strict, aa[:n2, n2:], 0.0)
    a_rb = jnp.where(incl, aa[n2:, :n2], 0.0)
    a_rk = jnp.where(incl, aa[n2:, n2:], 0.0)

    eye = (rr == cc).astype(F32)
    tinv = eye + a_ab
    apow = a_ab
    steps = max(1, (c - 1).bit_length())
    for _ in range(steps - 1):
        apow = _bdot(apow, apow)
        tinv = tinv + _bdot(tinv, apow)

    rhs_q = _bdot(a_ak, vd)
    pq = _bdot(tinv, jnp.concatenate([xn, rhs_q], axis=1))
    ry = _bdot(a_rb, pq)
    r2 = xr + ry[:, :LANES]
    y0 = ry[:, LANES:] + _bdot(a_rk, vd)
    gh = _bdot(b2d.T, pq)
    gm = gh[:, :LANES]
    h0 = gh[:, LANES:] + _bdot(k2d.T, vd)

    st = st_ref[...]
    ystack = _bdot(r2, st) + y0
    gcol = jnp.sum(jnp.where(rr == cc, jnp.exp(total), 0.0), axis=1, keepdims=True)
    st_ref[...] = st * gcol + _bdot(gm, st) + h0
    return ystack[:c] + ystack[c:]


def _split_dot_left(a_exact, b):
    acc = None
    rem = b
    for _ in range(3):
        piece = rem.astype(BF16)
        part = jnp.dot(a_exact, piece, preferred_element_type=F32)
        acc = part if acc is None else acc + part
        rem = rem - piece.astype(F32)
    return acc


def _scan_body(rf, kf, vf, kkf, lwf, af, rb, kb, vb, kkb, lwb, ab, ka_ref, yf_o, yb_o, st_ref):
    @pl.when(pl.program_id(2) == 0)
    def _():
        st_ref[...] = jnp.zeros_like(st_ref)

    k_a = ka_ref[...]
    yf_o[...] = _chunk_step(rf[...], kf[...], vf[...], kkf[...], lwf[...], af[...], k_a, st_ref.at[0], False)
    yb_o[...] = _chunk_step(rb[...], kb[...], vb[...], kkb[...], lwb[...], ab[...], k_a, st_ref.at[1], True)


def _rwkv_scan(r, k, v, kkn, lw, a, k_a, batch, seq):
    m = r.shape[0]
    nc = seq // CHUNK
    npair = D_A // LANES
    fwd = lambda b, p, c: (b * nc + c, p)
    bwd = lambda b, p, c: (b * nc + nc - 1 - c, p)
    blk = (CHUNK, LANES)
    fspec = pl.BlockSpec(blk, fwd)
    bspec = pl.BlockSpec(blk, bwd)
    fdir = pl.BlockSpec((None,) + blk, lambda b, p, c: (0,) + fwd(b, p, c))
    bdir = pl.BlockSpec((None,) + blk, lambda b, p, c: (1,) + bwd(b, p, c))
    out = jax.ShapeDtypeStruct((m, D_A), F32)
    return pl.pallas_call(
        _scan_body,
        grid=(batch, npair, nc),
        in_specs=[fspec, fspec, fspec, fspec, fdir, fdir, bspec, bspec, bspec, bspec, bdir, bdir,
                  pl.BlockSpec((1, LANES), lambda b, p, c: (0, p))],
        out_specs=[fspec, bspec],
        out_shape=[out, out],
        scratch_shapes=[pltpu.VMEM((2, LANES, LANES), F32)],
        compiler_params=_cparams(("parallel", "parallel", "arbitrary")),
        name="rwkv_scan",
    )(r, k, v, kkn, lw, a, r, k, v, kkn, lw, a, k_a)


POOL_PAD = 8
POOL_COLS = 256
POOL_ROWS = 256


def _pool_body(p_ref, w_ref, z0_ref, z1_ref, z2_ref, b_ref, s_ref, o_ref, pad_ref, *, seq):
    cb = pl.program_id(1)
    ncb = pl.num_programs(1)
    pad_ref[0:POOL_PAD, :] = jnp.zeros((POOL_PAD, POOL_COLS), F32)
    pad_ref[POOL_PAD + seq:, :] = jnp.zeros((POOL_PAD, POOL_COLS), F32)
    pad_ref[POOL_PAD:POOL_PAD + seq, :] = p_ref[...]

    @pl.when(cb == 0)
    def _():
        o_ref[...] = jnp.zeros_like(o_ref)

    lane = lax.broadcasted_iota(jnp.int32, (1, POOL_COLS), 1) + cb * POOL_COLS
    group = lane // POOL_CH
    w = w_ref[...]

    def rows(i, carry):
        base = pl.multiple_of(i * POOL_ROWS, POOL_ROWS)
        t = lax.broadcasted_iota(jnp.int32, (POOL_ROWS, 1), 0) + base
        win = pad_ref[pl.ds(base, POOL_ROWS + 2 * POOL_PAD), :]
        x = win[POOL_PAD:POOL_PAD + POOL_ROWS]
        acc = x
        sums = []
        d = 1
        for half in POOL_HALF:
            while d <= half:
                acc = acc + win[POOL_PAD - d:POOL_PAD - d + POOL_ROWS]
                acc = acc + win[POOL_PAD + d:POOL_PAD + d + POOL_ROWS]
                d += 1
            cnt = jnp.minimum(t + half, seq - 1) - jnp.maximum(t - half, 0) + 1
            sums.append(acc / cnt.astype(F32))
        mean = sums[3]
        for g in (2, 1, 0):
            mean = jnp.where(group == g, sums[g], mean)
        o_ref[pl.ds(base, POOL_ROWS), :] += _bdot(mean - x, w)
        return carry

    lax.fori_loop(0, seq // POOL_ROWS, rows, 0)

    @pl.when(cb == ncb - 1)
    def _():
        z = jnp.concatenate([z0_ref[...], z1_ref[...], z2_ref[...]], axis=1)
        o_ref[...] = (o_ref[...] + b_ref[...]) * s_ref[...] * z


def _pool(p, zb, w_bd, b, scale, batch, seq):
    m = p.shape[0]
    ncb = D_B // POOL_COLS
    zspec = lambda j: pl.BlockSpec((seq, POOL_COLS), lambda bi, cb: (bi, j))
    return pl.pallas_call(
        functools.partial(_pool_body, seq=seq),
        grid=(batch, ncb),
        in_specs=[pl.BlockSpec((seq, POOL_COLS), lambda bi, cb: (bi, cb)),
                  pl.BlockSpec((POOL_COLS, D_B), lambda bi, cb: (cb, 0)),
                  zspec(0), zspec(1), zspec(2),
                  pl.BlockSpec((1, D_B), lambda bi, cb: (0, 0)),
                  pl.BlockSpec((1, D_B), lambda bi, cb: (0, 0))],
        out_specs=pl.BlockSpec((seq, D_B), lambda bi, cb: (bi, 0)),
        out_shape=jax.ShapeDtypeStruct((m, D_B), F32),
        scratch_shapes=[pltpu.VMEM((seq + 2 * POOL_PAD, POOL_COLS), F32)],
        compiler_params=_cparams(("parallel", "arbitrary")),
        name="pool_mixer",
    )(p, w_bd, zb, zb, zb, b, scale)


def _rope(t, cos, sin):
    lane = lax.broadcasted_iota(jnp.int32, t.shape, 1) % HEAD_DIM
    half = HEAD_DIM // 2
    up = pltpu.roll(t, LANES - half, 1)
    down = pltpu.roll(t, half, 1)
    rot = jnp.where(lane < half, -up, down)
    return t * cos + rot * sin


def _attn_body(q0, q1, q2, k0, k1, k2, v0, v1, v2, cos_ref, sin_ref, zc_ref, o_ref, kr_ref, vb_ref, *, seq):
    qi = pl.program_id(2)
    q_refs = (q0, q1, q2)
    k_refs = (k0, k1, k2)
    v_refs = (v0, v1, v2)

    @pl.when(qi == 0)
    def _():
        for g in range(3):
            kr_ref[g] = _rope(k_refs[g][...], cos_ref[...], sin_ref[...]).astype(BF16)
            vb_ref[g] = v_refs[g][...].astype(BF16)

    qs = pl.multiple_of(qi * Q_TILE, Q_TILE)
    cos_q = cos_ref[pl.ds(qs, Q_TILE), :]
    sin_q = sin_ref[pl.ds(qs, Q_TILE), :]
    lane = lax.broadcasted_iota(jnp.int32, (Q_TILE, LANES), 1)
    first = lane < HEAD_DIM

    outs = [[None] * 3 for _ in range(2)]
    lses = [[None] * 3 for _ in range(2)]
    for g, (half_span, dil) in enumerate(ATTN_GROUPS):
        klen = min(seq, Q_TILE + 2 * ((half_span + Q_TILE - 1) // Q_TILE) * Q_TILE)
        lo = ((half_span + Q_TILE - 1) // Q_TILE) * Q_TILE
        start = pl.multiple_of(jnp.clip(qs - lo, 0, seq - klen), Q_TILE)
        q = _rope(q_refs[g][...], cos_q, sin_q) * (HEAD_DIM ** -0.5)
        keys = kr_ref[g, pl.ds(start, klen), :]
        vals = vb_ref[g, pl.ds(start, klen), :]
        delta = (lax.broadcasted_iota(jnp.int32, (Q_TILE, klen), 1)
                 - lax.broadcasted_iota(jnp.int32, (Q_TILE, klen), 0)) + (start - qs)
        valid = (jnp.abs(delta) <= half_span) & ((delta & (dil - 1)) == 0)
        for hh in range(2):
            qm = jnp.where(first if hh == 0 else ~first, q, 0.0)
            s = jnp.where(valid, _bdot_nt(qm, keys), NEG_BIG)
            mx = jnp.max(s, axis=1, keepdims=True)
            e = jnp.exp(s - mx)
            den = jnp.sum(e, axis=1, keepdims=True)
            outs[hh][g] = _bdot(e, vals) / den
            lses[hh][g] = mx + jnp.log(den)

    res = []
    for hh in range(2):
        top = jnp.maximum(jnp.maximum(lses[hh][0], lses[hh][1]), lses[hh][2])
        ws = [jnp.exp(l - top) for l in lses[hh]]
        tot = ws[0] + ws[1] + ws[2]
        res.append((ws[0] * outs[hh][0] + ws[1] * outs[hh][1] + ws[2] * outs[hh][2]) / tot)
    o_ref[...] = jnp.where(first, res[0], res[1]) * zc_ref[...]


def _attention(qkv, zc, cos, sin, batch, seq):
    m = qkv.shape[0]
    nq = seq // Q_TILE
    ncol = D_C // LANES
    per_group = D_C_OUT // LANES
    qspec = lambda g: pl.BlockSpec((Q_TILE, LANES), lambda b, jp, qi: (b * nq + qi, g * per_group + jp))
    kspec = lambda g: pl.BlockSpec((seq, LANES), lambda b, jp, qi: (b, ncol + g * per_group + jp))
    vspec = lambda g: pl.BlockSpec((seq, LANES), lambda b, jp, qi: (b, 2 * ncol + g * per_group + jp))
    tab = pl.BlockSpec((seq, LANES), lambda b, jp, qi: (0, 0))
    oz = pl.BlockSpec((Q_TILE, LANES), lambda b, jp, qi: (b * nq + qi, jp))
    return pl.pallas_call(
        functools.partial(_attn_body, seq=seq),
        grid=(batch, per_group, nq),
        in_specs=[qspec(0), qspec(1), qspec(2), kspec(0), kspec(1), kspec(2),
                  vspec(0), vspec(1), vspec(2), tab, tab, oz],
        out_specs=oz,
        out_shape=jax.ShapeDtypeStruct((m, D_C_OUT), F32),
        scratch_shapes=[pltpu.VMEM((3, seq, LANES), BF16), pltpu.VMEM((3, seq, LANES), BF16)],
        compiler_params=_cparams(("parallel", "parallel", "arbitrary")),
        name="dilated_attention",
    )(qkv, qkv, qkv, qkv, qkv, qkv, qkv, qkv, qkv, cos, sin, zc)


def _merge_body(x_ref, yf_ref, yb_ref, bonus_ref, za_ref, ob_ref, oc_ref, g_ref,
                gng_ref, gnb_ref, pa_ref, pb_ref, pc_ref, wo_ref, lng_ref, lnb_ref, o_ref):
    ones = _head_ones(LANES)
    y = yf_ref[...] + yb_ref[...]
    parts = []
    for cb in range(D_A // LANES):
        sl = slice(cb * LANES, (cb + 1) * LANES)
        yc = y[:, sl]
        mu = _split_dot(yc, ones, 2) * (1.0 / HEAD_DIM)
        d = yc - mu
        var = _split_dot(d * d, ones, 2) * (1.0 / HEAD_DIM)
        parts.append(d * lax.rsqrt(var + RWKV_GN_EPS))
    yn = jnp.concatenate(parts, axis=1)
    o_a = (yn * gng_ref[...] + gnb_ref[...] + bonus_ref[...]) * za_ref[...]
    g = g_ref[...]
    merged = (g[:, 0:D_MODEL] * _bdot(o_a, pa_ref[...])
              + g[:, D_MODEL:2 * D_MODEL] * _bdot(ob_ref[...], pb_ref[...])
              + g[:, 2 * D_MODEL:3 * D_MODEL] * _bdot(oc_ref[...], pc_ref[...]))
    t = DEEPNORM_ALPHA * x_ref[...] + _bdot(merged, wo_ref[...])
    mean = jnp.mean(t, axis=1, keepdims=True)
    d = t - mean
    var = jnp.mean(d * d, axis=1, keepdims=True)
    o_ref[...] = d * lax.rsqrt(var + LN_EPS) * lng_ref[...] + lnb_ref[...]


def _merge(x, yf, yb, bonus, za, ob, oc, gates, gn_g, gn_b, pa, pb, pc, wo, ln_g, ln_b):
    m = x.shape[0]
    tm = min(256, m)
    rows = lambda n: pl.BlockSpec((tm, n), lambda i: (i, 0))
    full = lambda a: pl.BlockSpec(a.shape, lambda i: (0, 0))
    return pl.pallas_call(
        _merge_body,
        grid=(m // tm,),
        in_specs=[rows(D_MODEL), rows(D_A), rows(D_A), rows(D_A), rows(D_A), rows(D_B), rows(D_C_OUT),
                  rows(3 * D_MODEL), full(gn_g), full(gn_b), full(pa), full(pb), full(pc), full(wo),
                  full(ln_g), full(ln_b)],
        out_specs=rows(D_MODEL),
        out_shape=jax.ShapeDtypeStruct((m, D_MODEL), F32),
        compiler_params=_cparams(("parallel",)),
        name="merge_norm",
    )(x, yf, yb, bonus, za, ob, oc, gates, gn_g, gn_b, pa, pb, pc, wo, ln_g, ln_b)


def _rope_tables(seq):
    inv = jnp.power(ROPE_THETA, -jnp.arange(0, HEAD_DIM, 2, dtype=F32) / HEAD_DIM)
    ang = jnp.arange(seq, dtype=F32)[:, None] * inv[None, :]
    ang = jnp.concatenate([ang, ang, ang, ang], axis=-1)
    return jnp.cos(ang), jnp.sin(ang)


def _pad_lora(w_up):
    z = jnp.zeros_like(w_up[0])
    return jnp.stack([jnp.concatenate([w_up[0], z], axis=0), jnp.concatenate([z, w_up[1]], axis=0)])


def _block_diag(w_g):
    g, c, _ = w_g.shape
    eye = jnp.eye(g, dtype=w_g.dtype)
    return (eye[:, None, :, None] * w_g[:, :, None, :]).reshape(g * c, g * c)


def kernel(x, w_in, b_in, rwkv_mu, rwkv_w0, rwkv_w_up, rwkv_a0, rwkv_a_up, rwkv_k_k, rwkv_k_a, rwkv_r_k, rwkv_gn_g, rwkv_gn_b, pool_w, pool_b, pool_scale, proj_a, proj_b, proj_c, w_out, ln_g, ln_b):
    batch, seq, _ = x.shape
    m = batch * seq
    cos, sin = _rope_tables(seq)
    row = lambda t: t.reshape(1, -1)
    bounds = [0]
    for width in (N_SHIFT, D_A, D_B, D_B, 3 * D_C, D_C_OUT, 3 * D_MODEL):
        bounds.append(bounds[-1] + width)
    acts = (None, "silu", None, "silu", None, "silu", "sigmoid")

    xf = x.reshape(m, D_MODEL)
    for l in range(DEPTH):
        w_l = w_in[l].astype(BF16)
        b_l = b_in[l]
        sec = [_linear(xf, w_l[:, lo:hi], row(b_l[lo:hi]), act)
               for lo, hi, act in zip(bounds[:-1], bounds[1:], acts)]
        u, za, p, zb, qkv, zc, gates = sec

        r, k, v, kkn, lw, a, bonus = _rwkv_prep(
            u, rwkv_mu[l], rwkv_w0[l], rwkv_a0[l],
            _pad_lora(rwkv_w_up[l]).astype(BF16), _pad_lora(rwkv_a_up[l]).astype(BF16),
            row(rwkv_k_k[l]), row(rwkv_r_k[l]), seq)
        yf, yb = _rwkv_scan(r, k, v, kkn, lw, a, row(rwkv_k_a[l]), batch, seq)
        ob = _pool(p, zb, _block_diag(pool_w[l]).astype(BF16), row(pool_b[l]), row(pool_scale[l]), batch, seq)
        oc = _attention(qkv, zc, cos, sin, batch, seq)
        xf = _merge(xf, yf, yb, bonus, za, ob, oc, gates, row(rwkv_gn_g[l]), row(rwkv_gn_b[l]),
                    proj_a[l].astype(BF16), proj_b[l].astype(BF16), proj_c[l].astype(BF16),
                    w_out[l].astype(BF16), row(ln_g[l]), row(ln_b[l]))
    return xf.reshape(batch, seq, D_MODEL)
```

```python
import functools

import jax
import jax.numpy as jnp
from jax import lax
from jax.experimental import pallas as pl
from jax.experimental.pallas import tpu as pltpu

F32 = jnp.float32
BF16 = jnp.bfloat16

D_MODEL = 1024
DEPTH = 4
HEAD_DIM = 64
LANES = 128
D_A = 1024
LORA = 64
N_SHIFT = 3 * D_A + 4 * LORA
RWKV_GN_EPS = 64e-5
POOL_HALF = (1, 2, 4, 8)
POOL_CH = 192
D_B = 768
D_C = 768
D_C_OUT = 256
ATTN_GROUPS = ((64, 1), (256, 4), (1024, 16))
ROPE_THETA = 10000.0
DEEPNORM_ALPHA = (2 * DEPTH) ** 0.25
LN_EPS = 1e-5
CHUNK = 64
Q_TILE = 128
NEG_BIG = -1e30
VMEM_LIMIT = 52 * 1024 * 1024


def _cparams(sem):
    return pltpu.CompilerParams(dimension_semantics=sem, vmem_limit_bytes=VMEM_LIMIT)


def _bdot(a, b):
    return jnp.dot(a.astype(BF16), b.astype(BF16), preferred_element_type=F32)


def _bdot_nt(a, b):
    return lax.dot_general(a.astype(BF16), b.astype(BF16), (((1,), (1,)), ((), ())),
                           preferred_element_type=F32)


def _split_dot(a, b_exact, terms):
    acc = None
    rem = a
    for _ in range(terms):
        piece = rem.astype(BF16)
        part = jnp.dot(piece, b_exact, preferred_element_type=F32)
        acc = part if acc is None else acc + part
        rem = rem - piece.astype(F32)
    return acc


def _head_ones(n):
    r = lax.broadcasted_iota(jnp.int32, (n, n), 0) // HEAD_DIM
    c = lax.broadcasted_iota(jnp.int32, (n, n), 1) // HEAD_DIM
    return (r == c).astype(BF16)


def _sigmoid(x):
    return 1.0 / (1.0 + jnp.exp(-x))


def _linear_body(x_ref, w_ref, b_ref, o_ref, *, act):
    acc = jnp.dot(x_ref[...].astype(BF16), w_ref[...], preferred_element_type=F32) + b_ref[...]
    if act == "silu":
        acc = acc * _sigmoid(acc)
    elif act == "sigmoid":
        acc = _sigmoid(acc)
    o_ref[...] = acc


def _pick_tile(n, cap):
    best = LANES
    for t in range(LANES, cap + 1, LANES):
        if n % t == 0:
            best = t
    return best


def _linear(x, w, b, act=None):
    m, k = x.shape
    n = w.shape[1]
    tm = min(512, m)
    tn = _pick_tile(n, 1664)
    return pl.pallas_call(
        functools.partial(_linear_body, act=act),
        grid=(n // tn, m // tm),
        in_specs=[pl.BlockSpec((tm, k), lambda j, i: (i, 0)),
                  pl.BlockSpec((k, tn), lambda j, i: (0, j)),
                  pl.BlockSpec((1, tn), lambda j, i: (0, j))],
        out_specs=pl.BlockSpec((tm, tn), lambda j, i: (i, j)),
        out_shape=jax.ShapeDtypeStruct((m, n), F32),
        compiler_params=_cparams(("parallel", "parallel")),
        name="linear_" + (act or "id"),
    )(x, w, b)


def _prep_body(u_ref, up_ref, un_ref, mu_ref, w0_ref, a0_ref, wup_ref, aup_ref, kk_ref, rk_ref,
               r_o, k_o, v_o, kkn_o, lw_o, a_o, bonus_o, *, tm, seq):
    i = pl.program_id(0)
    u = u_ref[...]
    row = lax.broadcasted_iota(jnp.int32, (tm, 1), 0)
    t0 = (i * tm) % seq
    prev_edge = jnp.where(t0 == 0, 0.0, up_ref[7:8, :])
    next_edge = jnp.where(t0 + tm == seq, 0.0, un_ref[0:1, :])
    prev = jnp.where(row == 0, prev_edge, pltpu.roll(u, 1, 0))
    nxt = jnp.where(row == tm - 1, next_edge, pltpu.roll(u, tm - 1, 0))
    u = u + mu_ref[0:1, :] * (prev - u) + mu_ref[1:2, :] * (nxt - u)

    r = u[:, 0:D_A]
    k = u[:, D_A:2 * D_A]
    v = u[:, 2 * D_A:3 * D_A]
    wd = jnp.tanh(u[:, 3 * D_A:3 * D_A + 2 * LORA])
    ad = u[:, 3 * D_A + 2 * LORA:N_SHIFT]
    r_o[...] = r
    k_o[...] = k
    v_o[...] = v
    for z in range(2):
        warg = w0_ref[z:z + 1, :] + _bdot(wd, wup_ref[z])
        neg = -warg
        softplus = jnp.maximum(neg, 0.0) + jnp.log(1.0 + jnp.exp(-jnp.abs(neg)))
        lw_o[z] = -jnp.exp(-softplus - 0.5)
        a_o[z] = _sigmoid(a0_ref[z:z + 1, :] + _bdot(ad, aup_ref[z]))

    ones = _head_ones(LANES)
    for cb in range(D_A // LANES):
        sl = slice(cb * LANES, (cb + 1) * LANES)
        kk = k[:, sl] * kk_ref[:, sl]
        ss = _split_dot(kk * kk, ones, 2)
        kkn_o[:, sl] = kk * lax.rsqrt(ss + 1e-12)
        rk = _split_dot(r[:, sl] * k[:, sl] * rk_ref[:, sl], ones, 2)
        bonus_o[:, sl] = rk * v[:, sl]


def _rwkv_prep(u, mu, w0, a0, wup, aup, k_k, r_k, seq):
    m = u.shape[0]
    tm = min(256, seq)
    nb8 = m // 8
    full = lambda shape: pl.BlockSpec(shape, lambda i: (0,) * len(shape))
    row_spec = pl.BlockSpec((tm, D_A), lambda i: (i, 0))
    dir_spec = pl.BlockSpec((2, tm, D_A), lambda i: (0, i, 0))
    act = jax.ShapeDtypeStruct((m, D_A), F32)
    act2 = jax.ShapeDtypeStruct((2, m, D_A), F32)
    return pl.pallas_call(
        functools.partial(_prep_body, tm=tm, seq=seq),
        grid=(m // tm,),
        in_specs=[pl.BlockSpec((tm, N_SHIFT), lambda i: (i, 0)),
                  pl.BlockSpec((8, N_SHIFT), lambda i: (jnp.maximum(i * (tm // 8) - 1, 0), 0)),
                  pl.BlockSpec((8, N_SHIFT), lambda i: (jnp.minimum((i + 1) * (tm // 8), nb8 - 1), 0)),
                  full((2, N_SHIFT)), full((2, D_A)), full((2, D_A)),
                  full((2, LANES, D_A)), full((2, LANES, D_A)), full((1, D_A)), full((1, D_A))],
        out_specs=[row_spec, row_spec, row_spec, row_spec, dir_spec, dir_spec, row_spec],
        out_shape=[act, act, act, act, act2, act2, act],
        compiler_params=_cparams(("parallel",)),
        name="rwkv_prep",
    )(u, u, u, mu, w0, a0, wup, aup, k_k, r_k)


SCAN_CHUNKS = 4


def _bmm(a, b):
    return lax.dot_general(a.astype(BF16), b.astype(BF16), (((2,), (1,)), ((0,), (0,))),
                           preferred_element_type=F32)


def _bmm_nt(a, b):
    return lax.dot_general(a.astype(BF16), b.astype(BF16), (((2,), (2,)), ((0,), (0,))),
                           preferred_element_type=F32)


def _stack_heads(t):
    lane = lax.broadcasted_iota(jnp.int32, t.shape, 2)
    first = lane < HEAD_DIM
    return jnp.concatenate([jnp.where(first, t, 0.0), jnp.where(first, 0.0, t)], axis=1)


def _chunk_terms(r, k, v, kk, lw, a, k_a, nfwd):
    n, c, _ = r.shape
    n2 = 2 * c
    rev = lax.broadcasted_iota(jnp.int32, (n, c, c), 0) >= nfwd
    ri = lax.broadcasted_iota(jnp.int32, (n, c, c), 1)
    ci = lax.broadcasted_iota(jnp.int32, (n, c, c), 2)
    tri = ((rev & (ci >= ri)) | (~rev & (ci <= ri))).astype(BF16)
    cum = None
    rem = lw
    for _ in range(3):
        piece = rem.astype(BF16)
        part = lax.dot_general(tri, piece, (((2,), (1,)), ((0,), (0,))), preferred_element_type=F32)
        cum = part if cum is None else cum + part
        rem = rem - piece.astype(F32)
    is_rev = lax.broadcasted_iota(jnp.int32, (n, 1, LANES), 0) >= nfwd
    total = jnp.where(is_rev, cum[:, 0:1, :], cum[:, c - 1:c, :])
    kd = k * (1.0 + (a - 1.0) * k_a)
    bb = kk * a
    e_neg = jnp.exp(-cum)
    e_rem = jnp.exp(total - cum)
    xn = _stack_heads(-kk * jnp.exp(cum - lw))
    xr = _stack_heads(r * jnp.exp(cum))
    vd = _stack_heads(v)
    b2d = _stack_heads(bb * e_rem)
    k2d = _stack_heads(kd * e_rem)
    bt = bb * e_neg
    kt = kd * e_neg
    bd = jnp.concatenate([bt, bt], axis=1)
    kdd = jnp.concatenate([kt, kt], axis=1)

    rev2 = lax.broadcasted_iota(jnp.int32, (n, n2, n2), 0) >= nfwd
    rr = lax.broadcasted_iota(jnp.int32, (n, n2, n2), 1)
    cc = lax.broadcasted_iota(jnp.int32, (n, n2, n2), 2)
    same = (rr // c) == (cc // c)
    tt = rr % c
    ss = cc % c
    diag = rr == cc
    strict = same & ((rev2 & (ss > tt)) | (~rev2 & (ss < tt)))
    incl = strict | diag

    aa = _bmm_nt(jnp.concatenate([xn, xr], axis=1), jnp.concatenate([bd, kdd], axis=1))
    a_ab = jnp.where(strict, aa[:, :n2, :n2], 0.0)
    a_ak = jnp.where(strict, aa[:, :n2, n2:], 0.0)
    a_rb = jnp.where(incl, aa[:, n2:, :n2], 0.0)
    a_rk = jnp.where(incl, aa[:, n2:, n2:], 0.0)

    tinv = jnp.where(diag, 1.0, a_ab)
    apow = a_ab
    for _ in range(max(1, (c - 1).bit_length()) - 1):
        apow = _bmm(apow, apow)
        tinv = tinv + _bmm(tinv, apow)

    rhs_q = _bmm(a_ak, vd)
    pq = _bmm(tinv, jnp.concatenate([xn, rhs_q], axis=2))
    ry = _bmm(a_rb, pq)
    r2 = xr + ry[:, :, :LANES]
    y0 = ry[:, :, LANES:] + _bmm(a_rk, vd)
    gh = _bmm(jnp.swapaxes(b2d, 1, 2), pq)
    gm = gh[:, :, :LANES]
    h0 = gh[:, :, LANES:] + _bmm(jnp.swapaxes(k2d, 1, 2), vd)
    gcol = jnp.sum(jnp.where(diag, jnp.exp(total), 0.0), axis=2, keepdims=True)
    return r2, y0, gcol, gm, h0


def _scan_body(rf, kf, vf, kkf, lwf, af, rb, kb, vb, kkb, lwb, ab, ka_ref, yf_o, yb_o, st_ref):
    @pl.when(pl.program_id(2) == 0)
    def _():
        st_ref[...] = jnp.zeros_like(st_ref)

    nch = SCAN_CHUNKS
    both = lambda f, b: jnp.concatenate([f[...].reshape(nch, CHUNK, LANES), b[...].reshape(nch, CHUNK, LANES)], axis=0)
    r2, y0, gcol, gm, h0 = _chunk_terms(both(rf, rb), both(kf, kb), both(vf, vb), both(kkf, kkb),
                                        both(lwf, lwb), both(af, ab), ka_ref[...], nch)
    st_f = st_ref[0]
    st_b = st_ref[1]
    for j in range(nch):
        jf = j
        jb = 2 * nch - 1 - j
        yf = _bdot(r2[jf], st_f) + y0[jf]
        yb = _bdot(r2[jb], st_b) + y0[jb]
        st_f = st_f * gcol[jf] + _bdot(gm[jf], st_f) + h0[jf]
        st_b = st_b * gcol[jb] + _bdot(gm[jb], st_b) + h0[jb]
        yf_o[jf * CHUNK:(jf + 1) * CHUNK, :] = yf[:CHUNK] + yf[CHUNK:]
        yb_o[(jb - nch) * CHUNK:(jb - nch + 1) * CHUNK, :] = yb[:CHUNK] + yb[CHUNK:]
    st_ref[0] = st_f
    st_ref[1] = st_b


def _rwkv_scan(r, k, v, kkn, lw, a, k_a, batch, seq):
    m = r.shape[0]
    rows = SCAN_CHUNKS * CHUNK
    ns = seq // rows
    npair = D_A // LANES
    fwd = lambda b, p, s: (b * ns + s, p)
    bwd = lambda b, p, s: (b * ns + ns - 1 - s, p)
    blk = (rows, LANES)
    fspec = pl.BlockSpec(blk, fwd)
    bspec = pl.BlockSpec(blk, bwd)
    fdir = pl.BlockSpec((None,) + blk, lambda b, p, s: (0,) + fwd(b, p, s))
    bdir = pl.BlockSpec((None,) + blk, lambda b, p, s: (1,) + bwd(b, p, s))
    out = jax.ShapeDtypeStruct((m, D_A), F32)
    return pl.pallas_call(
        _scan_body,
        grid=(batch, npair, ns),
        in_specs=[fspec, fspec, fspec, fspec, fdir, fdir, bspec, bspec, bspec, bspec, bdir, bdir,
                  pl.BlockSpec((1, LANES), lambda b, p, s: (0, p))],
        out_specs=[fspec, bspec],
        out_shape=[out, out],
        scratch_shapes=[pltpu.VMEM((2, LANES, LANES), F32)],
        compiler_params=_cparams(("parallel", "parallel", "arbitrary")),
        name="rwkv_scan",
    )(r, k, v, kkn, lw, a, r, k, v, kkn, lw, a, k_a)


POOL_PAD = 8
POOL_COLS = 256
POOL_ROWS = 256


def _pool_body(p_ref, w_ref, z0_ref, z1_ref, z2_ref, b_ref, s_ref, o_ref, pad_ref, *, seq):
    cb = pl.program_id(1)
    ncb = pl.num_programs(1)
    pad_ref[0:POOL_PAD, :] = jnp.zeros((POOL_PAD, POOL_COLS), F32)
    pad_ref[POOL_PAD + seq:, :] = jnp.zeros((POOL_PAD, POOL_COLS), F32)
    pad_ref[POOL_PAD:POOL_PAD + seq, :] = p_ref[...]

    @pl.when(cb == 0)
    def _():
        o_ref[...] = jnp.zeros_like(o_ref)

    lane = lax.broadcasted_iota(jnp.int32, (1, POOL_COLS), 1) + cb * POOL_COLS
    group = lane // POOL_CH
    w = w_ref[...]

    def rows(i, carry):
        base = pl.multiple_of(i * POOL_ROWS, POOL_ROWS)
        t = lax.broadcasted_iota(jnp.int32, (POOL_ROWS, 1), 0) + base
        win = pad_ref[pl.ds(base, POOL_ROWS + 2 * POOL_PAD), :]
        x = win[POOL_PAD:POOL_PAD + POOL_ROWS]
        acc = x
        sums = []
        d = 1
        for half in POOL_HALF:
            while d <= half:
                acc = acc + win[POOL_PAD - d:POOL_PAD - d + POOL_ROWS]
                acc = acc + win[POOL_PAD + d:POOL_PAD + d + POOL_ROWS]
                d += 1
            cnt = jnp.minimum(t + half, seq - 1) - jnp.maximum(t - half, 0) + 1
            sums.append(acc / cnt.astype(F32))
        mean = sums[3]
        for g in (2, 1, 0):
            mean = jnp.where(group == g, sums[g], mean)
        o_ref[pl.ds(base, POOL_ROWS), :] += _bdot(mean - x, w)
        return carry

    lax.fori_loop(0, seq // POOL_ROWS, rows, 0)

    @pl.when(cb == ncb - 1)
    def _():
        z = jnp.concatenate([z0_ref[...], z1_ref[...], z2_ref[...]], axis=1)
        o_ref[...] = (o_ref[...] + b_ref[...]) * s_ref[...] * z


def _pool(p, zb, w_bd, b, scale, batch, seq):
    m = p.shape[0]
    ncb = D_B // POOL_COLS
    zspec = lambda j: pl.BlockSpec((seq, POOL_COLS), lambda bi, cb: (bi, j))
    return pl.pallas_call(
        functools.partial(_pool_body, seq=seq),
        grid=(batch, ncb),
        in_specs=[pl.BlockSpec((seq, POOL_COLS), lambda bi, cb: (bi, cb)),
                  pl.BlockSpec((POOL_COLS, D_B), lambda bi, cb: (cb, 0)),
                  zspec(0), zspec(1), zspec(2),
                  pl.BlockSpec((1, D_B), lambda bi, cb: (0, 0)),
                  pl.BlockSpec((1, D_B), lambda bi, cb: (0, 0))],
        out_specs=pl.BlockSpec((seq, D_B), lambda bi, cb: (bi, 0)),
        out_shape=jax.ShapeDtypeStruct((m, D_B), F32),
        scratch_shapes=[pltpu.VMEM((seq + 2 * POOL_PAD, POOL_COLS), F32)],
        compiler_params=_cparams(("parallel", "arbitrary")),
        name="pool_mixer",
    )(p, w_bd, zb, zb, zb, b, scale)


def _rope(t, cos, sin):
    lane = lax.broadcasted_iota(jnp.int32, t.shape, 1) % HEAD_DIM
    half = HEAD_DIM // 2
    up = pltpu.roll(t, LANES - half, 1)
    down = pltpu.roll(t, half, 1)
    rot = jnp.where(lane < half, -up, down)
    return t * cos + rot * sin


def _attn_body(q0, q1, q2, k0, k1, k2, v0, v1, v2, cos_ref, sin_ref, zc_ref, o_ref, kr_ref, vb_ref, *, seq):
    qi = pl.program_id(2)
    q_refs = (q0, q1, q2)
    k_refs = (k0, k1, k2)
    v_refs = (v0, v1, v2)

    @pl.when(qi == 0)
    def _():
        for g in range(3):
            kr_ref[g] = _rope(k_refs[g][...], cos_ref[...], sin_ref[...]).astype(BF16)
            vb_ref[g] = v_refs[g][...].astype(BF16)

    qs = pl.multiple_of(qi * Q_TILE, Q_TILE)
    cos_q = cos_ref[pl.ds(qs, Q_TILE), :]
    sin_q = sin_ref[pl.ds(qs, Q_TILE), :]
    lane = lax.broadcasted_iota(jnp.int32, (Q_TILE, LANES), 1)
    first = lane < HEAD_DIM

    outs = [[None] * 3 for _ in range(2)]
    lses = [[None] * 3 for _ in range(2)]
    for g, (half_span, dil) in enumerate(ATTN_GROUPS):
        klen = min(seq, Q_TILE + 2 * ((half_span + Q_TILE - 1) // Q_TILE) * Q_TILE)
        lo = ((half_span + Q_TILE - 1) // Q_TILE) * Q_TILE
        start = pl.multiple_of(jnp.clip(qs - lo, 0, seq - klen), Q_TILE)
        q = _rope(q_refs[g][...], cos_q, sin_q) * (HEAD_DIM ** -0.5)
        keys = kr_ref[g, pl.ds(start, klen), :]
        vals = vb_ref[g, pl.ds(start, klen), :]
        delta = (lax.broadcasted_iota(jnp.int32, (Q_TILE, klen), 1)
                 - lax.broadcasted_iota(jnp.int32, (Q_TILE, klen), 0)) + (start - qs)
        valid = (jnp.abs(delta) <= half_span) & ((delta & (dil - 1)) == 0)
        for hh in range(2):
            qm = jnp.where(first if hh == 0 else ~first, q, 0.0)
            s = jnp.where(valid, _bdot_nt(qm, keys), NEG_BIG)
            mx = jnp.max(s, axis=1, keepdims=True)
            e = jnp.exp(s - mx)
            den = jnp.sum(e, axis=1, keepdims=True)
            outs[hh][g] = _bdot(e, vals) / den
            lses[hh][g] = mx + jnp.log(den)

    res = []
    for hh in range(2):
        top = jnp.maximum(jnp.maximum(lses[hh][0], lses[hh][1]), lses[hh][2])
        ws = [jnp.exp(l - top) for l in lses[hh]]
        tot = ws[0] + ws[1] + ws[2]
        res.append((ws[0] * outs[hh][0] + ws[1] * outs[hh][1] + ws[2] * outs[hh][2]) / tot)
    o_ref[...] = jnp.where(first, res[0], res[1]) * zc_ref[...]


def _attention(qkv, zc, cos, sin, batch, seq):
    m = qkv.shape[0]
    nq = seq // Q_TILE
    ncol = D_C // LANES
    per_group = D_C_OUT // LANES
    qspec = lambda g: pl.BlockSpec((Q_TILE, LANES), lambda b, jp, qi: (b * nq + qi, g * per_group + jp))
    kspec = lambda g: pl.BlockSpec((seq, LANES), lambda b, jp, qi: (b, ncol + g * per_group + jp))
    vspec = lambda g: pl.BlockSpec((seq, LANES), lambda b, jp, qi: (b, 2 * ncol + g * per_group + jp))
    tab = pl.BlockSpec((seq, LANES), lambda b, jp, qi: (0, 0))
    oz = pl.BlockSpec((Q_TILE, LANES), lambda b, jp, qi: (b * nq + qi, jp))
    return pl.pallas_call(
        functools.partial(_attn_body, seq=seq),
        grid=(batch, per_group, nq),
        in_specs=[qspec(0), qspec(1), qspec(2), kspec(0), kspec(1), kspec(2),
                  vspec(0), vspec(1), vspec(2), tab, tab, oz],
        out_specs=oz,
        out_shape=jax.ShapeDtypeStruct((m, D_C_OUT), F32),
        scratch_shapes=[pltpu.VMEM((3, seq, LANES), BF16), pltpu.VMEM((3, seq, LANES), BF16)],
        compiler_params=_cparams(("parallel", "parallel", "arbitrary")),
        name="dilated_attention",
    )(qkv, qkv, qkv, qkv, qkv, qkv, qkv, qkv, qkv, cos, sin, zc)


def _merge_body(x_ref, yf_ref, yb_ref, bonus_ref, za_ref, ob_ref, oc_ref, g_ref,
                gng_ref, gnb_ref, pa_ref, pb_ref, pc_ref, wo_ref, lng_ref, lnb_ref, o_ref):
    ones = _head_ones(LANES)
    y = yf_ref[...] + yb_ref[...]
    parts = []
    for cb in range(D_A // LANES):
        sl = slice(cb * LANES, (cb + 1) * LANES)
        yc = y[:, sl]
        mu = _split_dot(yc, ones, 2) * (1.0 / HEAD_DIM)
        d = yc - mu
        var = _split_dot(d * d, ones, 2) * (1.0 / HEAD_DIM)
        parts.append(d * lax.rsqrt(var + RWKV_GN_EPS))
    yn = jnp.concatenate(parts, axis=1)
    o_a = (yn * gng_ref[...] + gnb_ref[...] + bonus_ref[...]) * za_ref[...]
    g = g_ref[...]
    merged = (g[:, 0:D_MODEL] * _bdot(o_a, pa_ref[...])
              + g[:, D_MODEL:2 * D_MODEL] * _bdot(ob_ref[...], pb_ref[...])
              + g[:, 2 * D_MODEL:3 * D_MODEL] * _bdot(oc_ref[...], pc_ref[...]))
    t = DEEPNORM_ALPHA * x_ref[...] + _bdot(merged, wo_ref[...])
    mean = jnp.mean(t, axis=1, keepdims=True)
    d = t - mean
    var = jnp.mean(d * d, axis=1, keepdims=True)
    o_ref[...] = d * lax.rsqrt(var + LN_EPS) * lng_ref[...] + lnb_ref[...]


def _merge(x, yf, yb, bonus, za, ob, oc, gates, gn_g, gn_b, pa, pb, pc, wo, ln_g, ln_b):
    m = x.shape[0]
    tm = min(256, m)
    rows = lambda n: pl.BlockSpec((tm, n), lambda i: (i, 0))
    full = lambda a: pl.BlockSpec(a.shape, lambda i: (0, 0))
    return pl.pallas_call(
        _merge_body,
        grid=(m // tm,),
        in_specs=[rows(D_MODEL), rows(D_A), rows(D_A), rows(D_A), rows(D_A), rows(D_B), rows(D_C_OUT),
                  rows(3 * D_MODEL), full(gn_g), full(gn_b), full(pa), full(pb), full(pc), full(wo),
                  full(ln_g), full(ln_b)],
        out_specs=rows(D_MODEL),
        out_shape=jax.ShapeDtypeStruct((m, D_MODEL), F32),
        compiler_params=_cparams(("parallel",)),
        name="merge_norm",
    )(x, yf, yb, bonus, za, ob, oc, gates, gn_g, gn_b, pa, pb, pc, wo, ln_g, ln_b)


def _rope_tables(seq):
    inv = jnp.power(ROPE_THETA, -jnp.arange(0, HEAD_DIM, 2, dtype=F32) / HEAD_DIM)
    ang = jnp.arange(seq, dtype=F32)[:, None] * inv[None, :]
    ang = jnp.concatenate([ang, ang, ang, ang], axis=-1)
    return jnp.cos(ang), jnp.sin(ang)


def _pad_lora(w_up):
    z = jnp.zeros_like(w_up[0])
    return jnp.stack([jnp.concatenate([w_up[0], z], axis=0), jnp.concatenate([z, w_up[1]], axis=0)])


def _block_diag(w_g):
    g, c, _ = w_g.shape
    eye = jnp.eye(g, dtype=w_g.dtype)
    return (eye[:, None, :, None] * w_g[:, :, None, :]).reshape(g * c, g * c)


def kernel(x, w_in, b_in, rwkv_mu, rwkv_w0, rwkv_w_up, rwkv_a0, rwkv_a_up, rwkv_k_k, rwkv_k_a, rwkv_r_k, rwkv_gn_g, rwkv_gn_b, pool_w, pool_b, pool_scale, proj_a, proj_b, proj_c, w_out, ln_g, ln_b):
    batch, seq, _ = x.shape
    m = batch * seq
    cos, sin = _rope_tables(seq)
    row = lambda t: t.reshape(1, -1)
    bounds = [0]
    for width in (N_SHIFT, D_A, D_B, D_B, 3 * D_C, D_C_OUT, 3 * D_MODEL):
        bounds.append(bounds[-1] + width)
    acts = (None, "silu", None, "silu", None, "silu", "sigmoid")

    xf = x.reshape(m, D_MODEL)
    for l in range(DEPTH):
        w_l = w_in[l].astype(BF16)
        b_l = b_in[l]
        sec = [_linear(xf, w_l[:, lo:hi], row(b_l[lo:hi]), act)
               for lo, hi, act in zip(bounds[:-1], bounds[1:], acts)]
        u, za, p, zb, qkv, zc, gates = sec

        r, k, v, kkn, lw, a, bonus = _rwkv_prep(
            u, rwkv_mu[l], rwkv_w0[l], rwkv_a0[l],
            _pad_lora(rwkv_w_up[l]).astype(BF16), _pad_lora(rwkv_a_up[l]).astype(BF16),
            row(rwkv_k_k[l]), row(rwkv_r_k[l]), seq)
        yf, yb = _rwkv_scan(r, k, v, kkn, lw, a, row(rwkv_k_a[l]), batch, seq)
        ob = _pool(p, zb, _block_diag(pool_w[l]).astype(BF16), row(pool_b[l]), row(pool_scale[l]), batch, seq)
        oc = _attention(qkv, zc, cos, sin, batch, seq)
        xf = _merge(xf, yf, yb, bonus, za, ob, oc, gates, row(rwkv_gn_g[l]), row(rwkv_gn_b[l]),
                    proj_a[l].astype(BF16), proj_b[l].astype(BF16), proj_c[l].astype(BF16),
                    w_out[l].astype(BF16), row(ln_g[l]), row(ln_b[l]))
    return xf.reshape(batch, seq, D_MODEL)
```

```python
import functools

import jax
import jax.numpy as jnp
from jax import lax
from jax.experimental import pallas as pl
from jax.experimental.pallas import tpu as pltpu

F32 = jnp.float32
BF16 = jnp.bfloat16

D_MODEL = 1024
DEPTH = 4
HEAD_DIM = 64
LANES = 128
D_A = 1024
LORA = 64
N_SHIFT = 3 * D_A + 4 * LORA
RWKV_GN_EPS = 64e-5
POOL_HALF = (1, 2, 4, 8)
POOL_CH = 192
D_B = 768
D_C = 768
D_C_OUT = 256
ATTN_GROUPS = ((64, 1), (256, 4), (1024, 16))
ROPE_THETA = 10000.0
DEEPNORM_ALPHA = (2 * DEPTH) ** 0.25
LN_EPS = 1e-5
CHUNK = 64
Q_TILE = 128
NEG_BIG = -1e30
VMEM_LIMIT = 52 * 1024 * 1024


def _cparams(sem):
    return pltpu.CompilerParams(dimension_semantics=sem, vmem_limit_bytes=VMEM_LIMIT)


def _bdot(a, b):
    return jnp.dot(a.astype(BF16), b.astype(BF16), preferred_element_type=F32)


def _bdot_nt(a, b):
    return lax.dot_general(a.astype(BF16), b.astype(BF16), (((1,), (1,)), ((), ())),
                           preferred_element_type=F32)


def _split_dot(a, b_exact, terms):
    acc = None
    rem = a
    for _ in range(terms):
        piece = rem.astype(BF16)
        part = jnp.dot(piece, b_exact, preferred_element_type=F32)
        acc = part if acc is None else acc + part
        rem = rem - piece.astype(F32)
    return acc


def _head_ones(n):
    r = lax.broadcasted_iota(jnp.int32, (n, n), 0) // HEAD_DIM
    c = lax.broadcasted_iota(jnp.int32, (n, n), 1) // HEAD_DIM
    return (r == c).astype(BF16)


def _sigmoid(x):
    return 1.0 / (1.0 + jnp.exp(-x))


def _linear_body(x_ref, w_ref, b_ref, o_ref, *, act):
    acc = jnp.dot(x_ref[...].astype(BF16), w_ref[...], preferred_element_type=F32) + b_ref[...]
    if act == "silu":
        acc = acc * _sigmoid(acc)
    elif act == "sigmoid":
        acc = _sigmoid(acc)
    o_ref[...] = acc


def _pick_tile(n, cap):
    best = LANES
    for t in range(LANES, cap + 1, LANES):
        if n % t == 0:
            best = t
    return best


def _linear(x, w, b, act=None):
    m, k = x.shape
    n = w.shape[1]
    tm = min(512, m)
    tn = _pick_tile(n, 1664)
    return pl.pallas_call(
        functools.partial(_linear_body, act=act),
        grid=(n // tn, m // tm),
        in_specs=[pl.BlockSpec((tm, k), lambda j, i: (i, 0)),
                  pl.BlockSpec((k, tn), lambda j, i: (0, j)),
                  pl.BlockSpec((1, tn), lambda j, i: (0, j))],
        out_specs=pl.BlockSpec((tm, tn), lambda j, i: (i, j)),
        out_shape=jax.ShapeDtypeStruct((m, n), F32),
        compiler_params=_cparams(("parallel", "parallel")),
        name="linear_" + (act or "id"),
    )(x, w, b)


def _prep_body(u_ref, up_ref, un_ref, mu_ref, w0_ref, a0_ref, wup_ref, aup_ref, kk_ref, rk_ref,
               r_o, k_o, v_o, kkn_o, lw_o, a_o, bonus_o, *, tm, seq):
    i = pl.program_id(0)
    u = u_ref[...]
    row = lax.broadcasted_iota(jnp.int32, (tm, 1), 0)
    t0 = (i * tm) % seq
    prev_edge = jnp.where(t0 == 0, 0.0, up_ref[7:8, :])
    next_edge = jnp.where(t0 + tm == seq, 0.0, un_ref[0:1, :])
    prev = jnp.where(row == 0, prev_edge, pltpu.roll(u, 1, 0))
    nxt = jnp.where(row == tm - 1, next_edge, pltpu.roll(u, tm - 1, 0))
    u = u + mu_ref[0:1, :] * (prev - u) + mu_ref[1:2, :] * (nxt - u)

    r = u[:, 0:D_A]
    k = u[:, D_A:2 * D_A]
    v = u[:, 2 * D_A:3 * D_A]
    wd = jnp.tanh(u[:, 3 * D_A:3 * D_A + 2 * LORA])
    ad = u[:, 3 * D_A + 2 * LORA:N_SHIFT]
    r_o[...] = r
    k_o[...] = k
    v_o[...] = v
    for z in range(2):
        warg = w0_ref[z:z + 1, :] + _bdot(wd, wup_ref[z])
        neg = -warg
        softplus = jnp.maximum(neg, 0.0) + jnp.log(1.0 + jnp.exp(-jnp.abs(neg)))
        lw_o[z] = -jnp.exp(-softplus - 0.5)
        a_o[z] = _sigmoid(a0_ref[z:z + 1, :] + _bdot(ad, aup_ref[z]))

    ones = _head_ones(LANES)
    for cb in range(D_A // LANES):
        sl = slice(cb * LANES, (cb + 1) * LANES)
        kk = k[:, sl] * kk_ref[:, sl]
        ss = _split_dot(kk * kk, ones, 2)
        kkn_o[:, sl] = kk * lax.rsqrt(ss + 1e-12)
        rk = _split_dot(r[:, sl] * k[:, sl] * rk_ref[:, sl], ones, 2)
        bonus_o[:, sl] = rk * v[:, sl]


def _rwkv_prep(u, mu, w0, a0, wup, aup, k_k, r_k, seq):
    m = u.shape[0]
    tm = min(256, seq)
    nb8 = m // 8
    full = lambda shape: pl.BlockSpec(shape, lambda i: (0,) * len(shape))
    row_spec = pl.BlockSpec((tm, D_A), lambda i: (i, 0))
    dir_spec = pl.BlockSpec((2, tm, D_A), lambda i: (0, i, 0))
    act = jax.ShapeDtypeStruct((m, D_A), F32)
    act2 = jax.ShapeDtypeStruct((2, m, D_A), F32)
    return pl.pallas_call(
        functools.partial(_prep_body, tm=tm, seq=seq),
        grid=(m // tm,),
        in_specs=[pl.BlockSpec((tm, N_SHIFT), lambda i: (i, 0)),
                  pl.BlockSpec((8, N_SHIFT), lambda i: (jnp.maximum(i * (tm // 8) - 1, 0), 0)),
                  pl.BlockSpec((8, N_SHIFT), lambda i: (jnp.minimum((i + 1) * (tm // 8), nb8 - 1), 0)),
                  full((2, N_SHIFT)), full((2, D_A)), full((2, D_A)),
                  full((2, LANES, D_A)), full((2, LANES, D_A)), full((1, D_A)), full((1, D_A))],
        out_specs=[row_spec, row_spec, row_spec, row_spec, dir_spec, dir_spec, row_spec],
        out_shape=[act, act, act, act, act2, act2, act],
        compiler_params=_cparams(("parallel",)),
        name="rwkv_prep",
    )(u, u, u, mu, w0, a0, wup, aup, k_k, r_k)


SCAN_CHUNKS = 4


def _bmm(a, b):
    return lax.dot_general(a.astype(BF16), b.astype(BF16), (((2,), (1,)), ((0,), (0,))),
                           preferred_element_type=F32)


def _bmm_nt(a, b):
    return lax.dot_general(a.astype(BF16), b.astype(BF16), (((2,), (2,)), ((0,), (0,))),
                           preferred_element_type=F32)


def _stack_heads(t):
    lane = lax.broadcasted_iota(jnp.int32, t.shape, 2)
    first = lane < HEAD_DIM
    return jnp.concatenate([jnp.where(first, t, 0.0), jnp.where(first, 0.0, t)], axis=1)


def _chunk_terms(r, k, v, kk, lw, a, k_a, nfwd):
    n, c, _ = r.shape
    n2 = 2 * c
    rev = lax.broadcasted_iota(jnp.int32, (n, c, 3 * c), 0) >= nfwd
    ri = lax.broadcasted_iota(jnp.int32, (n, c, 3 * c), 1)
    ci = lax.broadcasted_iota(jnp.int32, (n, c, 3 * c), 2) % c
    tri = ((rev & (ci >= ri)) | (~rev & (ci <= ri))).astype(BF16)
    pieces = []
    rem = lw
    for _ in range(3):
        pieces.append(rem.astype(BF16))
        rem = rem - pieces[-1].astype(F32)
    cum = _bmm(tri, jnp.concatenate(pieces, axis=1))
    is_rev = lax.broadcasted_iota(jnp.int32, (n, 1, LANES), 0) >= nfwd
    total = jnp.where(is_rev, cum[:, 0:1, :], cum[:, c - 1:c, :])
    kd = k * (1.0 + (a - 1.0) * k_a)
    bb = kk * a
    e_neg = jnp.exp(-cum)
    e_rem = jnp.exp(total - cum)
    xn = _stack_heads(-kk * jnp.exp(cum - lw))
    xr = _stack_heads(r * jnp.exp(cum))
    vd = _stack_heads(v)
    b2d = _stack_heads(bb * e_rem)
    k2d = _stack_heads(kd * e_rem)
    bt = bb * e_neg
    kt = kd * e_neg
    bd = jnp.concatenate([bt, bt], axis=1)
    kdd = jnp.concatenate([kt, kt], axis=1)

    rev2 = lax.broadcasted_iota(jnp.int32, (n, n2, n2), 0) >= nfwd
    rr = lax.broadcasted_iota(jnp.int32, (n, n2, n2), 1)
    cc = lax.broadcasted_iota(jnp.int32, (n, n2, n2), 2)
    same = (rr // c) == (cc // c)
    tt = rr % c
    ss = cc % c
    diag = rr == cc
    strict = same & ((rev2 & (ss > tt)) | (~rev2 & (ss < tt)))
    incl = strict | diag

    aa = _bmm_nt(jnp.concatenate([xn, xr], axis=1), jnp.concatenate([bd, kdd], axis=1))
    a_ab = jnp.where(strict, aa[:, :n2, :n2], 0.0)
    a_ak = jnp.where(strict, aa[:, :n2, n2:], 0.0)
    a_rb = jnp.where(incl, aa[:, n2:, :n2], 0.0)
    a_rk = jnp.where(incl, aa[:, n2:, n2:], 0.0)

    tinv = jnp.where(diag, 1.0, a_ab)
    apow = _bmm(a_ab, a_ab)
    for _ in range(max(1, (c - 1).bit_length()) - 2):
        both = _bmm(apow, jnp.concatenate([apow, tinv], axis=2))
        apow = both[:, :, :n2]
        tinv = tinv + both[:, :, n2:]
    tinv = tinv + _bmm(apow, tinv)

    rhs_q = _bmm(a_ak, vd)
    pq = _bmm(tinv, jnp.concatenate([xn, rhs_q], axis=2))
    lhs = jnp.concatenate([jnp.concatenate([a_rb, a_rk], axis=2),
                           jnp.concatenate([jnp.swapaxes(b2d, 1, 2), jnp.swapaxes(k2d, 1, 2)], axis=2)], axis=1)
    rhs = jnp.concatenate([pq, jnp.concatenate([jnp.zeros_like(vd), vd], axis=2)], axis=1)
    out = _bmm(lhs, rhs)
    r2 = xr + out[:, :n2, :LANES]
    y0 = out[:, :n2, LANES:]
    gm = out[:, n2:, :LANES]
    h0 = out[:, n2:, LANES:]
    gcol = jnp.sum(jnp.where(diag, jnp.exp(total), 0.0), axis=2, keepdims=True)
    return jnp.concatenate([r2, gm], axis=1), y0, gcol, h0


def _scan_body(rf, kf, vf, kkf, lwf, af, rb, kb, vb, kkb, lwb, ab, ka_ref, yf_o, yb_o, st_ref):
    @pl.when(pl.program_id(2) == 0)
    def _():
        st_ref[...] = jnp.zeros_like(st_ref)

    nch = SCAN_CHUNKS
    both = lambda f, b: jnp.concatenate([f[...].reshape(nch, CHUNK, LANES), b[...].reshape(nch, CHUNK, LANES)], axis=0)
    rg, y0, gcol, h0 = _chunk_terms(both(rf, rb), both(kf, kb), both(vf, vb), both(kkf, kkb),
                                    both(lwf, lwb), both(af, ab), ka_ref[...], nch)
    n2 = 2 * CHUNK
    st_f = st_ref[0]
    st_b = st_ref[1]
    for j in range(nch):
        jf = j
        jb = 2 * nch - 1 - j
        of = _bdot(rg[jf], st_f)
        ob = _bdot(rg[jb], st_b)
        yf = of[:n2] + y0[jf]
        yb = ob[:n2] + y0[jb]
        st_f = st_f * gcol[jf] + of[n2:] + h0[jf]
        st_b = st_b * gcol[jb] + ob[n2:] + h0[jb]
        yf_o[jf * CHUNK:(jf + 1) * CHUNK, :] = yf[:CHUNK] + yf[CHUNK:]
        yb_o[(jb - nch) * CHUNK:(jb - nch + 1) * CHUNK, :] = yb[:CHUNK] + yb[CHUNK:]
    st_ref[0] = st_f
    st_ref[1] = st_b


def _rwkv_scan(r, k, v, kkn, lw, a, k_a, batch, seq):
    m = r.shape[0]
    rows = SCAN_CHUNKS * CHUNK
    ns = seq // rows
    npair = D_A // LANES
    fwd = lambda b, p, s: (b * ns + s, p)
    bwd = lambda b, p, s: (b * ns + ns - 1 - s, p)
    blk = (rows, LANES)
    fspec = pl.BlockSpec(blk, fwd)
    bspec = pl.BlockSpec(blk, bwd)
    fdir = pl.BlockSpec((None,) + blk, lambda b, p, s: (0,) + fwd(b, p, s))
    bdir = pl.BlockSpec((None,) + blk, lambda b, p, s: (1,) + bwd(b, p, s))
    out = jax.ShapeDtypeStruct((m, D_A), F32)
    return pl.pallas_call(
        _scan_body,
        grid=(batch, npair, ns),
        in_specs=[fspec, fspec, fspec, fspec, fdir, fdir, bspec, bspec, bspec, bspec, bdir, bdir,
                  pl.BlockSpec((1, LANES), lambda b, p, s: (0, p))],
        out_specs=[fspec, bspec],
        out_shape=[out, out],
        scratch_shapes=[pltpu.VMEM((2, LANES, LANES), F32)],
        compiler_params=_cparams(("parallel", "parallel", "arbitrary")),
        name="rwkv_scan",
    )(r, k, v, kkn, lw, a, r, k, v, kkn, lw, a, k_a)


POOL_PAD = 8
POOL_COLS = 256
POOL_ROWS = 256


def _pool_body(p_ref, w_ref, z0_ref, z1_ref, z2_ref, b_ref, s_ref, o_ref, pad_ref, *, seq):
    cb = pl.program_id(1)
    ncb = pl.num_programs(1)
    pad_ref[0:POOL_PAD, :] = jnp.zeros((POOL_PAD, POOL_COLS), F32)
    pad_ref[POOL_PAD + seq:, :] = jnp.zeros((POOL_PAD, POOL_COLS), F32)
    pad_ref[POOL_PAD:POOL_PAD + seq, :] = p_ref[...]

    @pl.when(cb == 0)
    def _():
        o_ref[...] = jnp.zeros_like(o_ref)

    lane = lax.broadcasted_iota(jnp.int32, (1, POOL_COLS), 1) + cb * POOL_COLS
    group = lane // POOL_CH
    w = w_ref[...]

    def rows(i, carry):
        base = pl.multiple_of(i * POOL_ROWS, POOL_ROWS)
        t = lax.broadcasted_iota(jnp.int32, (POOL_ROWS, 1), 0) + base
        win = pad_ref[pl.ds(base, POOL_ROWS + 2 * POOL_PAD), :]
        x = win[POOL_PAD:POOL_PAD + POOL_ROWS]
        acc = x
        sums = []
        d = 1
        for half in POOL_HALF:
            while d <= half:
                acc = acc + win[POOL_PAD - d:POOL_PAD - d + POOL_ROWS]
                acc = acc + win[POOL_PAD + d:POOL_PAD + d + POOL_ROWS]
                d += 1
            cnt = jnp.minimum(t + half, seq - 1) - jnp.maximum(t - half, 0) + 1
            sums.append(acc / cnt.astype(F32))
        mean = sums[3]
        for g in (2, 1, 0):
            mean = jnp.where(group == g, sums[g], mean)
        o_ref[pl.ds(base, POOL_ROWS), :] += _bdot(mean - x, w)
        return carry

    lax.fori_loop(0, seq // POOL_ROWS, rows, 0)

    @pl.when(cb == ncb - 1)
    def _():
        z = jnp.concatenate([z0_ref[...], z1_ref[...], z2_ref[...]], axis=1)
        o_ref[...] = (o_ref[...] + b_ref[...]) * s_ref[...] * z


def _pool(p, zb, w_bd, b, scale, batch, seq):
    m = p.shape[0]
    ncb = D_B // POOL_COLS
    zspec = lambda j: pl.BlockSpec((seq, POOL_COLS), lambda bi, cb: (bi, j))
    return pl.pallas_call(
        functools.partial(_pool_body, seq=seq),
        grid=(batch, ncb),
        in_specs=[pl.BlockSpec((seq, POOL_COLS), lambda bi, cb: (bi, cb)),
                  pl.BlockSpec((POOL_COLS, D_B), lambda bi, cb: (cb, 0)),
                  zspec(0), zspec(1), zspec(2),
                  pl.BlockSpec((1, D_B), lambda bi, cb: (0, 0)),
                  pl.BlockSpec((1, D_B), lambda bi, cb: (0, 0))],
        out_specs=pl.BlockSpec((seq, D_B), lambda bi, cb: (bi, 0)),
        out_shape=jax.ShapeDtypeStruct((m, D_B), F32),
        scratch_shapes=[pltpu.VMEM((seq + 2 * POOL_PAD, POOL_COLS), F32)],
        compiler_params=_cparams(("parallel", "arbitrary")),
        name="pool_mixer",
    )(p, w_bd, zb, zb, zb, b, scale)


def _rope_matrix():
    half = HEAD_DIM // 2
    src = lax.broadcasted_iota(jnp.int32, (LANES, LANES), 0)
    dst = lax.broadcasted_iota(jnp.int32, (LANES, LANES), 1)
    low = (dst % HEAD_DIM) < half
    return jnp.where(low & (src == dst + half), -1.0, jnp.where(~low & (src == dst - half), 1.0, 0.0)).astype(BF16)


def _rope(t, cos, sin, rot_matrix):
    return t * cos + _split_dot(t, rot_matrix, 2) * sin


ATTN_SIDE = 64
ATTN_TILES = 4
COMBINE_ROWS = 256


def _attn_body(q0, q1, q2, k0, k1, k2, v0, v1, v2, cos_ref, sin_ref, zc_ref, o_ref,
               qp_ref, kp_ref, vp_ref, op_ref, lp_ref, on_ref, ln_ref, *, seq):
    q_refs = (q0, q1, q2)
    k_refs = (k0, k1, k2)
    v_refs = (v0, v1, v2)
    scale = HEAD_DIM ** -0.5
    rot_matrix = _rope_matrix()
    for g, (half_span, dil) in enumerate(ATTN_GROUPS):
        cls = seq // dil
        for r in range(dil):
            rows = pl.ds(r, cls, stride=dil) if dil > 1 else pl.ds(0, seq)
            dst = pl.ds(r * cls, cls)
            cos = cos_ref[rows, :]
            sin = sin_ref[rows, :]
            qp_ref[g, dst, :] = (_rope(q_refs[g][rows, :], cos, sin, rot_matrix) * scale).astype(BF16)
            kp_ref[g, dst, :] = _rope(k_refs[g][rows, :], cos, sin, rot_matrix).astype(BF16)
            vp_ref[g, dst, :] = v_refs[g][rows, :].astype(BF16)

    nt = ATTN_TILES
    first = lax.broadcasted_iota(jnp.int32, (nt, Q_TILE, LANES), 2) < HEAD_DIM
    for g, (half_span, dil) in enumerate(ATTN_GROUPS):
        cls = seq // dil
        klen = min(cls, Q_TILE + 2 * ATTN_SIDE)
        rel = (lax.broadcasted_iota(jnp.int32, (Q_TILE, klen), 1)
               - lax.broadcasted_iota(jnp.int32, (Q_TILE, klen), 0))

        def tiles(it, carry, g=g, cls=cls, klen=klen, rel=rel):
            row0 = pl.multiple_of(it * (nt * Q_TILE), nt * Q_TILE)
            keys, vals, valid = [], [], []
            for j in range(nt):
                qrow = row0 + j * Q_TILE
                base = (qrow // cls) * cls
                krow = pl.multiple_of(base + jnp.clip(qrow - base - ATTN_SIDE, 0, cls - klen), ATTN_SIDE)
                keys.append(kp_ref[g, pl.ds(krow, klen), :])
                vals.append(vp_ref[g, pl.ds(krow, klen), :])
                valid.append(jnp.abs(rel + (krow - qrow)) <= ATTN_SIDE)
            q = qp_ref[g, pl.ds(row0, nt * Q_TILE), :].reshape(nt, Q_TILE, LANES)
            zero = jnp.zeros_like(q)
            qm = jnp.concatenate([jnp.where(first, q, zero), jnp.where(first, zero, q)], axis=0)
            keys = jnp.stack(keys + keys)
            vals = jnp.stack(vals + vals)
            valid = jnp.stack(valid + valid)
            s = jnp.where(valid, _bmm_nt(qm, keys), NEG_BIG)
            mx = jnp.max(s, axis=2, keepdims=True)
            e = jnp.exp(s - mx)
            den = jnp.sum(e, axis=2, keepdims=True)
            o = _bmm(e, vals) / den
            lse = jnp.broadcast_to(mx + jnp.log(den), o.shape)
            rows = pl.ds(row0, nt * Q_TILE)
            op_ref[g, rows, :] = jnp.where(first, o[:nt], o[nt:]).reshape(nt * Q_TILE, LANES)
            lp_ref[g, rows, :] = jnp.where(first, lse[:nt], lse[nt:]).reshape(nt * Q_TILE, LANES)
            return carry

        lax.fori_loop(0, seq // (nt * Q_TILE), tiles, 0)
        if dil > 1:
            for r in range(dil):
                src = pl.ds(r * cls, cls)
                on_ref[g - 1, pl.ds(r, cls, stride=dil), :] = op_ref[g, src, :]
                ln_ref[g - 1, pl.ds(r, cls, stride=dil), :] = lp_ref[g, src, :]

    def combine(i, carry):
        rows = pl.ds(pl.multiple_of(i * COMBINE_ROWS, COMBINE_ROWS), COMBINE_ROWS)
        ls = [lp_ref[0, rows, :], ln_ref[0, rows, :], ln_ref[1, rows, :]]
        os_ = [op_ref[0, rows, :], on_ref[0, rows, :], on_ref[1, rows, :]]
        top = jnp.maximum(jnp.maximum(ls[0], ls[1]), ls[2])
        ws = [jnp.exp(l - top) for l in ls]
        tot = ws[0] + ws[1] + ws[2]
        o_ref[rows, :] = (ws[0] * os_[0] + ws[1] * os_[1] + ws[2] * os_[2]) / tot * zc_ref[rows, :]
        return carry

    lax.fori_loop(0, seq // COMBINE_ROWS, combine, 0)


def _attention(qkv, zc, cos, sin, batch, seq):
    m = qkv.shape[0]
    ncol = D_C // LANES
    per_group = D_C_OUT // LANES
    ngroup = len(ATTN_GROUPS)
    spec = lambda sec, g: pl.BlockSpec((seq, LANES), lambda b, jp: (b, sec * ncol + g * per_group + jp))
    tab = pl.BlockSpec((seq, LANES), lambda b, jp: (0, 0))
    oz = pl.BlockSpec((seq, LANES), lambda b, jp: (b, jp))
    perm_bf = pltpu.VMEM((ngroup, seq, LANES), BF16)
    perm_f32 = pltpu.VMEM((ngroup, seq, LANES), F32)
    nat_f32 = pltpu.VMEM((ngroup - 1, seq, LANES), F32)
    return pl.pallas_call(
        functools.partial(_attn_body, seq=seq),
        grid=(batch, per_group),
        in_specs=[spec(0, 0), spec(0, 1), spec(0, 2), spec(1, 0), spec(1, 1), spec(1, 2),
                  spec(2, 0), spec(2, 1), spec(2, 2), tab, tab, oz],
        out_specs=oz,
        out_shape=jax.ShapeDtypeStruct((m, D_C_OUT), F32),
        scratch_shapes=[perm_bf, perm_bf, perm_bf, perm_f32, perm_f32, nat_f32, nat_f32],
        compiler_params=_cparams(("parallel", "parallel")),
        name="dilated_attention",
    )(qkv, qkv, qkv, qkv, qkv, qkv, qkv, qkv, qkv, cos, sin, zc)


def _merge_body(x_ref, yf_ref, yb_ref, bonus_ref, za_ref, ob_ref, oc_ref, g_ref,
                gng_ref, gnb_ref, pa_ref, pb_ref, pc_ref, wo_ref, lng_ref, lnb_ref, o_ref):
    ones = _head_ones(LANES)
    y = yf_ref[...] + yb_ref[...]
    parts = []
    for cb in range(D_A // LANES):
        sl = slice(cb * LANES, (cb + 1) * LANES)
        yc = y[:, sl]
        mu = _split_dot(yc, ones, 2) * (1.0 / HEAD_DIM)
        d = yc - mu
        var = _split_dot(d * d, ones, 2) * (1.0 / HEAD_DIM)
        parts.append(d * lax.rsqrt(var + RWKV_GN_EPS))
    yn = jnp.concatenate(parts, axis=1)
    o_a = (yn * gng_ref[...] + gnb_ref[...] + bonus_ref[...]) * za_ref[...]
    g = g_ref[...]
    merged = (g[:, 0:D_MODEL] * _bdot(o_a, pa_ref[...])
              + g[:, D_MODEL:2 * D_MODEL] * _bdot(ob_ref[...], pb_ref[...])
              + g[:, 2 * D_MODEL:3 * D_MODEL] * _bdot(oc_ref[...], pc_ref[...]))
    t = DEEPNORM_ALPHA * x_ref[...] + _bdot(merged, wo_ref[...])
    mean = jnp.mean(t, axis=1, keepdims=True)
    d = t - mean
    var = jnp.mean(d * d, axis=1, keepdims=True)
    o_ref[...] = d * lax.rsqrt(var + LN_EPS) * lng_ref[...] + lnb_ref[...]


def _merge(x, yf, yb, bonus, za, ob, oc, gates, gn_g, gn_b, pa, pb, pc, wo, ln_g, ln_b):
    m = x.shape[0]
    tm = min(256, m)
    rows = lambda n: pl.BlockSpec((tm, n), lambda i: (i, 0))
    full = lambda a: pl.BlockSpec(a.shape, lambda i: (0, 0))
    return pl.pallas_call(
        _merge_body,
        grid=(m // tm,),
        in_specs=[rows(D_MODEL), rows(D_A), rows(D_A), rows(D_A), rows(D_A), rows(D_B), rows(D_C_OUT),
                  rows(3 * D_MODEL), full(gn_g), full(gn_b), full(pa), full(pb), full(pc), full(wo),
                  full(ln_g), full(ln_b)],
        out_specs=rows(D_MODEL),
        out_shape=jax.ShapeDtypeStruct((m, D_MODEL), F32),
        compiler_params=_cparams(("parallel",)),
        name="merge_norm",
    )(x, yf, yb, bonus, za, ob, oc, gates, gn_g, gn_b, pa, pb, pc, wo, ln_g, ln_b)


def _rope_tables(seq):
    inv = jnp.power(ROPE_THETA, -jnp.arange(0, HEAD_DIM, 2, dtype=F32) / HEAD_DIM)
    ang = jnp.arange(seq, dtype=F32)[:, None] * inv[None, :]
    ang = jnp.concatenate([ang, ang, ang, ang], axis=-1)
    return jnp.cos(ang), jnp.sin(ang)


def _pad_lora(w_up):
    z = jnp.zeros_like(w_up[0])
    return jnp.stack([jnp.concatenate([w_up[0], z], axis=0), jnp.concatenate([z, w_up[1]], axis=0)])


def _block_diag(w_g):
    g, c, _ = w_g.shape
    eye = jnp.eye(g, dtype=w_g.dtype)
    return (eye[:, None, :, None] * w_g[:, :, None, :]).reshape(g * c, g * c)


def kernel(x, w_in, b_in, rwkv_mu, rwkv_w0, rwkv_w_up, rwkv_a0, rwkv_a_up, rwkv_k_k, rwkv_k_a, rwkv_r_k, rwkv_gn_g, rwkv_gn_b, pool_w, pool_b, pool_scale, proj_a, proj_b, proj_c, w_out, ln_g, ln_b):
    batch, seq, _ = x.shape
    m = batch * seq
    cos, sin = _rope_tables(seq)
    row = lambda t: t.reshape(1, -1)
    bounds = [0]
    for width in (N_SHIFT, D_A, D_B, D_B, 3 * D_C, D_C_OUT, 3 * D_MODEL):
        bounds.append(bounds[-1] + width)
    acts = (None, "silu", None, "silu", None, "silu", "sigmoid")

    xf = x.reshape(m, D_MODEL)
    for l in range(DEPTH):
        w_l = w_in[l].astype(BF16)
        b_l = b_in[l]
        sec = [_linear(xf, w_l[:, lo:hi], row(b_l[lo:hi]), act)
               for lo, hi, act in zip(bounds[:-1], bounds[1:], acts)]
        u, za, p, zb, qkv, zc, gates = sec

        r, k, v, kkn, lw, a, bonus = _rwkv_prep(
            u, rwkv_mu[l], rwkv_w0[l], rwkv_a0[l],
            _pad_lora(rwkv_w_up[l]).astype(BF16), _pad_lora(rwkv_a_up[l]).astype(BF16),
            row(rwkv_k_k[l]), row(rwkv_r_k[l]), seq)
        yf, yb = _rwkv_scan(r, k, v, kkn, lw, a, row(rwkv_k_a[l]), batch, seq)
        ob = _pool(p, zb, _block_diag(pool_w[l]).astype(BF16), row(pool_b[l]), row(pool_scale[l]), batch, seq)
        oc = _attention(qkv, zc, cos, sin, batch, seq)
        xf = _merge(xf, yf, yb, bonus, za, ob, oc, gates, row(rwkv_gn_g[l]), row(rwkv_gn_b[l]),
                    proj_a[l].astype(BF16), proj_b[l].astype(BF16), proj_c[l].astype(BF16),
                    w_out[l].astype(BF16), row(ln_g[l]), row(ln_b[l]))
    return xf.reshape(batch, seq, D_MODEL)
```

```python
import functools

import jax
import jax.numpy as jnp
from jax import lax
from jax.experimental import pallas as pl
from jax.experimental.pallas import tpu as pltpu

F32 = jnp.float32
BF16 = jnp.bfloat16

D_MODEL = 1024
DEPTH = 4
HEAD_DIM = 64
LANES = 128
D_A = 1024
LORA = 64
N_SHIFT = 3 * D_A + 4 * LORA
RWKV_GN_EPS = 64e-5
POOL_HALF = (1, 2, 4, 8)
POOL_CH = 192
D_B = 768
D_C = 768
D_C_OUT = 256
ATTN_GROUPS = ((64, 1), (256, 4), (1024, 16))
ROPE_THETA = 10000.0
DEEPNORM_ALPHA = (2 * DEPTH) ** 0.25
LN_EPS = 1e-5
CHUNK = 64
Q_TILE = 128
NEG_BIG = -1e30
VMEM_LIMIT = 52 * 1024 * 1024


def _cparams(sem):
    return pltpu.CompilerParams(dimension_semantics=sem, vmem_limit_bytes=VMEM_LIMIT)


def _bdot(a, b):
    return jnp.dot(a.astype(BF16), b.astype(BF16), preferred_element_type=F32)


def _bdot_nt(a, b):
    return lax.dot_general(a.astype(BF16), b.astype(BF16), (((1,), (1,)), ((), ())),
                           preferred_element_type=F32)


def _split_dot(a, b_exact, terms):
    acc = None
    rem = a
    for _ in range(terms):
        piece = rem.astype(BF16)
        part = jnp.dot(piece, b_exact, preferred_element_type=F32)
        acc = part if acc is None else acc + part
        rem = rem - piece.astype(F32)
    return acc


def _head_ones(n):
    r = lax.broadcasted_iota(jnp.int32, (n, n), 0) // HEAD_DIM
    c = lax.broadcasted_iota(jnp.int32, (n, n), 1) // HEAD_DIM
    return (r == c).astype(BF16)


def _sigmoid(x):
    return 1.0 / (1.0 + jnp.exp(-x))


def _linear_body(x_ref, w_ref, b_ref, o_ref, *, act):
    acc = jnp.dot(x_ref[...], w_ref[...], preferred_element_type=F32) + b_ref[...]
    if act == "silu":
        acc = acc * _sigmoid(acc)
    elif act == "sigmoid":
        acc = _sigmoid(acc)
    o_ref[...] = acc.astype(o_ref.dtype)


def _pick_tile(n, cap):
    best = LANES
    for t in range(LANES, cap + 1, LANES):
        if n % t == 0:
            best = t
    return best


def _linear(x, w, b, act, out_dtype):
    m, k = x.shape
    n = w.shape[1]
    tm = min(1024, m)
    tn = _pick_tile(n, 1664)
    return pl.pallas_call(
        functools.partial(_linear_body, act=act),
        grid=(n // tn, m // tm),
        in_specs=[pl.BlockSpec((tm, k), lambda j, i: (i, 0)),
                  pl.BlockSpec((k, tn), lambda j, i: (0, j)),
                  pl.BlockSpec((1, tn), lambda j, i: (0, j))],
        out_specs=pl.BlockSpec((tm, tn), lambda j, i: (i, j)),
        out_shape=jax.ShapeDtypeStruct((m, n), out_dtype),
        compiler_params=_cparams(("parallel", "parallel")),
        name="linear_" + (act or "id"),
    )(x, w, b)


def _prep_body(u_ref, up_ref, un_ref, mu_ref, w0_ref, a0_ref, wup_ref, aup_ref, kk_ref, rk_ref,
               r_o, k_o, v_o, kkn_o, lw_o, a_o, bonus_o, *, tm, seq):
    i = pl.program_id(0)
    u = u_ref[...]
    row = lax.broadcasted_iota(jnp.int32, (tm, 1), 0)
    t0 = (i * tm) % seq
    prev_edge = jnp.where(t0 == 0, 0.0, up_ref[7:8, :])
    next_edge = jnp.where(t0 + tm == seq, 0.0, un_ref[0:1, :])
    prev = jnp.where(row == 0, prev_edge, pltpu.roll(u, 1, 0))
    nxt = jnp.where(row == tm - 1, next_edge, pltpu.roll(u, tm - 1, 0))
    u = u + mu_ref[0:1, :] * (prev - u) + mu_ref[1:2, :] * (nxt - u)

    r = u[:, 0:D_A]
    k = u[:, D_A:2 * D_A]
    v = u[:, 2 * D_A:3 * D_A]
    wd = jnp.tanh(u[:, 3 * D_A:3 * D_A + 2 * LORA])
    ad = u[:, 3 * D_A + 2 * LORA:N_SHIFT]
    r_o[...] = r.astype(r_o.dtype)
    k_o[...] = k.astype(k_o.dtype)
    v_o[...] = v.astype(v_o.dtype)
    for z in range(2):
        warg = w0_ref[z:z + 1, :] + _bdot(wd, wup_ref[z])
        neg = -warg
        softplus = jnp.maximum(neg, 0.0) + jnp.log(1.0 + jnp.exp(-jnp.abs(neg)))
        lw_o[z] = -jnp.exp(-softplus - 0.5)
        a_o[z] = _sigmoid(a0_ref[z:z + 1, :] + _bdot(ad, aup_ref[z])).astype(a_o.dtype)

    ones = _head_ones(LANES)
    for cb in range(D_A // LANES):
        sl = slice(cb * LANES, (cb + 1) * LANES)
        kk = k[:, sl] * kk_ref[:, sl]
        ss = _split_dot(kk * kk, ones, 2)
        kkn_o[:, sl] = (kk * lax.rsqrt(ss + 1e-12)).astype(kkn_o.dtype)
        rk = _split_dot(r[:, sl] * k[:, sl] * rk_ref[:, sl], ones, 2)
        bonus_o[:, sl] = (rk * v[:, sl]).astype(bonus_o.dtype)


def _rwkv_prep(u, mu, w0, a0, wup, aup, k_k, r_k, seq):
    m = u.shape[0]
    tm = min(256, seq)
    nb8 = m // 8
    full = lambda shape: pl.BlockSpec(shape, lambda i: (0,) * len(shape))
    row_spec = pl.BlockSpec((tm, D_A), lambda i: (i, 0))
    dir_spec = pl.BlockSpec((2, tm, D_A), lambda i: (0, i, 0))
    act = jax.ShapeDtypeStruct((m, D_A), BF16)
    act2 = jax.ShapeDtypeStruct((2, m, D_A), BF16)
    logw = jax.ShapeDtypeStruct((2, m, D_A), F32)
    return pl.pallas_call(
        functools.partial(_prep_body, tm=tm, seq=seq),
        grid=(m // tm,),
        in_specs=[pl.BlockSpec((tm, N_SHIFT), lambda i: (i, 0)),
                  pl.BlockSpec((8, N_SHIFT), lambda i: (jnp.maximum(i * (tm // 8) - 1, 0), 0)),
                  pl.BlockSpec((8, N_SHIFT), lambda i: (jnp.minimum((i + 1) * (tm // 8), nb8 - 1), 0)),
                  full((2, N_SHIFT)), full((2, D_A)), full((2, D_A)),
                  full((2, LANES, D_A)), full((2, LANES, D_A)), full((1, D_A)), full((1, D_A))],
        out_specs=[row_spec, row_spec, row_spec, row_spec, dir_spec, dir_spec, row_spec],
        out_shape=[act, act, act, act, logw, act2, act],
        compiler_params=_cparams(("parallel",)),
        name="rwkv_prep",
    )(u, u, u, mu, w0, a0, wup, aup, k_k, r_k)


SCAN_CHUNKS = 4
SCAN_GROUPS = 2


def _bmm(a, b):
    return lax.dot_general(a.astype(BF16), b.astype(BF16), (((2,), (1,)), ((0,), (0,))),
                           preferred_element_type=F32)


def _bmm_nt(a, b):
    return lax.dot_general(a.astype(BF16), b.astype(BF16), (((2,), (2,)), ((0,), (0,))),
                           preferred_element_type=F32)


def _stack_heads(t):
    lane = lax.broadcasted_iota(jnp.int32, t.shape, 2)
    first = lane < HEAD_DIM
    return jnp.concatenate([jnp.where(first, t, 0.0), jnp.where(first, 0.0, t)], axis=1)


def _chunk_terms(r, k, v, kk, lw, a, k_a, nfwd):
    n, c, _ = r.shape
    n2 = 2 * c
    rev = lax.broadcasted_iota(jnp.int32, (n, c, 3 * c), 0) >= nfwd
    ri = lax.broadcasted_iota(jnp.int32, (n, c, 3 * c), 1)
    ci = lax.broadcasted_iota(jnp.int32, (n, c, 3 * c), 2) % c
    tri = ((rev & (ci >= ri)) | (~rev & (ci <= ri))).astype(BF16)
    pieces = []
    rem = lw
    for _ in range(3):
        pieces.append(rem.astype(BF16))
        rem = rem - pieces[-1].astype(F32)
    cum = _bmm(tri, jnp.concatenate(pieces, axis=1))
    is_rev = lax.broadcasted_iota(jnp.int32, (n, 1, LANES), 0) >= nfwd
    total = jnp.where(is_rev, cum[:, 0:1, :], cum[:, c - 1:c, :])
    kd = k * (1.0 + (a - 1.0) * k_a)
    bb = kk * a
    e_neg = jnp.exp(-cum)
    e_rem = jnp.exp(total - cum)
    xn = _stack_heads(-kk * jnp.exp(cum - lw))
    xr = _stack_heads(r * jnp.exp(cum))
    vd = _stack_heads(v)
    b2d = _stack_heads(bb * e_rem)
    k2d = _stack_heads(kd * e_rem)
    bt = bb * e_neg
    kt = kd * e_neg
    bd = jnp.concatenate([bt, bt], axis=1)
    kdd = jnp.concatenate([kt, kt], axis=1)

    rev2 = lax.broadcasted_iota(jnp.int32, (n, n2, n2), 0) >= nfwd
    rr = lax.broadcasted_iota(jnp.int32, (n, n2, n2), 1)
    cc = lax.broadcasted_iota(jnp.int32, (n, n2, n2), 2)
    same = (rr // c) == (cc // c)
    tt = rr % c
    ss = cc % c
    diag = rr == cc
    strict = same & ((rev2 & (ss > tt)) | (~rev2 & (ss < tt)))
    incl = strict | diag

    aa = _bmm_nt(jnp.concatenate([xn, xr], axis=1), jnp.concatenate([bd, kdd], axis=1))
    a_ab = jnp.where(strict, aa[:, :n2, :n2], 0.0)
    a_ak = jnp.where(strict, aa[:, :n2, n2:], 0.0)
    a_rb = jnp.where(incl, aa[:, n2:, :n2], 0.0)
    a_rk = jnp.where(incl, aa[:, n2:, n2:], 0.0)

    tinv = jnp.where(diag, 1.0, a_ab)
    apow = _bmm(a_ab, a_ab)
    for _ in range(max(1, (c - 1).bit_length()) - 2):
        both = _bmm(apow, jnp.concatenate([apow, tinv], axis=2))
        apow = both[:, :, :n2]
        tinv = tinv + both[:, :, n2:]
    tinv = tinv + _bmm(apow, tinv)

    rhs_q = _bmm(a_ak, vd)
    pq = _bmm(tinv, jnp.concatenate([xn, rhs_q], axis=2))
    lhs = jnp.concatenate([jnp.concatenate([a_rb, a_rk], axis=2),
                           jnp.concatenate([jnp.swapaxes(b2d, 1, 2), jnp.swapaxes(k2d, 1, 2)], axis=2)], axis=1)
    rhs = jnp.concatenate([pq, jnp.concatenate([jnp.zeros_like(vd), vd], axis=2)], axis=1)
    out = _bmm(lhs, rhs)
    r2 = xr + out[:, :n2, :LANES]
    y0 = out[:, :n2, LANES:]
    gm = out[:, n2:, :LANES]
    h0 = out[:, n2:, LANES:]
    gcol = jnp.sum(jnp.where(diag, jnp.exp(total), 0.0), axis=2, keepdims=True)
    return jnp.concatenate([r2, gm], axis=1), y0, gcol, h0


def _scan_body(rf, kf, vf, kkf, lwf, af, rb, kb, vb, kkb, lwb, ab, ka_ref, yf_o, yb_o, st_ref):
    @pl.when(pl.program_id(2) == 0)
    def _():
        st_ref[...] = jnp.zeros_like(st_ref)

    nch = SCAN_CHUNKS
    rows = nch * CHUNK
    n2 = 2 * CHUNK
    st_f = st_ref[0]
    st_b = st_ref[1]
    for grp in range(SCAN_GROUPS):
        f0 = grp * rows
        b0 = (SCAN_GROUPS - 1 - grp) * rows

        def both(f, b, f0=f0, b0=b0):
            return jnp.concatenate([f[f0:f0 + rows, :].astype(F32).reshape(nch, CHUNK, LANES),
                                    b[b0:b0 + rows, :].astype(F32).reshape(nch, CHUNK, LANES)], axis=0)

        rg, y0, gcol, h0 = _chunk_terms(both(rf, rb), both(kf, kb), both(vf, vb), both(kkf, kkb),
                                        both(lwf, lwb), both(af, ab), ka_ref[...], nch)
        for j in range(nch):
            jf = j
            jb = 2 * nch - 1 - j
            of = _bdot(rg[jf], st_f)
            ob = _bdot(rg[jb], st_b)
            yf = of[:n2] + y0[jf]
            yb = ob[:n2] + y0[jb]
            st_f = st_f * gcol[jf] + of[n2:] + h0[jf]
            st_b = st_b * gcol[jb] + ob[n2:] + h0[jb]
            fr = f0 + jf * CHUNK
            br = b0 + (jb - nch) * CHUNK
            yf_o[fr:fr + CHUNK, :] = (yf[:CHUNK] + yf[CHUNK:]).astype(yf_o.dtype)
            yb_o[br:br + CHUNK, :] = (yb[:CHUNK] + yb[CHUNK:]).astype(yb_o.dtype)
    st_ref[0] = st_f
    st_ref[1] = st_b


def _rwkv_scan(r, k, v, kkn, lw, a, k_a, batch, seq):
    m = r.shape[0]
    rows = SCAN_GROUPS * SCAN_CHUNKS * CHUNK
    ns = seq // rows
    npair = D_A // LANES
    fwd = lambda b, p, s: (b * ns + s, p)
    bwd = lambda b, p, s: (b * ns + ns - 1 - s, p)
    blk = (rows, LANES)
    fspec = pl.BlockSpec(blk, fwd)
    bspec = pl.BlockSpec(blk, bwd)
    fdir = pl.BlockSpec((None,) + blk, lambda b, p, s: (0,) + fwd(b, p, s))
    bdir = pl.BlockSpec((None,) + blk, lambda b, p, s: (1,) + bwd(b, p, s))
    out = jax.ShapeDtypeStruct((m, D_A), BF16)
    return pl.pallas_call(
        _scan_body,
        grid=(batch, npair, ns),
        in_specs=[fspec, fspec, fspec, fspec, fdir, fdir, bspec, bspec, bspec, bspec, bdir, bdir,
                  pl.BlockSpec((1, LANES), lambda b, p, s: (0, p))],
        out_specs=[fspec, bspec],
        out_shape=[out, out],
        scratch_shapes=[pltpu.VMEM((2, LANES, LANES), F32)],
        compiler_params=_cparams(("parallel", "parallel", "arbitrary")),
        name="rwkv_scan",
    )(r, k, v, kkn, lw, a, r, k, v, kkn, lw, a, k_a)


POOL_PAD = 8
POOL_COLS = 256
POOL_ROWS = 256


def _pool_body(p_ref, w_ref, z_ref, b_ref, s_ref, o_ref, pad_ref, acc_ref, *, seq):
    cb = pl.program_id(1)
    ncb = pl.num_programs(1)
    pad_ref[0:POOL_PAD, :] = jnp.zeros((POOL_PAD, POOL_COLS), F32)
    pad_ref[POOL_PAD + seq:, :] = jnp.zeros((POOL_PAD, POOL_COLS), F32)
    pad_ref[POOL_PAD:POOL_PAD + seq, :] = p_ref[...].astype(F32)

    @pl.when(cb == 0)
    def _():
        acc_ref[...] = jnp.zeros_like(acc_ref)

    lane = lax.broadcasted_iota(jnp.int32, (1, POOL_COLS), 1) + cb * POOL_COLS
    group = lane // POOL_CH
    w = w_ref[...]

    def rows(i, carry):
        base = pl.multiple_of(i * POOL_ROWS, POOL_ROWS)
        t = lax.broadcasted_iota(jnp.int32, (POOL_ROWS, 1), 0) + base
        win = pad_ref[pl.ds(base, POOL_ROWS + 2 * POOL_PAD), :]
        x = win[POOL_PAD:POOL_PAD + POOL_ROWS]
        acc = x
        sums = []
        d = 1
        for half in POOL_HALF:
            while d <= half:
                acc = acc + win[POOL_PAD - d:POOL_PAD - d + POOL_ROWS]
                acc = acc + win[POOL_PAD + d:POOL_PAD + d + POOL_ROWS]
                d += 1
            cnt = jnp.minimum(t + half, seq - 1) - jnp.maximum(t - half, 0) + 1
            sums.append(acc / cnt.astype(F32))
        mean = sums[3]
        for g in (2, 1, 0):
            mean = jnp.where(group == g, sums[g], mean)
        acc_ref[pl.ds(base, POOL_ROWS), :] += _bdot(mean - x, w)
        return carry

    lax.fori_loop(0, seq // POOL_ROWS, rows, 0)

    @pl.when(cb == ncb - 1)
    def _():
        o_ref[...] = ((acc_ref[...] + b_ref[...]) * s_ref[...] * z_ref[...].astype(F32)).astype(o_ref.dtype)


def _pool(p, zb, w_bd, b, scale, batch, seq):
    m = p.shape[0]
    ncb = D_B // POOL_COLS
    full = pl.BlockSpec((seq, D_B), lambda bi, cb: (bi, 0))
    vec = pl.BlockSpec((1, D_B), lambda bi, cb: (0, 0))
    return pl.pallas_call(
        functools.partial(_pool_body, seq=seq),
        grid=(batch, ncb),
        in_specs=[pl.BlockSpec((seq, POOL_COLS), lambda bi, cb: (bi, cb)),
                  pl.BlockSpec((POOL_COLS, D_B), lambda bi, cb: (cb, 0)),
                  full, vec, vec],
        out_specs=full,
        out_shape=jax.ShapeDtypeStruct((m, D_B), BF16),
        scratch_shapes=[pltpu.VMEM((seq + 2 * POOL_PAD, POOL_COLS), F32), pltpu.VMEM((seq, D_B), F32)],
        compiler_params=_cparams(("parallel", "arbitrary")),
        name="pool_mixer",
    )(p, w_bd, zb, b, scale)


def _rope_matrix():
    half = HEAD_DIM // 2
    src = lax.broadcasted_iota(jnp.int32, (LANES, LANES), 0)
    dst = lax.broadcasted_iota(jnp.int32, (LANES, LANES), 1)
    low = (dst % HEAD_DIM) < half
    return jnp.where(low & (src == dst + half), -1.0, jnp.where(~low & (src == dst - half), 1.0, 0.0)).astype(BF16)


def _rope(t, cos, sin, rot_matrix):
    return t * cos + _split_dot(t, rot_matrix, 2) * sin


ATTN_SIDE = 64
ATTN_TILES = 4
COMBINE_ROWS = 256


def _attn_body(q0, q1, q2, k0, k1, k2, v0, v1, v2, cos_ref, sin_ref, zc_ref, o_ref,
               qp_ref, kp_ref, vp_ref, op_ref, lp_ref, on_ref, ln_ref, *, seq):
    q_refs = (q0, q1, q2)
    k_refs = (k0, k1, k2)
    v_refs = (v0, v1, v2)
    scale = HEAD_DIM ** -0.5
    rot_matrix = _rope_matrix()
    for g, (half_span, dil) in enumerate(ATTN_GROUPS):
        cls = seq // dil
        for r in range(dil):
            rows = pl.ds(r, cls, stride=dil) if dil > 1 else pl.ds(0, seq)
            dst = pl.ds(r * cls, cls)
            cos = cos_ref[rows, :]
            sin = sin_ref[rows, :]
            qp_ref[g, dst, :] = (_rope(q_refs[g][rows, :], cos, sin, rot_matrix) * scale).astype(BF16)
            kp_ref[g, dst, :] = _rope(k_refs[g][rows, :], cos, sin, rot_matrix).astype(BF16)
            vp_ref[g, dst, :] = v_refs[g][rows, :].astype(BF16)

    nt = ATTN_TILES
    first = lax.broadcasted_iota(jnp.int32, (nt, Q_TILE, LANES), 2) < HEAD_DIM
    for g, (half_span, dil) in enumerate(ATTN_GROUPS):
        cls = seq // dil
        klen = min(cls, Q_TILE + 2 * ATTN_SIDE)
        rel = (lax.broadcasted_iota(jnp.int32, (Q_TILE, klen), 1)
               - lax.broadcasted_iota(jnp.int32, (Q_TILE, klen), 0))

        def tiles(it, carry, g=g, cls=cls, klen=klen, rel=rel):
            row0 = pl.multiple_of(it * (nt * Q_TILE), nt * Q_TILE)
            keys, vals, valid = [], [], []
            for j in range(nt):
                qrow = row0 + j * Q_TILE
                base = (qrow // cls) * cls
                krow = pl.multiple_of(base + jnp.clip(qrow - base - ATTN_SIDE, 0, cls - klen), ATTN_SIDE)
                keys.append(kp_ref[g, pl.ds(krow, klen), :])
                vals.append(vp_ref[g, pl.ds(krow, klen), :])
                valid.append(jnp.abs(rel + (krow - qrow)) <= ATTN_SIDE)
            q = qp_ref[g, pl.ds(row0, nt * Q_TILE), :].reshape(nt, Q_TILE, LANES)
            zero = jnp.zeros_like(q)
            qm = jnp.concatenate([jnp.where(first, q, zero), jnp.where(first, zero, q)], axis=0)
            keys = jnp.stack(keys + keys)
            vals = jnp.stack(vals + vals)
            valid = jnp.stack(valid + valid)
            s = jnp.where(valid, _bmm_nt(qm, keys), NEG_BIG)
            mx = jnp.max(s, axis=2, keepdims=True)
            e = jnp.exp(s - mx)
            den = jnp.sum(e, axis=2, keepdims=True)
            o = _bmm(e, vals) / den
            lse = jnp.broadcast_to(mx + jnp.log(den), o.shape)
            rows = pl.ds(row0, nt * Q_TILE)
            op_ref[g, rows, :] = jnp.where(first, o[:nt], o[nt:]).reshape(nt * Q_TILE, LANES)
            lp_ref[g, rows, :] = jnp.where(first, lse[:nt], lse[nt:]).reshape(nt * Q_TILE, LANES)
            return carry

        lax.fori_loop(0, seq // (nt * Q_TILE), tiles, 0)
        if dil > 1:
            for r in range(dil):
                src = pl.ds(r * cls, cls)
                on_ref[g - 1, pl.ds(r, cls, stride=dil), :] = op_ref[g, src, :]
                ln_ref[g - 1, pl.ds(r, cls, stride=dil), :] = lp_ref[g, src, :]

    def combine(i, carry):
        rows = pl.ds(pl.multiple_of(i * COMBINE_ROWS, COMBINE_ROWS), COMBINE_ROWS)
        ls = [lp_ref[0, rows, :], ln_ref[0, rows, :], ln_ref[1, rows, :]]
        os_ = [op_ref[0, rows, :], on_ref[0, rows, :], on_ref[1, rows, :]]
        top = jnp.maximum(jnp.maximum(ls[0], ls[1]), ls[2])
        ws = [jnp.exp(l - top) for l in ls]
        tot = ws[0] + ws[1] + ws[2]
        out = (ws[0] * os_[0] + ws[1] * os_[1] + ws[2] * os_[2]) / tot * zc_ref[rows, :].astype(F32)
        o_ref[rows, :] = out.astype(o_ref.dtype)
        return carry

    lax.fori_loop(0, seq // COMBINE_ROWS, combine, 0)


def _attention(qkv, zc, cos, sin, batch, seq):
    m = qkv.shape[0]
    ncol = D_C // LANES
    per_group = D_C_OUT // LANES
    ngroup = len(ATTN_GROUPS)
    spec = lambda sec, g: pl.BlockSpec((seq, LANES), lambda b, jp: (b, sec * ncol + g * per_group + jp))
    tab = pl.BlockSpec((seq, LANES), lambda b, jp: (0, 0))
    oz = pl.BlockSpec((seq, LANES), lambda b, jp: (b, jp))
    perm_bf = pltpu.VMEM((ngroup, seq, LANES), BF16)
    perm_f32 = pltpu.VMEM((ngroup, seq, LANES), F32)
    nat_f32 = pltpu.VMEM((ngroup - 1, seq, LANES), F32)
    return pl.pallas_call(
        functools.partial(_attn_body, seq=seq),
        grid=(batch, per_group),
        in_specs=[spec(0, 0), spec(0, 1), spec(0, 2), spec(1, 0), spec(1, 1), spec(1, 2),
                  spec(2, 0), spec(2, 1), spec(2, 2), tab, tab, oz],
        out_specs=oz,
        out_shape=jax.ShapeDtypeStruct((m, D_C_OUT), BF16),
        scratch_shapes=[perm_bf, perm_bf, perm_bf, perm_f32, perm_f32, nat_f32, nat_f32],
        compiler_params=_cparams(("parallel", "parallel")),
        name="dilated_attention",
    )(qkv, qkv, qkv, qkv, qkv, qkv, qkv, qkv, qkv, cos, sin, zc)


def _merge_body(x_ref, yf_ref, yb_ref, bonus_ref, za_ref, ob_ref, oc_ref, g_ref,
                gng_ref, gnb_ref, pa_ref, pb_ref, pc_ref, wo_ref, lng_ref, lnb_ref, o_ref, ob16_ref):
    ones = _head_ones(LANES)
    y = yf_ref[...].astype(F32) + yb_ref[...].astype(F32)
    parts = []
    for cb in range(D_A // LANES):
        sl = slice(cb * LANES, (cb + 1) * LANES)
        yc = y[:, sl]
        mu = _split_dot(yc, ones, 2) * (1.0 / HEAD_DIM)
        d = yc - mu
        var = _split_dot(d * d, ones, 2) * (1.0 / HEAD_DIM)
        parts.append(d * lax.rsqrt(var + RWKV_GN_EPS))
    yn = jnp.concatenate(parts, axis=1)
    o_a = (yn * gng_ref[...] + gnb_ref[...] + bonus_ref[...].astype(F32)) * za_ref[...].astype(F32)
    merged = (g_ref[:, 0:D_MODEL].astype(F32) * _bdot(o_a, pa_ref[...])
              + g_ref[:, D_MODEL:2 * D_MODEL].astype(F32) * _bdot(ob_ref[...], pb_ref[...])
              + g_ref[:, 2 * D_MODEL:3 * D_MODEL].astype(F32) * _bdot(oc_ref[...], pc_ref[...]))
    t = DEEPNORM_ALPHA * x_ref[...] + _bdot(merged, wo_ref[...])
    mean = jnp.mean(t, axis=1, keepdims=True)
    d = t - mean
    var = jnp.mean(d * d, axis=1, keepdims=True)
    out = d * lax.rsqrt(var + LN_EPS) * lng_ref[...] + lnb_ref[...]
    o_ref[...] = out
    ob16_ref[...] = out.astype(BF16)


def _merge(x, yf, yb, bonus, za, ob, oc, gates, gn_g, gn_b, pa, pb, pc, wo, ln_g, ln_b):
    m = x.shape[0]
    tm = min(256, m)
    rows = lambda n: pl.BlockSpec((tm, n), lambda i: (i, 0))
    full = lambda a: pl.BlockSpec(a.shape, lambda i: (0, 0))
    return pl.pallas_call(
        _merge_body,
        grid=(m // tm,),
        in_specs=[rows(D_MODEL), rows(D_A), rows(D_A), rows(D_A), rows(D_A), rows(D_B), rows(D_C_OUT),
                  rows(3 * D_MODEL), full(gn_g), full(gn_b), full(pa), full(pb), full(pc), full(wo),
                  full(ln_g), full(ln_b)],
        out_specs=[rows(D_MODEL), rows(D_MODEL)],
        out_shape=[jax.ShapeDtypeStruct((m, D_MODEL), F32), jax.ShapeDtypeStruct((m, D_MODEL), BF16)],
        compiler_params=_cparams(("parallel",)),
        name="merge_norm",
    )(x, yf, yb, bonus, za, ob, oc, gates, gn_g, gn_b, pa, pb, pc, wo, ln_g, ln_b)


def _rope_tables(seq):
    inv = jnp.power(ROPE_THETA, -jnp.arange(0, HEAD_DIM, 2, dtype=F32) / HEAD_DIM)
    ang = jnp.arange(seq, dtype=F32)[:, None] * inv[None, :]
    ang = jnp.concatenate([ang, ang, ang, ang], axis=-1)
    return jnp.cos(ang), jnp.sin(ang)


def _pad_lora(w_up):
    z = jnp.zeros_like(w_up[0])
    return jnp.stack([jnp.concatenate([w_up[0], z], axis=0), jnp.concatenate([z, w_up[1]], axis=0)])


def _block_diag(w_g):
    g, c, _ = w_g.shape
    eye = jnp.eye(g, dtype=w_g.dtype)
    return (eye[:, None, :, None] * w_g[:, :, None, :]).reshape(g * c, g * c)


def kernel(x, w_in, b_in, rwkv_mu, rwkv_w0, rwkv_w_up, rwkv_a0, rwkv_a_up, rwkv_k_k, rwkv_k_a, rwkv_r_k, rwkv_gn_g, rwkv_gn_b, pool_w, pool_b, pool_scale, proj_a, proj_b, proj_c, w_out, ln_g, ln_b):
    batch, seq, _ = x.shape
    m = batch * seq
    cos, sin = _rope_tables(seq)
    row = lambda t: t.reshape(1, -1)
    bounds = [0]
    for width in (N_SHIFT, D_A, D_B, D_B, 3 * D_C, D_C_OUT, 3 * D_MODEL):
        bounds.append(bounds[-1] + width)
    acts = (None, "silu", None, "silu", None, "silu", "sigmoid")
    dtypes = (F32, BF16, BF16, BF16, F32, BF16, BF16)

    xf = x.reshape(m, D_MODEL)
    xb = xf.astype(BF16)
    for l in range(DEPTH):
        w_l = w_in[l].astype(BF16)
        b_l = b_in[l]
        sec = [_linear(xb, w_l[:, lo:hi], row(b_l[lo:hi]), act, dt)
               for lo, hi, act, dt in zip(bounds[:-1], bounds[1:], acts, dtypes)]
        u, za, p, zb, qkv, zc, gates = sec

        r, k, v, kkn, lw, a, bonus = _rwkv_prep(
            u, rwkv_mu[l], rwkv_w0[l], rwkv_a0[l],
            _pad_lora(rwkv_w_up[l]).astype(BF16), _pad_lora(rwkv_a_up[l]).astype(BF16),
            row(rwkv_k_k[l]), row(rwkv_r_k[l]), seq)
        yf, yb = _rwkv_scan(r, k, v, kkn, lw, a, row(rwkv_k_a[l]), batch, seq)
        ob = _pool(p, zb, _block_diag(pool_w[l]).astype(BF16), row(pool_b[l]), row(pool_scale[l]), batch, seq)
        oc = _attention(qkv, zc, cos, sin, batch, seq)
        xf, xb = _merge(xf, yf, yb, bonus, za, ob, oc, gates, row(rwkv_gn_g[l]), row(rwkv_gn_b[l]),
                        proj_a[l].astype(BF16), proj_b[l].astype(BF16), proj_c[l].astype(BF16),
                        w_out[l].astype(BF16), row(ln_g[l]), row(ln_b[l]))
    return xf.reshape(batch, seq, D_MODEL)
```

```python
import functools
import math

import jax
import jax.numpy as jnp
from jax import lax
from jax.experimental import pallas as pl
from jax.experimental.pallas import tpu as pltpu

F32 = jnp.float32
BF16 = jnp.bfloat16

D_MODEL = 1024
DEPTH = 4
HEAD_DIM = 64
LANES = 128
D_A = 1024
LORA = 64
N_SHIFT = 3 * D_A + 4 * LORA
RWKV_GN_EPS = 64e-5
LOG_DECAY_SCALE = -math.exp(-0.5)
POOL_HALF = (1, 2, 4, 8)
POOL_CH = 192
D_B = 768
D_C = 768
D_C_OUT = 256
ATTN_GROUPS = ((64, 1), (256, 4), (1024, 16))
ROPE_THETA = 10000.0
DEEPNORM_ALPHA = (2 * DEPTH) ** 0.25
LN_EPS = 1e-5
CHUNK = 64
Q_TILE = 128
NEG_BIG = -1e30
VMEM_LIMIT = 52 * 1024 * 1024


def _cparams(sem):
    return pltpu.CompilerParams(dimension_semantics=sem, vmem_limit_bytes=VMEM_LIMIT)


def _bdot(a, b):
    return jnp.dot(a.astype(BF16), b.astype(BF16), preferred_element_type=F32)


def _bdot_nt(a, b):
    return lax.dot_general(a.astype(BF16), b.astype(BF16), (((1,), (1,)), ((), ())),
                           preferred_element_type=F32)


def _split_dot(a, b_exact, terms):
    acc = None
    rem = a
    for _ in range(terms):
        piece = rem.astype(BF16)
        part = jnp.dot(piece, b_exact, preferred_element_type=F32)
        acc = part if acc is None else acc + part
        rem = rem - piece.astype(F32)
    return acc


def _head_ones(n):
    r = lax.broadcasted_iota(jnp.int32, (n, n), 0) // HEAD_DIM
    c = lax.broadcasted_iota(jnp.int32, (n, n), 1) // HEAD_DIM
    return (r == c).astype(BF16)


def _sigmoid(x):
    return 1.0 / (1.0 + jnp.exp(-x))


def _linear_body(x_ref, w_ref, b_ref, o_ref, *, act):
    acc = jnp.dot(x_ref[...], w_ref[...], preferred_element_type=F32) + b_ref[...]
    if act == "silu":
        acc = acc * _sigmoid(acc)
    elif act == "sigmoid":
        acc = _sigmoid(acc)
    o_ref[...] = acc.astype(o_ref.dtype)


def _pick_tile(n, cap):
    best = LANES
    for t in range(LANES, cap + 1, LANES):
        if n % t == 0:
            best = t
    return best


def _linear(x, w, b, act, out_dtype):
    m, k = x.shape
    n = w.shape[1]
    tm = min(1024, m)
    tn = _pick_tile(n, 1664)
    return pl.pallas_call(
        functools.partial(_linear_body, act=act),
        grid=(n // tn, m // tm),
        in_specs=[pl.BlockSpec((tm, k), lambda j, i: (i, 0)),
                  pl.BlockSpec((k, tn), lambda j, i: (0, j)),
                  pl.BlockSpec((1, tn), lambda j, i: (0, j))],
        out_specs=pl.BlockSpec((tm, tn), lambda j, i: (i, j)),
        out_shape=jax.ShapeDtypeStruct((m, n), out_dtype),
        compiler_params=_cparams(("parallel", "parallel")),
        name="linear_" + (act or "id"),
    )(x, w, b)


def _prep_body(u_ref, up_ref, un_ref, mu_ref, w0_ref, a0_ref, wup_ref, aup_ref, kk_ref, rk_ref,
               r_o, k_o, v_o, kkn_o, lw_o, a_o, bonus_o, *, tm, seq):
    i = pl.program_id(0)
    u = u_ref[...]
    row = lax.broadcasted_iota(jnp.int32, (8, 1), 0)
    t0 = (i * tm) % seq
    prev_edge = jnp.where(t0 == 0, 0.0, up_ref[7:8, :])
    next_edge = jnp.where(t0 + tm == seq, 0.0, un_ref[0:1, :])
    prev = pltpu.roll(u, 1, 0)
    prev = jnp.concatenate([jnp.where(row == 0, prev_edge, prev[0:8]), prev[8:]], axis=0)
    nxt = pltpu.roll(u, tm - 1, 0)
    nxt = jnp.concatenate([nxt[:tm - 8], jnp.where(row == 7, next_edge, nxt[tm - 8:])], axis=0)
    mu_prev = mu_ref[0:1, :]
    mu_next = mu_ref[1:2, :]
    u = (1.0 - mu_prev - mu_next) * u + mu_prev * prev + mu_next * nxt

    r = u[:, 0:D_A]
    k = u[:, D_A:2 * D_A]
    v = u[:, 2 * D_A:3 * D_A]
    wd = jnp.tanh(u[:, 3 * D_A:3 * D_A + 2 * LORA])
    ad = u[:, 3 * D_A + 2 * LORA:N_SHIFT]
    r_o[...] = r.astype(r_o.dtype)
    k_o[...] = k.astype(k_o.dtype)
    v_o[...] = v.astype(v_o.dtype)
    for z in range(2):
        warg = w0_ref[z:z + 1, :] + _bdot(wd, wup_ref[z])
        lw_o[z] = LOG_DECAY_SCALE * _sigmoid(warg)
        a_o[z] = _sigmoid(a0_ref[z:z + 1, :] + _bdot(ad, aup_ref[z])).astype(a_o.dtype)

    ones = _head_ones(LANES)
    for cb in range(D_A // LANES):
        sl = slice(cb * LANES, (cb + 1) * LANES)
        kk = k[:, sl] * kk_ref[:, sl]
        ss = _bdot(kk * kk, ones)
        kkn_o[:, sl] = (kk * lax.rsqrt(ss + 1e-12)).astype(kkn_o.dtype)
        rk = _bdot(r[:, sl] * k[:, sl] * rk_ref[:, sl], ones)
        bonus_o[:, sl] = (rk * v[:, sl]).astype(bonus_o.dtype)


def _rwkv_prep(u, mu, w0, a0, wup, aup, k_k, r_k, seq):
    m = u.shape[0]
    tm = min(256, seq)
    nb8 = m // 8
    full = lambda shape: pl.BlockSpec(shape, lambda i: (0,) * len(shape))
    row_spec = pl.BlockSpec((tm, D_A), lambda i: (i, 0))
    dir_spec = pl.BlockSpec((2, tm, D_A), lambda i: (0, i, 0))
    act = jax.ShapeDtypeStruct((m, D_A), BF16)
    act2 = jax.ShapeDtypeStruct((2, m, D_A), BF16)
    logw = jax.ShapeDtypeStruct((2, m, D_A), F32)
    return pl.pallas_call(
        functools.partial(_prep_body, tm=tm, seq=seq),
        grid=(m // tm,),
        in_specs=[pl.BlockSpec((tm, N_SHIFT), lambda i: (i, 0)),
                  pl.BlockSpec((8, N_SHIFT), lambda i: (jnp.maximum(i * (tm // 8) - 1, 0), 0)),
                  pl.BlockSpec((8, N_SHIFT), lambda i: (jnp.minimum((i + 1) * (tm // 8), nb8 - 1), 0)),
                  full((2, N_SHIFT)), full((2, D_A)), full((2, D_A)),
                  full((2, LANES, D_A)), full((2, LANES, D_A)), full((1, D_A)), full((1, D_A))],
        out_specs=[row_spec, row_spec, row_spec, row_spec, dir_spec, dir_spec, row_spec],
        out_shape=[act, act, act, act, logw, act2, act],
        compiler_params=_cparams(("parallel",)),
        name="rwkv_prep",
    )(u, u, u, mu, w0, a0, wup, aup, k_k, r_k)


SCAN_CHUNKS = 4
SCAN_GROUPS = 2


def _bmm(a, b):
    return lax.dot_general(a.astype(BF16), b.astype(BF16), (((2,), (1,)), ((0,), (0,))),
                           preferred_element_type=F32)


def _bmm_nt(a, b):
    return lax.dot_general(a.astype(BF16), b.astype(BF16), (((2,), (2,)), ((0,), (0,))),
                           preferred_element_type=F32)


def _stack_heads(t):
    lane = lax.broadcasted_iota(jnp.int32, t.shape, 2)
    first = lane < HEAD_DIM
    return jnp.concatenate([jnp.where(first, t, 0.0), jnp.where(first, 0.0, t)], axis=1)


def _chunk_terms(r, k, v, kk, lw, a, k_a, nfwd):
    n, c, _ = r.shape
    n2 = 2 * c
    rev = lax.broadcasted_iota(jnp.int32, (n, c, 3 * c), 0) >= nfwd
    ri = lax.broadcasted_iota(jnp.int32, (n, c, 3 * c), 1)
    ci = lax.broadcasted_iota(jnp.int32, (n, c, 3 * c), 2) % c
    tri = ((rev & (ci >= ri)) | (~rev & (ci <= ri))).astype(BF16)
    pieces = []
    rem = lw
    for _ in range(3):
        pieces.append(rem.astype(BF16))
        rem = rem - pieces[-1].astype(F32)
    cum = _bmm(tri, jnp.concatenate(pieces, axis=1))
    is_rev = lax.broadcasted_iota(jnp.int32, (n, 1, LANES), 0) >= nfwd
    total = jnp.where(is_rev, cum[:, 0:1, :], cum[:, c - 1:c, :])
    kd = k * (1.0 + (a - 1.0) * k_a)
    bb = kk * a
    e_neg = jnp.exp(-cum)
    e_rem = jnp.exp(total - cum)
    xn = _stack_heads(-kk * jnp.exp(cum - lw))
    xr = _stack_heads(r * jnp.exp(cum))
    vd = _stack_heads(v)
    b2d = _stack_heads(bb * e_rem)
    k2d = _stack_heads(kd * e_rem)
    bt = bb * e_neg
    kt = kd * e_neg
    bd = jnp.concatenate([bt, bt], axis=1)
    kdd = jnp.concatenate([kt, kt], axis=1)

    rev2 = lax.broadcasted_iota(jnp.int32, (n, n2, n2), 0) >= nfwd
    rr = lax.broadcasted_iota(jnp.int32, (n, n2, n2), 1)
    cc = lax.broadcasted_iota(jnp.int32, (n, n2, n2), 2)
    same = (rr // c) == (cc // c)
    tt = rr % c
    ss = cc % c
    diag = rr == cc
    strict = same & ((rev2 & (ss > tt)) | (~rev2 & (ss < tt)))
    incl = strict | diag

    aa = _bmm_nt(jnp.concatenate([xn, xr], axis=1), jnp.concatenate([bd, kdd], axis=1))
    a_ab = jnp.where(strict, aa[:, :n2, :n2], 0.0)
    a_ak = jnp.where(strict, aa[:, :n2, n2:], 0.0)
    a_rb = jnp.where(incl, aa[:, n2:, :n2], 0.0)
    a_rk = jnp.where(incl, aa[:, n2:, n2:], 0.0)

    tinv = jnp.where(diag, 1.0, a_ab)
    apow = _bmm(a_ab, a_ab)
    for _ in range(max(1, (c - 1).bit_length()) - 2):
        both = _bmm(apow, jnp.concatenate([apow, tinv], axis=2))
        apow = both[:, :, :n2]
        tinv = tinv + both[:, :, n2:]
    tinv = tinv + _bmm(apow, tinv)

    rhs_q = _bmm(a_ak, vd)
    pq = _bmm(tinv, jnp.concatenate([xn, rhs_q], axis=2))
    lhs = jnp.concatenate([jnp.concatenate([a_rb, a_rk], axis=2),
                           jnp.concatenate([jnp.swapaxes(b2d, 1, 2), jnp.swapaxes(k2d, 1, 2)], axis=2)], axis=1)
    rhs = jnp.concatenate([pq, jnp.concatenate([jnp.zeros_like(vd), vd], axis=2)], axis=1)
    out = _bmm(lhs, rhs)
    r2 = xr + out[:, :n2, :LANES]
    y0 = out[:, :n2, LANES:]
    gm = out[:, n2:, :LANES]
    h0 = out[:, n2:, LANES:]
    gcol = jnp.sum(jnp.where(diag, jnp.exp(total), 0.0), axis=2, keepdims=True)
    return jnp.concatenate([r2, gm], axis=1), y0, gcol, h0


def _scan_body(rf, kf, vf, kkf, lwf, af, rb, kb, vb, kkb, lwb, ab, ka_ref, yf_o, yb_o, st_ref):
    @pl.when(pl.program_id(2) == 0)
    def _():
        st_ref[...] = jnp.zeros_like(st_ref)

    nch = SCAN_CHUNKS
    rows = nch * CHUNK
    n2 = 2 * CHUNK
    st_f = st_ref[0]
    st_b = st_ref[1]
    for grp in range(SCAN_GROUPS):
        f0 = grp * rows
        b0 = (SCAN_GROUPS - 1 - grp) * rows

        def both(f, b, f0=f0, b0=b0):
            return jnp.concatenate([f[f0:f0 + rows, :].astype(F32).reshape(nch, CHUNK, LANES),
                                    b[b0:b0 + rows, :].astype(F32).reshape(nch, CHUNK, LANES)], axis=0)

        rg, y0, gcol, h0 = _chunk_terms(both(rf, rb), both(kf, kb), both(vf, vb), both(kkf, kkb),
                                        both(lwf, lwb), both(af, ab), ka_ref[...], nch)
        for j in range(nch):
            jf = j
            jb = 2 * nch - 1 - j
            of = _bdot(rg[jf], st_f)
            ob = _bdot(rg[jb], st_b)
            yf = of[:n2] + y0[jf]
            yb = ob[:n2] + y0[jb]
            st_f = st_f * gcol[jf] + of[n2:] + h0[jf]
            st_b = st_b * gcol[jb] + ob[n2:] + h0[jb]
            fr = f0 + jf * CHUNK
            br = b0 + (jb - nch) * CHUNK
            yf_o[fr:fr + CHUNK, :] = (yf[:CHUNK] + yf[CHUNK:]).astype(yf_o.dtype)
            yb_o[br:br + CHUNK, :] = (yb[:CHUNK] + yb[CHUNK:]).astype(yb_o.dtype)
    st_ref[0] = st_f
    st_ref[1] = st_b


def _rwkv_scan(r, k, v, kkn, lw, a, k_a, batch, seq):
    m = r.shape[0]
    rows = SCAN_GROUPS * SCAN_CHUNKS * CHUNK
    ns = seq // rows
    npair = D_A // LANES
    fwd = lambda b, p, s: (b * ns + s, p)
    bwd = lambda b, p, s: (b * ns + ns - 1 - s, p)
    blk = (rows, LANES)
    fspec = pl.BlockSpec(blk, fwd)
    bspec = pl.BlockSpec(blk, bwd)
    fdir = pl.BlockSpec((None,) + blk, lambda b, p, s: (0,) + fwd(b, p, s))
    bdir = pl.BlockSpec((None,) + blk, lambda b, p, s: (1,) + bwd(b, p, s))
    out = jax.ShapeDtypeStruct((m, D_A), BF16)
    return pl.pallas_call(
        _scan_body,
        grid=(batch, npair, ns),
        in_specs=[fspec, fspec, fspec, fspec, fdir, fdir, bspec, bspec, bspec, bspec, bdir, bdir,
                  pl.BlockSpec((1, LANES), lambda b, p, s: (0, p))],
        out_specs=[fspec, bspec],
        out_shape=[out, out],
        scratch_shapes=[pltpu.VMEM((2, LANES, LANES), F32)],
        compiler_params=_cparams(("parallel", "parallel", "arbitrary")),
        name="rwkv_scan",
    )(r, k, v, kkn, lw, a, r, k, v, kkn, lw, a, k_a)


POOL_PAD = 8
POOL_COLS = 256
POOL_ROWS = 256


def _pool_body(p_ref, w_ref, inv_ref, z_ref, b_ref, s_ref, o_ref, pad_ref, acc_ref, *, seq):
    cb = pl.program_id(1)
    ncb = pl.num_programs(1)
    pad_ref[0:POOL_PAD, :] = jnp.zeros((POOL_PAD, POOL_COLS), F32)
    pad_ref[POOL_PAD + seq:, :] = jnp.zeros((POOL_PAD, POOL_COLS), F32)
    pad_ref[POOL_PAD:POOL_PAD + seq, :] = p_ref[...].astype(F32)

    @pl.when(cb == 0)
    def _():
        acc_ref[...] = jnp.zeros_like(acc_ref)

    lane = lax.broadcasted_iota(jnp.int32, (1, POOL_COLS), 1) + cb * POOL_COLS
    group = lane // POOL_CH
    w = w_ref[...]

    def rows(i, carry):
        base = pl.multiple_of(i * POOL_ROWS, POOL_ROWS)
        nwin = POOL_ROWS + 2 * POOL_PAD
        win = pad_ref[pl.ds(base, nwin), :]
        pair = win + pltpu.roll(win, nwin - 1, 0)
        at = lambda a, d: a[POOL_PAD + d:POOL_PAD + d + POOL_ROWS]
        x = at(win, 0)
        s1 = x + at(win, -1) + at(win, 1)
        s2 = s1 + at(win, -2) + at(win, 2)
        s4 = s2 + at(pair, -4) + at(pair, 3)
        s8 = s4 + at(pair, -8) + at(pair, -6) + at(pair, 5) + at(pair, 7)
        total = jnp.where(group == 0, s1, jnp.where(group == 1, s2, jnp.where(group == 2, s4, s8)))
        mean = total * inv_ref[pl.ds(base, POOL_ROWS), :]
        acc_ref[pl.ds(base, POOL_ROWS), :] += _bdot(mean - x, w)
        return carry

    lax.fori_loop(0, seq // POOL_ROWS, rows, 0)

    @pl.when(cb == ncb - 1)
    def _():
        o_ref[...] = ((acc_ref[...] + b_ref[...]) * s_ref[...] * z_ref[...].astype(F32)).astype(o_ref.dtype)


def _pool_inv_count(seq):
    half = jnp.repeat(jnp.array(POOL_HALF, jnp.int32), POOL_CH)[None, :]
    t = jnp.arange(seq, dtype=jnp.int32)[:, None]
    count = jnp.minimum(t + half, seq - 1) - jnp.maximum(t - half, 0) + 1
    return 1.0 / count.astype(F32)


def _pool(p, zb, w_bd, inv_count, b, scale, batch, seq):
    m = p.shape[0]
    ncb = D_B // POOL_COLS
    full = pl.BlockSpec((seq, D_B), lambda bi, cb: (bi, 0))
    vec = pl.BlockSpec((1, D_B), lambda bi, cb: (0, 0))
    return pl.pallas_call(
        functools.partial(_pool_body, seq=seq),
        grid=(batch, ncb),
        in_specs=[pl.BlockSpec((seq, POOL_COLS), lambda bi, cb: (bi, cb)),
                  pl.BlockSpec((POOL_COLS, D_B), lambda bi, cb: (cb, 0)),
                  pl.BlockSpec((seq, POOL_COLS), lambda bi, cb: (0, cb)),
                  full, vec, vec],
        out_specs=full,
        out_shape=jax.ShapeDtypeStruct((m, D_B), BF16),
        scratch_shapes=[pltpu.VMEM((seq + 2 * POOL_PAD, POOL_COLS), F32), pltpu.VMEM((seq, D_B), F32)],
        compiler_params=_cparams(("parallel", "arbitrary")),
        name="pool_mixer",
    )(p, w_bd, inv_count, zb, b, scale)


def _rope_matrix():
    half = HEAD_DIM // 2
    src = lax.broadcasted_iota(jnp.int32, (LANES, LANES), 0)
    dst = lax.broadcasted_iota(jnp.int32, (LANES, LANES), 1)
    low = (dst % HEAD_DIM) < half
    return jnp.where(low & (src == dst + half), -1.0, jnp.where(~low & (src == dst - half), 1.0, 0.0)).astype(BF16)


def _rope(t, cos, sin, rot_matrix):
    return t * cos + _split_dot(t, rot_matrix, 2) * sin


ATTN_SIDE = 64
ATTN_TILES = 4
COMBINE_ROWS = 256


def _attn_body(q0, q1, q2, k0, k1, k2, v0, v1, v2, cos_ref, sin_ref, zc_ref, o_ref,
               qp_ref, kp_ref, vp_ref, op_ref, lp_ref, on_ref, ln_ref, *, seq):
    q_refs = (q0, q1, q2)
    k_refs = (k0, k1, k2)
    v_refs = (v0, v1, v2)
    scale = HEAD_DIM ** -0.5
    rot_matrix = _rope_matrix()
    for g, (half_span, dil) in enumerate(ATTN_GROUPS):
        cls = seq // dil
        for r in range(dil):
            rows = pl.ds(r, cls, stride=dil) if dil > 1 else pl.ds(0, seq)
            dst = pl.ds(r * cls, cls)
            cos = cos_ref[rows, :]
            sin = sin_ref[rows, :]
            qp_ref[g, dst, :] = (_rope(q_refs[g][rows, :], cos, sin, rot_matrix) * scale).astype(BF16)
            kp_ref[g, dst, :] = _rope(k_refs[g][rows, :], cos, sin, rot_matrix).astype(BF16)
            vp_ref[g, dst, :] = v_refs[g][rows, :].astype(BF16)

    nt = ATTN_TILES
    first = lax.broadcasted_iota(jnp.int32, (nt, Q_TILE, LANES), 2) < HEAD_DIM
    for g, (half_span, dil) in enumerate(ATTN_GROUPS):
        cls = seq // dil
        klen = min(cls, Q_TILE + 2 * ATTN_SIDE)
        rel = (lax.broadcasted_iota(jnp.int32, (Q_TILE, klen), 1)
               - lax.broadcasted_iota(jnp.int32, (Q_TILE, klen), 0))

        def tiles(it, carry, g=g, cls=cls, klen=klen, rel=rel):
            row0 = pl.multiple_of(it * (nt * Q_TILE), nt * Q_TILE)
            keys, vals, valid = [], [], []
            for j in range(nt):
                qrow = row0 + j * Q_TILE
                base = (qrow // cls) * cls
                krow = pl.multiple_of(base + jnp.clip(qrow - base - ATTN_SIDE, 0, cls - klen), ATTN_SIDE)
                keys.append(kp_ref[g, pl.ds(krow, klen), :])
                vals.append(vp_ref[g, pl.ds(krow, klen), :])
                valid.append(jnp.abs(rel + (krow - qrow)) <= ATTN_SIDE)
            q = qp_ref[g, pl.ds(row0, nt * Q_TILE), :].reshape(nt, Q_TILE, LANES)
            zero = jnp.zeros_like(q)
            qm = jnp.concatenate([jnp.where(first, q, zero), jnp.where(first, zero, q)], axis=0)
            keys = jnp.stack(keys + keys)
            vals = jnp.stack(vals + vals)
            valid = jnp.stack(valid + valid)
            s = jnp.where(valid, _bmm_nt(qm, keys), NEG_BIG)
            mx = jnp.max(s, axis=2, keepdims=True)
            e = jnp.exp(s - mx)
            den = jnp.sum(e, axis=2, keepdims=True)
            o = _bmm(e, vals) / den
            lse = jnp.broadcast_to(mx + jnp.log(den), o.shape)
            rows = pl.ds(row0, nt * Q_TILE)
            op_ref[g, rows, :] = jnp.where(first, o[:nt], o[nt:]).reshape(nt * Q_TILE, LANES)
            lp_ref[g, rows, :] = jnp.where(first, lse[:nt], lse[nt:]).reshape(nt * Q_TILE, LANES)
            return carry

        lax.fori_loop(0, seq // (nt * Q_TILE), tiles, 0)
        if dil > 1:
            for r in range(dil):
                src = pl.ds(r * cls, cls)
                on_ref[g - 1, pl.ds(r, cls, stride=dil), :] = op_ref[g, src, :]
                ln_ref[g - 1, pl.ds(r, cls, stride=dil), :] = lp_ref[g, src, :]

    def combine(i, carry):
        rows = pl.ds(pl.multiple_of(i * COMBINE_ROWS, COMBINE_ROWS), COMBINE_ROWS)
        ls = [lp_ref[0, rows, :], ln_ref[0, rows, :], ln_ref[1, rows, :]]
        os_ = [op_ref[0, rows, :], on_ref[0, rows, :], on_ref[1, rows, :]]
        top = jnp.maximum(jnp.maximum(ls[0], ls[1]), ls[2])
        ws = [jnp.exp(l - top) for l in ls]
        tot = ws[0] + ws[1] + ws[2]
        out = (ws[0] * os_[0] + ws[1] * os_[1] + ws[2] * os_[2]) / tot * zc_ref[rows, :].astype(F32)
        o_ref[rows, :] = out.astype(o_ref.dtype)
        return carry

    lax.fori_loop(0, seq // COMBINE_ROWS, combine, 0)


def _attention(qkv, zc, cos, sin, batch, seq):
    m = qkv.shape[0]
    ncol = D_C // LANES
    per_group = D_C_OUT // LANES
    ngroup = len(ATTN_GROUPS)
    spec = lambda sec, g: pl.BlockSpec((seq, LANES), lambda b, jp: (b, sec * ncol + g * per_group + jp))
    tab = pl.BlockSpec((seq, LANES), lambda b, jp: (0, 0))
    oz = pl.BlockSpec((seq, LANES), lambda b, jp: (b, jp))
    perm_bf = pltpu.VMEM((ngroup, seq, LANES), BF16)
    perm_f32 = pltpu.VMEM((ngroup, seq, LANES), F32)
    nat_f32 = pltpu.VMEM((ngroup - 1, seq, LANES), F32)
    return pl.pallas_call(
        functools.partial(_attn_body, seq=seq),
        grid=(batch, per_group),
        in_specs=[spec(0, 0), spec(0, 1), spec(0, 2), spec(1, 0), spec(1, 1), spec(1, 2),
                  spec(2, 0), spec(2, 1), spec(2, 2), tab, tab, oz],
        out_specs=oz,
        out_shape=jax.ShapeDtypeStruct((m, D_C_OUT), BF16),
        scratch_shapes=[perm_bf, perm_bf, perm_bf, perm_f32, perm_f32, nat_f32, nat_f32],
        compiler_params=_cparams(("parallel", "parallel")),
        name="dilated_attention",
    )(qkv, qkv, qkv, qkv, qkv, qkv, qkv, qkv, qkv, cos, sin, zc)


def _merge_body(x_ref, yf_ref, yb_ref, bonus_ref, za_ref, ob_ref, oc_ref, g_ref,
                gng_ref, gnb_ref, pa_ref, pb_ref, pc_ref, wo_ref, lng_ref, lnb_ref, o_ref, ob16_ref):
    ones = _head_ones(LANES)
    y = yf_ref[...].astype(F32) + yb_ref[...].astype(F32)
    parts = []
    for cb in range(D_A // LANES):
        sl = slice(cb * LANES, (cb + 1) * LANES)
        yc = y[:, sl]
        mu = _bdot(yc, ones) * (1.0 / HEAD_DIM)
        d = yc - mu
        var = _bdot(d * d, ones) * (1.0 / HEAD_DIM)
        parts.append(d * lax.rsqrt(var + RWKV_GN_EPS))
    yn = jnp.concatenate(parts, axis=1)
    o_a = (yn * gng_ref[...] + gnb_ref[...] + bonus_ref[...].astype(F32)) * za_ref[...].astype(F32)
    merged = (g_ref[:, 0:D_MODEL].astype(F32) * _bdot(o_a, pa_ref[...])
              + g_ref[:, D_MODEL:2 * D_MODEL].astype(F32) * _bdot(ob_ref[...], pb_ref[...])
              + g_ref[:, 2 * D_MODEL:3 * D_MODEL].astype(F32) * _bdot(oc_ref[...], pc_ref[...]))
    t = DEEPNORM_ALPHA * x_ref[...] + _bdot(merged, wo_ref[...])
    mean = jnp.mean(t, axis=1, keepdims=True)
    d = t - mean
    var = jnp.mean(d * d, axis=1, keepdims=True)
    out = d * lax.rsqrt(var + LN_EPS) * lng_ref[...] + lnb_ref[...]
    o_ref[...] = out
    ob16_ref[...] = out.astype(BF16)


def _merge(x, yf, yb, bonus, za, ob, oc, gates, gn_g, gn_b, pa, pb, pc, wo, ln_g, ln_b):
    m = x.shape[0]
    tm = min(256, m)
    rows = lambda n: pl.BlockSpec((tm, n), lambda i: (i, 0))
    full = lambda a: pl.BlockSpec(a.shape, lambda i: (0, 0))
    return pl.pallas_call(
        _merge_body,
        grid=(m // tm,),
        in_specs=[rows(D_MODEL), rows(D_A), rows(D_A), rows(D_A), rows(D_A), rows(D_B), rows(D_C_OUT),
                  rows(3 * D_MODEL), full(gn_g), full(gn_b), full(pa), full(pb), full(pc), full(wo),
                  full(ln_g), full(ln_b)],
        out_specs=[rows(D_MODEL), rows(D_MODEL)],
        out_shape=[jax.ShapeDtypeStruct((m, D_MODEL), F32), jax.ShapeDtypeStruct((m, D_MODEL), BF16)],
        compiler_params=_cparams(("parallel",)),
        name="merge_norm",
    )(x, yf, yb, bonus, za, ob, oc, gates, gn_g, gn_b, pa, pb, pc, wo, ln_g, ln_b)


def _rope_tables(seq):
    inv = jnp.power(ROPE_THETA, -jnp.arange(0, HEAD_DIM, 2, dtype=F32) / HEAD_DIM)
    ang = jnp.arange(seq, dtype=F32)[:, None] * inv[None, :]
    ang = jnp.concatenate([ang, ang, ang, ang], axis=-1)
    return jnp.cos(ang), jnp.sin(ang)


def _pad_lora(w_up):
    z = jnp.zeros_like(w_up[0])
    return jnp.stack([jnp.concatenate([w_up[0], z], axis=0), jnp.concatenate([z, w_up[1]], axis=0)])


def _block_diag(w_g):
    g, c, _ = w_g.shape
    eye = jnp.eye(g, dtype=w_g.dtype)
    return (eye[:, None, :, None] * w_g[:, :, None, :]).reshape(g * c, g * c)


def kernel(x, w_in, b_in, rwkv_mu, rwkv_w0, rwkv_w_up, rwkv_a0, rwkv_a_up, rwkv_k_k, rwkv_k_a, rwkv_r_k, rwkv_gn_g, rwkv_gn_b, pool_w, pool_b, pool_scale, proj_a, proj_b, proj_c, w_out, ln_g, ln_b):
    batch, seq, _ = x.shape
    m = batch * seq
    cos, sin = _rope_tables(seq)
    inv_count = _pool_inv_count(seq)
    row = lambda t: t.reshape(1, -1)
    bounds = [0]
    for width in (N_SHIFT, D_A, D_B, D_B, 3 * D_C, D_C_OUT, 3 * D_MODEL):
        bounds.append(bounds[-1] + width)
    acts = (None, "silu", None, "silu", None, "silu", "sigmoid")
    dtypes = (F32, BF16, BF16, BF16, F32, BF16, BF16)

    xf = x.reshape(m, D_MODEL)
    xb = xf.astype(BF16)
    for l in range(DEPTH):
        w_l = w_in[l].astype(BF16)
        b_l = b_in[l]
        sec = [_linear(xb, w_l[:, lo:hi], row(b_l[lo:hi]), act, dt)
               for lo, hi, act, dt in zip(bounds[:-1], bounds[1:], acts, dtypes)]
        u, za, p, zb, qkv, zc, gates = sec

        r, k, v, kkn, lw, a, bonus = _rwkv_prep(
            u, rwkv_mu[l], rwkv_w0[l], rwkv_a0[l],
            _pad_lora(rwkv_w_up[l]).astype(BF16), _pad_lora(rwkv_a_up[l]).astype(BF16),
            row(rwkv_k_k[l]), row(rwkv_r_k[l]), seq)
        yf, yb = _rwkv_scan(r, k, v, kkn, lw, a, row(rwkv_k_a[l]), batch, seq)
        ob = _pool(p, zb, _block_diag(pool_w[l]).astype(BF16), inv_count, row(pool_b[l]), row(pool_scale[l]),
                   batch, seq)
        oc = _attention(qkv, zc, cos, sin, batch, seq)
        xf, xb = _merge(xf, yf, yb, bonus, za, ob, oc, gates, row(rwkv_gn_g[l]), row(rwkv_gn_b[l]),
                        proj_a[l].astype(BF16), proj_b[l].astype(BF16), proj_c[l].astype(BF16),
                        w_out[l].astype(BF16), row(ln_g[l]), row(ln_b[l]))
    return xf.reshape(batch, seq, D_MODEL)
```

```python
import functools
import math

import jax
import jax.numpy as jnp
from jax import lax
from jax.experimental import pallas as pl
from jax.experimental.pallas import tpu as pltpu

F32 = jnp.float32
BF16 = jnp.bfloat16

D_MODEL = 1024
DEPTH = 4
HEAD_DIM = 64
LANES = 128
D_A = 1024
LORA = 64
N_SHIFT = 3 * D_A + 4 * LORA
RWKV_GN_EPS = 64e-5
LOG_DECAY_SCALE = -math.exp(-0.5)
POOL_HALF = (1, 2, 4, 8)
POOL_CH = 192
D_B = 768
D_C = 768
D_C_OUT = 256
ATTN_GROUPS = ((64, 1), (256, 4), (1024, 16))
ROPE_THETA = 10000.0
DEEPNORM_ALPHA = (2 * DEPTH) ** 0.25
LN_EPS = 1e-5
CHUNK = 64
Q_TILE = 128
NEG_BIG = -1e30
VMEM_LIMIT = 52 * 1024 * 1024


def _cparams(sem):
    return pltpu.CompilerParams(dimension_semantics=sem, vmem_limit_bytes=VMEM_LIMIT)


def _bdot(a, b):
    return jnp.dot(a.astype(BF16), b.astype(BF16), preferred_element_type=F32)


def _bdot_nt(a, b):
    return lax.dot_general(a.astype(BF16), b.astype(BF16), (((1,), (1,)), ((), ())),
                           preferred_element_type=F32)


def _split_dot(a, b_exact, terms):
    acc = None
    rem = a
    for _ in range(terms):
        piece = rem.astype(BF16)
        part = jnp.dot(piece, b_exact, preferred_element_type=F32)
        acc = part if acc is None else acc + part
        rem = rem - piece.astype(F32)
    return acc


def _head_ones(n):
    r = lax.broadcasted_iota(jnp.int32, (n, n), 0) // HEAD_DIM
    c = lax.broadcasted_iota(jnp.int32, (n, n), 1) // HEAD_DIM
    return (r == c).astype(BF16)


def _sigmoid(x):
    return 1.0 / (1.0 + jnp.exp(-x))


def _linear_body(x_ref, w_ref, b_ref, o_ref, *, act):
    acc = jnp.dot(x_ref[...], w_ref[...], preferred_element_type=F32) + b_ref[...]
    if act == "silu":
        acc = acc * _sigmoid(acc)
    elif act == "sigmoid":
        acc = _sigmoid(acc)
    o_ref[...] = acc.astype(o_ref.dtype)


def _pick_tile(n, cap):
    best = LANES
    for t in range(LANES, cap + 1, LANES):
        if n % t == 0:
            best = t
    return best


def _linear(x, w, b, act, out_dtype):
    m, k = x.shape
    n = w.shape[1]
    tm = min(1024, m)
    tn = _pick_tile(n, 1664)
    return pl.pallas_call(
        functools.partial(_linear_body, act=act),
        grid=(n // tn, m // tm),
        in_specs=[pl.BlockSpec((tm, k), lambda j, i: (i, 0)),
                  pl.BlockSpec((k, tn), lambda j, i: (0, j)),
                  pl.BlockSpec((1, tn), lambda j, i: (0, j))],
        out_specs=pl.BlockSpec((tm, tn), lambda j, i: (i, j)),
        out_shape=jax.ShapeDtypeStruct((m, n), out_dtype),
        compiler_params=_cparams(("parallel", "parallel")),
        name="linear_" + (act or "id"),
    )(x, w, b)


def _prep_body(u_ref, up_ref, un_ref, mu_ref, w0_ref, a0_ref, wup_ref, aup_ref, kk_ref, rk_ref,
               r_o, k_o, v_o, kkn_o, lw_o, a_o, bonus_o, *, tm, seq):
    i = pl.program_id(0)
    u = u_ref[...]
    row = lax.broadcasted_iota(jnp.int32, (8, 1), 0)
    t0 = (i * tm) % seq
    prev_edge = jnp.where(t0 == 0, 0.0, up_ref[7:8, :])
    next_edge = jnp.where(t0 + tm == seq, 0.0, un_ref[0:1, :])
    prev = pltpu.roll(u, 1, 0)
    prev = jnp.concatenate([jnp.where(row == 0, prev_edge, prev[0:8]), prev[8:]], axis=0)
    nxt = pltpu.roll(u, tm - 1, 0)
    nxt = jnp.concatenate([nxt[:tm - 8], jnp.where(row == 7, next_edge, nxt[tm - 8:])], axis=0)
    mu_prev = mu_ref[0:1, :]
    mu_next = mu_ref[1:2, :]
    u = (1.0 - mu_prev - mu_next) * u + mu_prev * prev + mu_next * nxt

    r = u[:, 0:D_A]
    k = u[:, D_A:2 * D_A]
    v = u[:, 2 * D_A:3 * D_A]
    wd = jnp.tanh(u[:, 3 * D_A:3 * D_A + 2 * LORA])
    ad = u[:, 3 * D_A + 2 * LORA:N_SHIFT]
    r_o[...] = r.astype(r_o.dtype)
    k_o[...] = k.astype(k_o.dtype)
    v_o[...] = v.astype(v_o.dtype)
    for z in range(2):
        warg = w0_ref[z:z + 1, :] + _bdot(wd, wup_ref[z])
        lw_o[z] = LOG_DECAY_SCALE * _sigmoid(warg)
        a_o[z] = _sigmoid(a0_ref[z:z + 1, :] + _bdot(ad, aup_ref[z])).astype(a_o.dtype)

    ones = _head_ones(LANES)
    for cb in range(D_A // LANES):
        sl = slice(cb * LANES, (cb + 1) * LANES)
        kk = k[:, sl] * kk_ref[:, sl]
        ss = _bdot(kk * kk, ones)
        kkn_o[:, sl] = (kk * lax.rsqrt(ss + 1e-12)).astype(kkn_o.dtype)
        rk = _bdot(r[:, sl] * k[:, sl] * rk_ref[:, sl], ones)
        bonus_o[:, sl] = (rk * v[:, sl]).astype(bonus_o.dtype)


def _rwkv_prep(u, mu, w0, a0, wup, aup, k_k, r_k, seq):
    m = u.shape[0]
    tm = min(256, seq)
    nb8 = m // 8
    full = lambda shape: pl.BlockSpec(shape, lambda i: (0,) * len(shape))
    row_spec = pl.BlockSpec((tm, D_A), lambda i: (i, 0))
    dir_spec = pl.BlockSpec((2, tm, D_A), lambda i: (0, i, 0))
    act = jax.ShapeDtypeStruct((m, D_A), BF16)
    act2 = jax.ShapeDtypeStruct((2, m, D_A), BF16)
    logw = jax.ShapeDtypeStruct((2, m, D_A), F32)
    return pl.pallas_call(
        functools.partial(_prep_body, tm=tm, seq=seq),
        grid=(m // tm,),
        in_specs=[pl.BlockSpec((tm, N_SHIFT), lambda i: (i, 0)),
                  pl.BlockSpec((8, N_SHIFT), lambda i: (jnp.maximum(i * (tm // 8) - 1, 0), 0)),
                  pl.BlockSpec((8, N_SHIFT), lambda i: (jnp.minimum((i + 1) * (tm // 8), nb8 - 1), 0)),
                  full((2, N_SHIFT)), full((2, D_A)), full((2, D_A)),
                  full((2, LANES, D_A)), full((2, LANES, D_A)), full((1, D_A)), full((1, D_A))],
        out_specs=[row_spec, row_spec, row_spec, row_spec, dir_spec, dir_spec, row_spec],
        out_shape=[act, act, act, act, logw, act2, act],
        compiler_params=_cparams(("parallel",)),
        name="rwkv_prep",
    )(u, u, u, mu, w0, a0, wup, aup, k_k, r_k)


SCAN_CHUNKS = 4
SCAN_GROUPS = 2


def _bmm(a, b):
    return lax.dot_general(a.astype(BF16), b.astype(BF16), (((2,), (1,)), ((0,), (0,))),
                           preferred_element_type=F32)


def _bmm_nt(a, b):
    return lax.dot_general(a.astype(BF16), b.astype(BF16), (((2,), (2,)), ((0,), (0,))),
                           preferred_element_type=F32)


def _stack_heads(t):
    lane = lax.broadcasted_iota(jnp.int32, t.shape, 2)
    first = lane < HEAD_DIM
    return jnp.concatenate([jnp.where(first, t, 0.0), jnp.where(first, 0.0, t)], axis=1)


def _chunk_terms(r, k, v, kk, lw, a, k_a, nfwd, interleave):
    n, c, _ = r.shape
    n2 = 2 * c
    rev = lax.broadcasted_iota(jnp.int32, (n, c, 3 * c), 0) >= nfwd
    ri = lax.broadcasted_iota(jnp.int32, (n, c, 3 * c), 1)
    ci = lax.broadcasted_iota(jnp.int32, (n, c, 3 * c), 2) % c
    tri = ((rev & (ci >= ri)) | (~rev & (ci <= ri))).astype(BF16)
    pieces = []
    rem = lw
    for _ in range(3):
        pieces.append(rem.astype(BF16))
        rem = rem - pieces[-1].astype(F32)
    cum = _bmm(tri, jnp.concatenate(pieces, axis=1))
    is_rev = lax.broadcasted_iota(jnp.int32, (n, 1, LANES), 0) >= nfwd
    total = jnp.where(is_rev, cum[:, 0:1, :], cum[:, c - 1:c, :])
    kd = k * (1.0 + (a - 1.0) * k_a)
    bb = kk * a
    e_neg = jnp.exp(-cum)
    e_rem = jnp.exp(total - cum)
    xn = _stack_heads(-kk * jnp.exp(cum - lw))
    xr = _stack_heads(r * jnp.exp(cum))
    vd = _stack_heads(v)
    b2d = _stack_heads(bb * e_rem)
    k2d = _stack_heads(kd * e_rem)
    bt = bb * e_neg
    kt = kd * e_neg
    bd = jnp.concatenate([bt, bt], axis=1)
    kdd = jnp.concatenate([kt, kt], axis=1)

    rev2 = lax.broadcasted_iota(jnp.int32, (n, n2, n2), 0) >= nfwd
    rr = lax.broadcasted_iota(jnp.int32, (n, n2, n2), 1)
    cc = lax.broadcasted_iota(jnp.int32, (n, n2, n2), 2)
    same = (rr // c) == (cc // c)
    tt = rr % c
    ss = cc % c
    diag = rr == cc
    strict = same & ((rev2 & (ss > tt)) | (~rev2 & (ss < tt)))
    incl = strict | diag

    aa = _bmm_nt(jnp.concatenate([xn, xr], axis=1), jnp.concatenate([bd, kdd], axis=1))
    a_ab = jnp.where(strict, aa[:, :n2, :n2], 0.0)
    a_ak = jnp.where(strict, aa[:, :n2, n2:], 0.0)
    a_rb = jnp.where(incl, aa[:, n2:, :n2], 0.0)
    a_rk = jnp.where(incl, aa[:, n2:, n2:], 0.0)
    interleave()

    tinv = jnp.where(diag, 1.0, a_ab)
    apow = _bmm(a_ab, a_ab)
    for step in range(max(1, (c - 1).bit_length()) - 2):
        both = _bmm(apow, jnp.concatenate([apow, tinv], axis=2))
        apow = both[:, :, :n2]
        tinv = tinv + both[:, :, n2:]
        if step % 2 == 1:
            interleave()
    tinv = tinv + _bmm(apow, tinv)

    rhs_q = _bmm(a_ak, vd)
    pq = _bmm(tinv, jnp.concatenate([xn, rhs_q], axis=2))
    interleave()
    lhs = jnp.concatenate([jnp.concatenate([a_rb, a_rk], axis=2),
                           jnp.concatenate([jnp.swapaxes(b2d, 1, 2), jnp.swapaxes(k2d, 1, 2)], axis=2)], axis=1)
    rhs = jnp.concatenate([pq, jnp.concatenate([jnp.zeros_like(vd), vd], axis=2)], axis=1)
    out = _bmm(lhs, rhs)
    r2 = xr + out[:, :n2, :LANES]
    y0 = out[:, :n2, LANES:]
    gm = out[:, n2:, :LANES]
    h0 = out[:, n2:, LANES:]
    gcol = jnp.sum(jnp.where(diag, jnp.exp(total), 0.0), axis=2, keepdims=True)
    return jnp.concatenate([r2, gm], axis=1), y0, gcol, h0


def _scan_body(rf, kf, vf, kkf, lwf, af, rb, kb, vb, kkb, lwb, ab, ka_ref, yf_o, yb_o,
               st_ref, rg_ref, y0_ref, gc_ref, h0_ref, *, steps_per_seq):
    g = pl.program_id(0)

    @pl.when(g == 0)
    def _():
        st_ref[...] = jnp.zeros_like(st_ref)
        rg_ref[...] = jnp.zeros_like(rg_ref)
        y0_ref[...] = jnp.zeros_like(y0_ref)
        gc_ref[...] = jnp.zeros_like(gc_ref)
        h0_ref[...] = jnp.zeros_like(h0_ref)

    nch = SCAN_CHUNKS
    rows = nch * CHUNK
    n2 = 2 * CHUNK
    rd = (g + 1) % 2
    wr = g % 2
    fresh = (g % steps_per_seq) == (1 % steps_per_seq)
    state = [jnp.where(fresh, 0.0, st_ref[0]), jnp.where(fresh, 0.0, st_ref[1])]
    pending = [(grp, j) for grp in range(SCAN_GROUPS) for j in range(nch)]

    def chain_step():
        grp, j = pending.pop(0)
        f0 = grp * rows
        b0 = (SCAN_GROUPS - 1 - grp) * rows
        for d, jc, row0 in ((0, j, f0 + j * CHUNK), (1, 2 * nch - 1 - j, b0 + (nch - 1 - j) * CHUNK)):
            out = jnp.dot(rg_ref[rd, grp, jc], state[d].astype(BF16), preferred_element_type=F32)
            y = out[:n2] + y0_ref[rd, grp, jc]
            state[d] = state[d] * gc_ref[rd, grp, jc] + out[n2:] + h0_ref[rd, grp, jc]
            o_ref = yf_o if d == 0 else yb_o
            o_ref[row0:row0 + CHUNK, :] = (y[:CHUNK] + y[CHUNK:]).astype(o_ref.dtype)

    for grp in range(SCAN_GROUPS):
        f0 = grp * rows
        b0 = (SCAN_GROUPS - 1 - grp) * rows

        def both(f, b, f0=f0, b0=b0):
            return jnp.concatenate([f[f0:f0 + rows, :].astype(F32).reshape(nch, CHUNK, LANES),
                                    b[b0:b0 + rows, :].astype(F32).reshape(nch, CHUNK, LANES)], axis=0)

        rg, y0, gcol, h0 = _chunk_terms(both(rf, rb), both(kf, kb), both(vf, vb), both(kkf, kkb),
                                        both(lwf, lwb), both(af, ab), ka_ref[...], nch, chain_step)
        rg_ref[wr, grp] = rg.astype(BF16)
        y0_ref[wr, grp] = y0
        gc_ref[wr, grp] = jnp.broadcast_to(gcol, h0.shape)
        h0_ref[wr, grp] = h0
    while pending:
        chain_step()
    st_ref[0] = state[0]
    st_ref[1] = state[1]


def _rwkv_scan(r, k, v, kkn, lw, a, k_a, batch, seq):
    m = r.shape[0]
    rows = SCAN_GROUPS * SCAN_CHUNKS * CHUNK
    ns = seq // rows
    npair = D_A // LANES
    nblocks = batch * npair * ns

    def place(blk_id, rev):
        s = blk_id % ns
        p = (blk_id // ns) % npair
        b = blk_id // (ns * npair)
        return (b * ns + (ns - 1 - s if rev else s), p)

    cur = lambda g: jnp.minimum(g, nblocks - 1)
    prev = lambda g: jnp.maximum(g - 1, 0)
    blk = (rows, LANES)
    fspec = pl.BlockSpec(blk, lambda g: place(cur(g), False))
    bspec = pl.BlockSpec(blk, lambda g: place(cur(g), True))
    fdir = pl.BlockSpec((None,) + blk, lambda g: (0,) + place(cur(g), False))
    bdir = pl.BlockSpec((None,) + blk, lambda g: (1,) + place(cur(g), True))
    out = jax.ShapeDtypeStruct((m, D_A), BF16)
    nchain = 2 * SCAN_CHUNKS
    terms = lambda rws, dt: pltpu.VMEM((2, SCAN_GROUPS, nchain, rws, LANES), dt)
    return pl.pallas_call(
        functools.partial(_scan_body, steps_per_seq=ns),
        grid=(nblocks + 1,),
        in_specs=[fspec, fspec, fspec, fspec, fdir, fdir, bspec, bspec, bspec, bspec, bdir, bdir,
                  pl.BlockSpec((1, LANES), lambda g: (0, place(cur(g), False)[1]))],
        out_specs=[pl.BlockSpec(blk, lambda g: place(prev(g), False)),
                   pl.BlockSpec(blk, lambda g: place(prev(g), True))],
        out_shape=[out, out],
        scratch_shapes=[pltpu.VMEM((2, LANES, LANES), F32), terms(2 * CHUNK + LANES, BF16),
                        terms(2 * CHUNK, F32), terms(LANES, F32), terms(LANES, F32)],
        compiler_params=_cparams(("arbitrary",)),
        name="rwkv_scan",
    )(r, k, v, kkn, lw, a, r, k, v, kkn, lw, a, k_a)


POOL_PAD = 8
POOL_COLS = 256
POOL_ROWS = 256


def _pool_body(p_ref, w_ref, inv_ref, z_ref, b_ref, s_ref, o_ref, pad_ref, acc_ref, *, seq):
    cb = pl.program_id(1)
    ncb = pl.num_programs(1)
    pad_ref[0:POOL_PAD, :] = jnp.zeros((POOL_PAD, POOL_COLS), F32)
    pad_ref[POOL_PAD + seq:, :] = jnp.zeros((POOL_PAD, POOL_COLS), F32)
    pad_ref[POOL_PAD:POOL_PAD + seq, :] = p_ref[...].astype(F32)

    @pl.when(cb == 0)
    def _():
        acc_ref[...] = jnp.zeros_like(acc_ref)

    lane = lax.broadcasted_iota(jnp.int32, (1, POOL_COLS), 1) + cb * POOL_COLS
    group = lane // POOL_CH
    w = w_ref[...]

    def rows(i, carry):
        base = pl.multiple_of(i * POOL_ROWS, POOL_ROWS)
        nwin = POOL_ROWS + 2 * POOL_PAD
        win = pad_ref[pl.ds(base, nwin), :]
        pair = win + pltpu.roll(win, nwin - 1, 0)
        at = lambda a, d: a[POOL_PAD + d:POOL_PAD + d + POOL_ROWS]
        x = at(win, 0)
        s1 = x + at(win, -1) + at(win, 1)
        s2 = s1 + at(win, -2) + at(win, 2)
        s4 = s2 + at(pair, -4) + at(pair, 3)
        s8 = s4 + at(pair, -8) + at(pair, -6) + at(pair, 5) + at(pair, 7)
        total = jnp.where(group == 0, s1, jnp.where(group == 1, s2, jnp.where(group == 2, s4, s8)))
        mean = total * inv_ref[pl.ds(base, POOL_ROWS), :]
        acc_ref[pl.ds(base, POOL_ROWS), :] += _bdot(mean - x, w)
        return carry

    lax.fori_loop(0, seq // POOL_ROWS, rows, 0)

    @pl.when(cb == ncb - 1)
    def _():
        o_ref[...] = ((acc_ref[...] + b_ref[...]) * s_ref[...] * z_ref[...].astype(F32)).astype(o_ref.dtype)


def _pool_inv_count(seq):
    half = jnp.repeat(jnp.array(POOL_HALF, jnp.int32), POOL_CH)[None, :]
    t = jnp.arange(seq, dtype=jnp.int32)[:, None]
    count = jnp.minimum(t + half, seq - 1) - jnp.maximum(t - half, 0) + 1
    return 1.0 / count.astype(F32)


def _pool(p, zb, w_bd, inv_count, b, scale, batch, seq):
    m = p.shape[0]
    ncb = D_B // POOL_COLS
    full = pl.BlockSpec((seq, D_B), lambda bi, cb: (bi, 0))
    vec = pl.BlockSpec((1, D_B), lambda bi, cb: (0, 0))
    return pl.pallas_call(
        functools.partial(_pool_body, seq=seq),
        grid=(batch, ncb),
        in_specs=[pl.BlockSpec((seq, POOL_COLS), lambda bi, cb: (bi, cb)),
                  pl.BlockSpec((POOL_COLS, D_B), lambda bi, cb: (cb, 0)),
                  pl.BlockSpec((seq, POOL_COLS), lambda bi, cb: (0, cb)),
                  full, vec, vec],
        out_specs=full,
        out_shape=jax.ShapeDtypeStruct((m, D_B), BF16),
        scratch_shapes=[pltpu.VMEM((seq + 2 * POOL_PAD, POOL_COLS), F32), pltpu.VMEM((seq, D_B), F32)],
        compiler_params=_cparams(("parallel", "arbitrary")),
        name="pool_mixer",
    )(p, w_bd, inv_count, zb, b, scale)


def _rope_matrix():
    half = HEAD_DIM // 2
    src = lax.broadcasted_iota(jnp.int32, (LANES, LANES), 0)
    dst = lax.broadcasted_iota(jnp.int32, (LANES, LANES), 1)
    low = (dst % HEAD_DIM) < half
    return jnp.where(low & (src == dst + half), -1.0, jnp.where(~low & (src == dst - half), 1.0, 0.0)).astype(BF16)


def _rope(t, cos, sin, rot_matrix):
    return t * cos + _split_dot(t, rot_matrix, 2) * sin


ATTN_SIDE = 64
ATTN_TILES = 4
COMBINE_ROWS = 256


def _attn_body(q0, q1, q2, k0, k1, k2, v0, v1, v2, cos_ref, sin_ref, zc_ref, o_ref,
               qp_ref, kp_ref, vp_ref, op_ref, lp_ref, on_ref, ln_ref, *, seq):
    q_refs = (q0, q1, q2)
    k_refs = (k0, k1, k2)
    v_refs = (v0, v1, v2)
    scale = HEAD_DIM ** -0.5
    rot_matrix = _rope_matrix()
    for g, (half_span, dil) in enumerate(ATTN_GROUPS):
        cls = seq // dil
        for r in range(dil):
            rows = pl.ds(r, cls, stride=dil) if dil > 1 else pl.ds(0, seq)
            dst = pl.ds(r * cls, cls)
            cos = cos_ref[rows, :]
            sin = sin_ref[rows, :]
            qp_ref[g, dst, :] = (_rope(q_refs[g][rows, :], cos, sin, rot_matrix) * scale).astype(BF16)
            kp_ref[g, dst, :] = _rope(k_refs[g][rows, :], cos, sin, rot_matrix).astype(BF16)
            vp_ref[g, dst, :] = v_refs[g][rows, :].astype(BF16)

    nt = ATTN_TILES
    first = lax.broadcasted_iota(jnp.int32, (nt, Q_TILE, LANES), 2) < HEAD_DIM
    for g, (half_span, dil) in enumerate(ATTN_GROUPS):
        cls = seq // dil
        klen = min(cls, Q_TILE + 2 * ATTN_SIDE)
        rel = (lax.broadcasted_iota(jnp.int32, (Q_TILE, klen), 1)
               - lax.broadcasted_iota(jnp.int32, (Q_TILE, klen), 0))

        def tiles(it, carry, g=g, cls=cls, klen=klen, rel=rel):
            row0 = pl.multiple_of(it * (nt * Q_TILE), nt * Q_TILE)
            keys, vals, valid = [], [], []
            for j in range(nt):
                qrow = row0 + j * Q_TILE
                base = (qrow // cls) * cls
                krow = pl.multiple_of(base + jnp.clip(qrow - base - ATTN_SIDE, 0, cls - klen), ATTN_SIDE)
                keys.append(kp_ref[g, pl.ds(krow, klen), :])
                vals.append(vp_ref[g, pl.ds(krow, klen), :])
                valid.append(jnp.abs(rel + (krow - qrow)) <= ATTN_SIDE)
            q = qp_ref[g, pl.ds(row0, nt * Q_TILE), :].reshape(nt, Q_TILE, LANES)
            zero = jnp.zeros_like(q)
            qm = jnp.concatenate([jnp.where(first, q, zero), jnp.where(first, zero, q)], axis=0)
            keys = jnp.stack(keys + keys)
            vals = jnp.stack(vals + vals)
            valid = jnp.stack(valid + valid)
            s = jnp.where(valid, _bmm_nt(qm, keys), NEG_BIG)
            mx = jnp.max(s, axis=2, keepdims=True)
            e = jnp.exp(s - mx)
            den = jnp.sum(e, axis=2, keepdims=True)
            o = _bmm(e, vals) / den
            lse = jnp.broadcast_to(mx + jnp.log(den), o.shape)
            rows = pl.ds(row0, nt * Q_TILE)
            op_ref[g, rows, :] = jnp.where(first, o[:nt], o[nt:]).reshape(nt * Q_TILE, LANES)
            lp_ref[g, rows, :] = jnp.where(first, lse[:nt], lse[nt:]).reshape(nt * Q_TILE, LANES)
            return carry

        lax.fori_loop(0, seq // (nt * Q_TILE), tiles, 0)
        if dil > 1:
            for r in range(dil):
                src = pl.ds(r * cls, cls)
                on_ref[g - 1, pl.ds(r, cls, stride=dil), :] = op_ref[g, src, :]
                ln_ref[g - 1, pl.ds(r, cls, stride=dil), :] = lp_ref[g, src, :]

    def combine(i, carry):
        rows = pl.ds(pl.multiple_of(i * COMBINE_ROWS, COMBINE_ROWS), COMBINE_ROWS)
        ls = [lp_ref[0, rows, :], ln_ref[0, rows, :], ln_ref[1, rows, :]]
        os_ = [op_ref[0, rows, :], on_ref[0, rows, :], on_ref[1, rows, :]]
        top = jnp.maximum(jnp.maximum(ls[0], ls[1]), ls[2])
        ws = [jnp.exp(l - top) for l in ls]
        tot = ws[0] + ws[1] + ws[2]
        out = (ws[0] * os_[0] + ws[1] * os_[1] + ws[2] * os_[2]) / tot * zc_ref[rows, :].astype(F32)
        o_ref[rows, :] = out.astype(o_ref.dtype)
        return carry

    lax.fori_loop(0, seq // COMBINE_ROWS, combine, 0)


def _attention(qkv, zc, cos, sin, batch, seq):
    m = qkv.shape[0]
    ncol = D_C // LANES
    per_group = D_C_OUT // LANES
    ngroup = len(ATTN_GROUPS)
    spec = lambda sec, g: pl.BlockSpec((seq, LANES), lambda b, jp: (b, sec * ncol + g * per_group + jp))
    tab = pl.BlockSpec((seq, LANES), lambda b, jp: (0, 0))
    oz = pl.BlockSpec((seq, LANES), lambda b, jp: (b, jp))
    perm_bf = pltpu.VMEM((ngroup, seq, LANES), BF16)
    perm_f32 = pltpu.VMEM((ngroup, seq, LANES), F32)
    nat_f32 = pltpu.VMEM((ngroup - 1, seq, LANES), F32)
    return pl.pallas_call(
        functools.partial(_attn_body, seq=seq),
        grid=(batch, per_group),
        in_specs=[spec(0, 0), spec(0, 1), spec(0, 2), spec(1, 0), spec(1, 1), spec(1, 2),
                  spec(2, 0), spec(2, 1), spec(2, 2), tab, tab, oz],
        out_specs=oz,
        out_shape=jax.ShapeDtypeStruct((m, D_C_OUT), BF16),
        scratch_shapes=[perm_bf, perm_bf, perm_bf, perm_f32, perm_f32, nat_f32, nat_f32],
        compiler_params=_cparams(("parallel", "parallel")),
        name="dilated_attention",
    )(qkv, qkv, qkv, qkv, qkv, qkv, qkv, qkv, qkv, cos, sin, zc)


def _merge_body(x_ref, yf_ref, yb_ref, bonus_ref, za_ref, ob_ref, oc_ref, g_ref,
                gng_ref, gnb_ref, pa_ref, pb_ref, pc_ref, wo_ref, lng_ref, lnb_ref, o_ref, ob16_ref):
    ones = _head_ones(LANES)
    y = yf_ref[...].astype(F32) + yb_ref[...].astype(F32)
    parts = []
    for cb in range(D_A // LANES):
        sl = slice(cb * LANES, (cb + 1) * LANES)
        yc = y[:, sl]
        mu = _bdot(yc, ones) * (1.0 / HEAD_DIM)
        d = yc - mu
        var = _bdot(d * d, ones) * (1.0 / HEAD_DIM)
        parts.append(d * lax.rsqrt(var + RWKV_GN_EPS))
    yn = jnp.concatenate(parts, axis=1)
    o_a = (yn * gng_ref[...] + gnb_ref[...] + bonus_ref[...].astype(F32)) * za_ref[...].astype(F32)
    merged = (g_ref[:, 0:D_MODEL].astype(F32) * _bdot(o_a, pa_ref[...])
              + g_ref[:, D_MODEL:2 * D_MODEL].astype(F32) * _bdot(ob_ref[...], pb_ref[...])
              + g_ref[:, 2 * D_MODEL:3 * D_MODEL].astype(F32) * _bdot(oc_ref[...], pc_ref[...]))
    t = DEEPNORM_ALPHA * x_ref[...] + _bdot(merged, wo_ref[...])
    mean = jnp.mean(t, axis=1, keepdims=True)
    d = t - mean
    var = jnp.mean(d * d, axis=1, keepdims=True)
    out = d * lax.rsqrt(var + LN_EPS) * lng_ref[...] + lnb_ref[...]
    o_ref[...] = out
    ob16_ref[...] = out.astype(BF16)


def _merge(x, yf, yb, bonus, za, ob, oc, gates, gn_g, gn_b, pa, pb, pc, wo, ln_g, ln_b):
    m = x.shape[0]
    tm = min(256, m)
    rows = lambda n: pl.BlockSpec((tm, n), lambda i: (i, 0))
    full = lambda a: pl.BlockSpec(a.shape, lambda i: (0, 0))
    return pl.pallas_call(
        _merge_body,
        grid=(m // tm,),
        in_specs=[rows(D_MODEL), rows(D_A), rows(D_A), rows(D_A), rows(D_A), rows(D_B), rows(D_C_OUT),
                  rows(3 * D_MODEL), full(gn_g), full(gn_b), full(pa), full(pb), full(pc), full(wo),
                  full(ln_g), full(ln_b)],
        out_specs=[rows(D_MODEL), rows(D_MODEL)],
        out_shape=[jax.ShapeDtypeStruct((m, D_MODEL), F32), jax.ShapeDtypeStruct((m, D_MODEL), BF16)],
        compiler_params=_cparams(("parallel",)),
        name="merge_norm",
    )(x, yf, yb, bonus, za, ob, oc, gates, gn_g, gn_b, pa, pb, pc, wo, ln_g, ln_b)


def _rope_tables(seq):
    inv = jnp.power(ROPE_THETA, -jnp.arange(0, HEAD_DIM, 2, dtype=F32) / HEAD_DIM)
    ang = jnp.arange(seq, dtype=F32)[:, None] * inv[None, :]
    ang = jnp.concatenate([ang, ang, ang, ang], axis=-1)
    return jnp.cos(ang), jnp.sin(ang)


def _pad_lora(w_up):
    z = jnp.zeros_like(w_up[0])
    return jnp.stack([jnp.concatenate([w_up[0], z], axis=0), jnp.concatenate([z, w_up[1]], axis=0)])


def _block_diag(w_g):
    g, c, _ = w_g.shape
    eye = jnp.eye(g, dtype=w_g.dtype)
    return (eye[:, None, :, None] * w_g[:, :, None, :]).reshape(g * c, g * c)


def kernel(x, w_in, b_in, rwkv_mu, rwkv_w0, rwkv_w_up, rwkv_a0, rwkv_a_up, rwkv_k_k, rwkv_k_a, rwkv_r_k, rwkv_gn_g, rwkv_gn_b, pool_w, pool_b, pool_scale, proj_a, proj_b, proj_c, w_out, ln_g, ln_b):
    batch, seq, _ = x.shape
    m = batch * seq
    cos, sin = _rope_tables(seq)
    inv_count = _pool_inv_count(seq)
    row = lambda t: t.reshape(1, -1)
    bounds = [0]
    for width in (N_SHIFT, D_A, D_B, D_B, 3 * D_C, D_C_OUT, 3 * D_MODEL):
        bounds.append(bounds[-1] + width)
    acts = (None, "silu", None, "silu", None, "silu", "sigmoid")
    dtypes = (F32, BF16, BF16, BF16, F32, BF16, BF16)

    xf = x.reshape(m, D_MODEL)
    xb = xf.astype(BF16)
    for l in range(DEPTH):
        w_l = w_in[l].astype(BF16)
        b_l = b_in[l]
        sec = [_linear(xb, w_l[:, lo:hi], row(b_l[lo:hi]), act, dt)
               for lo, hi, act, dt in zip(bounds[:-1], bounds[1:], acts, dtypes)]
        u, za, p, zb, qkv, zc, gates = sec

        r, k, v, kkn, lw, a, bonus = _rwkv_prep(
            u, rwkv_mu[l], rwkv_w0[l], rwkv_a0[l],
            _pad_lora(rwkv_w_up[l]).astype(BF16), _pad_lora(rwkv_a_up[l]).astype(BF16),
            row(rwkv_k_k[l]), row(rwkv_r_k[l]), seq)
        yf, yb = _rwkv_scan(r, k, v, kkn, lw, a, row(rwkv_k_a[l]), batch, seq)
        ob = _pool(p, zb, _block_diag(pool_w[l]).astype(BF16), inv_count, row(pool_b[l]), row(pool_scale[l]),
                   batch, seq)
        oc = _attention(qkv, zc, cos, sin, batch, seq)
        xf, xb = _merge(xf, yf, yb, bonus, za, ob, oc, gates, row(rwkv_gn_g[l]), row(rwkv_gn_b[l]),
                        proj_a[l].astype(BF16), proj_b[l].astype(BF16), proj_c[l].astype(BF16),
                        w_out[l].astype(BF16), row(ln_g[l]), row(ln_b[l]))
    return xf.reshape(batch, seq, D_MODEL)
```

```python
import functools
import math

import jax
import jax.numpy as jnp
from jax import lax
from jax.experimental import pallas as pl
from jax.experimental.pallas import tpu as pltpu

F32 = jnp.float32
BF16 = jnp.bfloat16

D_MODEL = 1024
DEPTH = 4
HEAD_DIM = 64
LANES = 128
D_A = 1024
LORA = 64
N_SHIFT = 3 * D_A + 4 * LORA
RWKV_GN_EPS = 64e-5
LOG_DECAY_SCALE = -math.exp(-0.5)
POOL_HALF = (1, 2, 4, 8)
POOL_CH = 192
D_B = 768
D_C = 768
D_C_OUT = 256
ATTN_GROUPS = ((64, 1), (256, 4), (1024, 16))
ROPE_THETA = 10000.0
DEEPNORM_ALPHA = (2 * DEPTH) ** 0.25
LN_EPS = 1e-5
CHUNK = 64
Q_TILE = 128
NEG_BIG = -1e30
VMEM_LIMIT = 52 * 1024 * 1024


def _cparams(sem):
    return pltpu.CompilerParams(dimension_semantics=sem, vmem_limit_bytes=VMEM_LIMIT)


def _bdot(a, b):
    return jnp.dot(a.astype(BF16), b.astype(BF16), preferred_element_type=F32)


def _bdot_nt(a, b):
    return lax.dot_general(a.astype(BF16), b.astype(BF16), (((1,), (1,)), ((), ())),
                           preferred_element_type=F32)


def _split_dot(a, b_exact, terms):
    acc = None
    rem = a
    for _ in range(terms):
        piece = rem.astype(BF16)
        part = jnp.dot(piece, b_exact, preferred_element_type=F32)
        acc = part if acc is None else acc + part
        rem = rem - piece.astype(F32)
    return acc


def _head_ones(n):
    r = lax.broadcasted_iota(jnp.int32, (n, n), 0) // HEAD_DIM
    c = lax.broadcasted_iota(jnp.int32, (n, n), 1) // HEAD_DIM
    return (r == c).astype(BF16)


def _sigmoid(x):
    return 1.0 / (1.0 + jnp.exp(-x))


def _linear_body(x_ref, w_ref, b_ref, o_ref, *, act):
    acc = jnp.dot(x_ref[...], w_ref[...], preferred_element_type=F32) + b_ref[...]
    if act == "silu":
        acc = acc * _sigmoid(acc)
    elif act == "sigmoid":
        acc = _sigmoid(acc)
    o_ref[...] = acc.astype(o_ref.dtype)


def _pick_tile(n, cap):
    best = LANES
    for t in range(LANES, cap + 1, LANES):
        if n % t == 0:
            best = t
    return best


def _linear(x, w, b, act, out_dtype):
    m, k = x.shape
    n = w.shape[1]
    tm = min(1024, m)
    tn = _pick_tile(n, 1664)
    return pl.pallas_call(
        functools.partial(_linear_body, act=act),
        grid=(n // tn, m // tm),
        in_specs=[pl.BlockSpec((tm, k), lambda j, i: (i, 0)),
                  pl.BlockSpec((k, tn), lambda j, i: (0, j)),
                  pl.BlockSpec((1, tn), lambda j, i: (0, j))],
        out_specs=pl.BlockSpec((tm, tn), lambda j, i: (i, j)),
        out_shape=jax.ShapeDtypeStruct((m, n), out_dtype),
        compiler_params=_cparams(("parallel", "parallel")),
        name="linear_" + (act or "id"),
    )(x, w, b)


def _prep_body(u_ref, up_ref, un_ref, mu_ref, w0_ref, a0_ref, wup_ref, aup_ref, kk_ref, rk_ref,
               r_o, k_o, v_o, kkn_o, lw_o, a_o, bonus_o, *, tm, seq):
    i = pl.program_id(0)
    u = u_ref[...]
    row = lax.broadcasted_iota(jnp.int32, (8, 1), 0)
    t0 = (i * tm) % seq
    prev_edge = jnp.where(t0 == 0, 0.0, up_ref[7:8, :])
    next_edge = jnp.where(t0 + tm == seq, 0.0, un_ref[0:1, :])
    prev = pltpu.roll(u, 1, 0)
    prev = jnp.concatenate([jnp.where(row == 0, prev_edge, prev[0:8]), prev[8:]], axis=0)
    nxt = pltpu.roll(u, tm - 1, 0)
    nxt = jnp.concatenate([nxt[:tm - 8], jnp.where(row == 7, next_edge, nxt[tm - 8:])], axis=0)
    mu_prev = mu_ref[0:1, :]
    mu_next = mu_ref[1:2, :]
    u = (1.0 - mu_prev - mu_next) * u + mu_prev * prev + mu_next * nxt

    r = u[:, 0:D_A]
    k = u[:, D_A:2 * D_A]
    v = u[:, 2 * D_A:3 * D_A]
    wd = jnp.tanh(u[:, 3 * D_A:3 * D_A + 2 * LORA])
    ad = u[:, 3 * D_A + 2 * LORA:N_SHIFT]
    r_o[...] = r.astype(r_o.dtype)
    k_o[...] = k.astype(k_o.dtype)
    v_o[...] = v.astype(v_o.dtype)
    for z in range(2):
        warg = w0_ref[z:z + 1, :] + _bdot(wd, wup_ref[z])
        lw_o[z] = LOG_DECAY_SCALE * _sigmoid(warg)
        a_o[z] = _sigmoid(a0_ref[z:z + 1, :] + _bdot(ad, aup_ref[z])).astype(a_o.dtype)

    ones = _head_ones(LANES)
    for cb in range(D_A // LANES):
        sl = slice(cb * LANES, (cb + 1) * LANES)
        kk = k[:, sl] * kk_ref[:, sl]
        ss = _bdot(kk * kk, ones)
        kkn_o[:, sl] = (kk * lax.rsqrt(ss + 1e-12)).astype(kkn_o.dtype)
        rk = _bdot(r[:, sl] * k[:, sl] * rk_ref[:, sl], ones)
        bonus_o[:, sl] = (rk * v[:, sl]).astype(bonus_o.dtype)


def _rwkv_prep(u, mu, w0, a0, wup, aup, k_k, r_k, seq):
    m = u.shape[0]
    tm = min(256, seq)
    nb8 = m // 8
    full = lambda shape: pl.BlockSpec(shape, lambda i: (0,) * len(shape))
    row_spec = pl.BlockSpec((tm, D_A), lambda i: (i, 0))
    dir_spec = pl.BlockSpec((2, tm, D_A), lambda i: (0, i, 0))
    act = jax.ShapeDtypeStruct((m, D_A), BF16)
    act2 = jax.ShapeDtypeStruct((2, m, D_A), BF16)
    logw = jax.ShapeDtypeStruct((2, m, D_A), F32)
    return pl.pallas_call(
        functools.partial(_prep_body, tm=tm, seq=seq),
        grid=(m // tm,),
        in_specs=[pl.BlockSpec((tm, N_SHIFT), lambda i: (i, 0)),
                  pl.BlockSpec((8, N_SHIFT), lambda i: (jnp.maximum(i * (tm // 8) - 1, 0), 0)),
                  pl.BlockSpec((8, N_SHIFT), lambda i: (jnp.minimum((i + 1) * (tm // 8), nb8 - 1), 0)),
                  full((2, N_SHIFT)), full((2, D_A)), full((2, D_A)),
                  full((2, LANES, D_A)), full((2, LANES, D_A)), full((1, D_A)), full((1, D_A))],
        out_specs=[row_spec, row_spec, row_spec, row_spec, dir_spec, dir_spec, row_spec],
        out_shape=[act, act, act, act, logw, act2, act],
        compiler_params=_cparams(("parallel",)),
        name="rwkv_prep",
    )(u, u, u, mu, w0, a0, wup, aup, k_k, r_k)


SCAN_CHUNKS = 8
SCAN_GROUPS = 1


def _bmm(a, b):
    return lax.dot_general(a.astype(BF16), b.astype(BF16), (((2,), (1,)), ((0,), (0,))),
                           preferred_element_type=F32)


def _bmm_nt(a, b):
    return lax.dot_general(a.astype(BF16), b.astype(BF16), (((2,), (2,)), ((0,), (0,))),
                           preferred_element_type=F32)


def _stack_heads(t):
    lane = lax.broadcasted_iota(jnp.int32, t.shape, 2)
    first = lane < HEAD_DIM
    return jnp.concatenate([jnp.where(first, t, 0.0), jnp.where(first, 0.0, t)], axis=1)


def _chunk_terms(r, k, v, kk, lw, a, k_a, nfwd, interleave):
    n, c, _ = r.shape
    n2 = 2 * c
    by_dir = lambda f, b: jnp.concatenate([f, b], axis=0)
    ri = lax.broadcasted_iota(jnp.int32, (c, 3 * c), 0)
    ci = lax.broadcasted_iota(jnp.int32, (c, 3 * c), 1) % c
    pieces = []
    rem = lw
    for _ in range(3):
        pieces.append(rem.astype(BF16))
        rem = rem - pieces[-1].astype(F32)
    pieces = jnp.concatenate(pieces, axis=1)
    interleave()

    def cumsum(tri, lo, hi):
        wide = jnp.dot(tri.astype(BF16), jnp.concatenate([pieces[i] for i in range(lo, hi)], axis=1),
                       preferred_element_type=F32)
        return jnp.stack([wide[:, (i - lo) * LANES:(i - lo + 1) * LANES] for i in range(lo, hi)])

    cum = by_dir(cumsum(ci <= ri, 0, nfwd), cumsum(ci >= ri, nfwd, n))
    total = by_dir(cum[:nfwd, c - 1:c, :], cum[nfwd:, 0:1, :])
    kd = k * (1.0 + (a - 1.0) * k_a)
    bb = kk * a
    interleave()
    e_neg = jnp.exp(-cum)
    bt = bb * e_neg
    kt = kd * e_neg
    bd = jnp.concatenate([bt, bt], axis=1)
    kdd = jnp.concatenate([kt, kt], axis=1)
    interleave()
    xn = _stack_heads(-kk * jnp.exp(cum - lw))
    xr = _stack_heads(r * jnp.exp(cum))

    rr = lax.broadcasted_iota(jnp.int32, (n2, n2), 0)
    cc = lax.broadcasted_iota(jnp.int32, (n2, n2), 1)
    same = (rr // c) == (cc // c)
    diag = (rr == cc)[None]
    before = (same & (cc % c < rr % c))[None]
    after = (same & (cc % c > rr % c))[None]

    def causal(x, inclusive):
        fm = (before | diag) if inclusive else before
        bm = (after | diag) if inclusive else after
        return by_dir(jnp.where(fm, x[:nfwd], 0.0), jnp.where(bm, x[nfwd:], 0.0))

    aa = _bmm_nt(jnp.concatenate([xn, xr], axis=1), jnp.concatenate([bd, kdd], axis=1))
    a_ab = causal(aa[:, :n2, :n2], False)
    a_ak = causal(aa[:, :n2, n2:], False)
    a_rb = causal(aa[:, n2:, :n2], True)
    a_rk = causal(aa[:, n2:, n2:], True)
    interleave()

    tinv = jnp.where(diag, 1.0, a_ab)
    apow = _bmm(a_ab, a_ab)
    e_rem = jnp.exp(total - cum)
    vd = _stack_heads(v)
    b2d = _stack_heads(bb * e_rem)
    k2d = _stack_heads(kd * e_rem)
    for step in range(max(1, (c - 1).bit_length()) - 2):
        both = _bmm(apow, jnp.concatenate([apow, tinv], axis=2))
        apow = both[:, :, :n2]
        tinv = tinv + both[:, :, n2:]
        interleave()
    tinv = tinv + _bmm(apow, tinv)

    rhs_q = _bmm(a_ak, vd)
    pq = _bmm(tinv, jnp.concatenate([xn, rhs_q], axis=2))
    lhs = jnp.concatenate([jnp.concatenate([a_rb, a_rk], axis=2),
                           jnp.concatenate([jnp.swapaxes(b2d, 1, 2), jnp.swapaxes(k2d, 1, 2)], axis=2)], axis=1)
    rhs = jnp.concatenate([pq, jnp.concatenate([jnp.zeros_like(vd), vd], axis=2)], axis=1)
    out = _bmm(lhs, rhs)
    r2 = xr + out[:, :n2, :LANES]
    y0 = out[:, :n2, LANES:]
    gm = out[:, n2:, :LANES]
    h0 = out[:, n2:, LANES:]
    gcol = jnp.sum(jnp.where(diag, jnp.exp(total), 0.0), axis=2, keepdims=True)
    return jnp.concatenate([r2, gm], axis=1), y0, gcol, h0


def _scan_body(rf, kf, vf, kkf, lwf, af, rb, kb, vb, kkb, lwb, ab, ka_ref, yf_o, yb_o,
               st_ref, rg_ref, y0_ref, gc_ref, h0_ref, *, steps_per_seq):
    g = pl.program_id(0)

    @pl.when(g == 0)
    def _():
        st_ref[...] = jnp.zeros_like(st_ref)
        rg_ref[...] = jnp.zeros_like(rg_ref)
        y0_ref[...] = jnp.zeros_like(y0_ref)
        gc_ref[...] = jnp.zeros_like(gc_ref)
        h0_ref[...] = jnp.zeros_like(h0_ref)

    nch = SCAN_CHUNKS
    rows = nch * CHUNK
    n2 = 2 * CHUNK
    rd = (g + 1) % 2
    wr = g % 2
    fresh = (g % steps_per_seq) == (1 % steps_per_seq)
    state = [jnp.where(fresh, 0.0, st_ref[0]), jnp.where(fresh, 0.0, st_ref[1])]
    pending = [(grp, j) for grp in range(SCAN_GROUPS) for j in range(nch)]

    def chain_step(drain=False):
        if not pending:
            return
        if drain:
            while len(pending) > 1:
                chain_step()
        grp, j = pending.pop(0)
        f0 = grp * rows
        b0 = (SCAN_GROUPS - 1 - grp) * rows
        for d, jc, row0 in ((0, j, f0 + j * CHUNK), (1, 2 * nch - 1 - j, b0 + (nch - 1 - j) * CHUNK)):
            out = jnp.dot(rg_ref[rd, grp, jc], state[d].astype(BF16), preferred_element_type=F32)
            y = out[:n2] + y0_ref[rd, grp, jc]
            state[d] = state[d] * gc_ref[rd, grp, jc] + out[n2:] + h0_ref[rd, grp, jc]
            o_ref = yf_o if d == 0 else yb_o
            o_ref[row0:row0 + CHUNK, :] = (y[:CHUNK] + y[CHUNK:]).astype(o_ref.dtype)

    for grp in range(SCAN_GROUPS):
        f0 = grp * rows
        b0 = (SCAN_GROUPS - 1 - grp) * rows

        def both(f, b, f0=f0, b0=b0):
            return jnp.concatenate([f[f0:f0 + rows, :].astype(F32).reshape(nch, CHUNK, LANES),
                                    b[b0:b0 + rows, :].astype(F32).reshape(nch, CHUNK, LANES)], axis=0)

        rg, y0, gcol, h0 = _chunk_terms(both(rf, rb), both(kf, kb), both(vf, vb), both(kkf, kkb),
                                        both(lwf, lwb), both(af, ab), ka_ref[...], nch, chain_step)
        rg_ref[wr, grp] = rg.astype(BF16)
        y0_ref[wr, grp] = y0
        gc_ref[wr, grp] = jnp.broadcast_to(gcol, h0.shape)
        h0_ref[wr, grp] = h0
    while pending:
        chain_step()
    st_ref[0] = state[0]
    st_ref[1] = state[1]


def _rwkv_scan(r, k, v, kkn, lw, a, k_a, batch, seq):
    m = r.shape[0]
    rows = SCAN_GROUPS * SCAN_CHUNKS * CHUNK
    ns = seq // rows
    npair = D_A // LANES
    nblocks = batch * npair * ns

    def place(blk_id, rev):
        s = blk_id % ns
        p = (blk_id // ns) % npair
        b = blk_id // (ns * npair)
        return (b * ns + (ns - 1 - s if rev else s), p)

    cur = lambda g: jnp.minimum(g, nblocks - 1)
    prev = lambda g: jnp.maximum(g - 1, 0)
    blk = (rows, LANES)
    fspec = pl.BlockSpec(blk, lambda g: place(cur(g), False))
    bspec = pl.BlockSpec(blk, lambda g: place(cur(g), True))
    fdir = pl.BlockSpec((None,) + blk, lambda g: (0,) + place(cur(g), False))
    bdir = pl.BlockSpec((None,) + blk, lambda g: (1,) + place(cur(g), True))
    out = jax.ShapeDtypeStruct((m, D_A), BF16)
    nchain = 2 * SCAN_CHUNKS
    terms = lambda rws, dt: pltpu.VMEM((2, SCAN_GROUPS, nchain, rws, LANES), dt)
    return pl.pallas_call(
        functools.partial(_scan_body, steps_per_seq=ns),
        grid=(nblocks + 1,),
        in_specs=[fspec, fspec, fspec, fspec, fdir, fdir, bspec, bspec, bspec, bspec, bdir, bdir,
                  pl.BlockSpec((1, LANES), lambda g: (0, place(cur(g), False)[1]))],
        out_specs=[pl.BlockSpec(blk, lambda g: place(prev(g), False)),
                   pl.BlockSpec(blk, lambda g: place(prev(g), True))],
        out_shape=[out, out],
        scratch_shapes=[pltpu.VMEM((2, LANES, LANES), F32), terms(2 * CHUNK + LANES, BF16),
                        terms(2 * CHUNK, F32), terms(LANES, F32), terms(LANES, F32)],
        compiler_params=_cparams(("arbitrary",)),
        name="rwkv_scan",
    )(r, k, v, kkn, lw, a, r, k, v, kkn, lw, a, k_a)


POOL_PAD = 8
POOL_COLS = 256
POOL_ROWS = 256


def _pool_body(p_ref, w_ref, inv_ref, z_ref, b_ref, s_ref, o_ref, pad_ref, acc_ref, *, seq):
    cb = pl.program_id(1)
    ncb = pl.num_programs(1)
    pad_ref[0:POOL_PAD, :] = jnp.zeros((POOL_PAD, POOL_COLS), F32)
    pad_ref[POOL_PAD + seq:, :] = jnp.zeros((POOL_PAD, POOL_COLS), F32)
    pad_ref[POOL_PAD:POOL_PAD + seq, :] = p_ref[...].astype(F32)

    @pl.when(cb == 0)
    def _():
        acc_ref[...] = jnp.zeros_like(acc_ref)

    lane = lax.broadcasted_iota(jnp.int32, (1, POOL_COLS), 1) + cb * POOL_COLS
    group = lane // POOL_CH
    w = w_ref[...]

    def rows(i, carry):
        base = pl.multiple_of(i * POOL_ROWS, POOL_ROWS)
        nwin = POOL_ROWS + 2 * POOL_PAD
        win = pad_ref[pl.ds(base, nwin), :]
        pair = win + pltpu.roll(win, nwin - 1, 0)
        at = lambda a, d: a[POOL_PAD + d:POOL_PAD + d + POOL_ROWS]
        x = at(win, 0)
        s1 = x + at(win, -1) + at(win, 1)
        s2 = s1 + at(win, -2) + at(win, 2)
        s4 = s2 + at(pair, -4) + at(pair, 3)
        s8 = s4 + at(pair, -8) + at(pair, -6) + at(pair, 5) + at(pair, 7)
        total = jnp.where(group == 0, s1, jnp.where(group == 1, s2, jnp.where(group == 2, s4, s8)))
        mean = total * inv_ref[pl.ds(base, POOL_ROWS), :]
        acc_ref[pl.ds(base, POOL_ROWS), :] += _bdot(mean - x, w)
        return carry

    lax.fori_loop(0, seq // POOL_ROWS, rows, 0)

    @pl.when(cb == ncb - 1)
    def _():
        o_ref[...] = ((acc_ref[...] + b_ref[...]) * s_ref[...] * z_ref[...].astype(F32)).astype(o_ref.dtype)


def _pool_inv_count(seq):
    half = jnp.repeat(jnp.array(POOL_HALF, jnp.int32), POOL_CH)[None, :]
    t = jnp.arange(seq, dtype=jnp.int32)[:, None]
    count = jnp.minimum(t + half, seq - 1) - jnp.maximum(t - half, 0) + 1
    return 1.0 / count.astype(F32)


def _pool(p, zb, w_bd, inv_count, b, scale, batch, seq):
    m = p.shape[0]
    ncb = D_B // POOL_COLS
    full = pl.BlockSpec((seq, D_B), lambda bi, cb: (bi, 0))
    vec = pl.BlockSpec((1, D_B), lambda bi, cb: (0, 0))
    return pl.pallas_call(
        functools.partial(_pool_body, seq=seq),
        grid=(batch, ncb),
        in_specs=[pl.BlockSpec((seq, POOL_COLS), lambda bi, cb: (bi, cb)),
                  pl.BlockSpec((POOL_COLS, D_B), lambda bi, cb: (cb, 0)),
                  pl.BlockSpec((seq, POOL_COLS), lambda bi, cb: (0, cb)),
                  full, vec, vec],
        out_specs=full,
        out_shape=jax.ShapeDtypeStruct((m, D_B), BF16),
        scratch_shapes=[pltpu.VMEM((seq + 2 * POOL_PAD, POOL_COLS), F32), pltpu.VMEM((seq, D_B), F32)],
        compiler_params=_cparams(("parallel", "arbitrary")),
        name="pool_mixer",
    )(p, w_bd, inv_count, zb, b, scale)


def _rope_matrix():
    half = HEAD_DIM // 2
    src = lax.broadcasted_iota(jnp.int32, (LANES, LANES), 0)
    dst = lax.broadcasted_iota(jnp.int32, (LANES, LANES), 1)
    low = (dst % HEAD_DIM) < half
    return jnp.where(low & (src == dst + half), -1.0, jnp.where(~low & (src == dst - half), 1.0, 0.0)).astype(BF16)


def _rope(t, cos, sin, rot_matrix):
    return t * cos + _split_dot(t, rot_matrix, 2) * sin


ATTN_SIDE = 64
ATTN_TILES = 4
COMBINE_ROWS = 256


def _attn_body(q0, q1, q2, k0, k1, k2, v0, v1, v2, cos_ref, sin_ref, zc_ref, o_ref,
               qp_ref, kp_ref, vp_ref, op_ref, lp_ref, on_ref, ln_ref, *, seq):
    q_refs = (q0, q1, q2)
    k_refs = (k0, k1, k2)
    v_refs = (v0, v1, v2)
    scale = HEAD_DIM ** -0.5
    rot_matrix = _rope_matrix()
    for g, (half_span, dil) in enumerate(ATTN_GROUPS):
        cls = seq // dil
        for r in range(dil):
            rows = pl.ds(r, cls, stride=dil) if dil > 1 else pl.ds(0, seq)
            dst = pl.ds(r * cls, cls)
            cos = cos_ref[rows, :]
            sin = sin_ref[rows, :]
            qp_ref[g, dst, :] = (_rope(q_refs[g][rows, :], cos, sin, rot_matrix) * scale).astype(BF16)
            kp_ref[g, dst, :] = _rope(k_refs[g][rows, :], cos, sin, rot_matrix).astype(BF16)
            vp_ref[g, dst, :] = v_refs[g][rows, :].astype(BF16)

    nt = ATTN_TILES
    first = lax.broadcasted_iota(jnp.int32, (nt, Q_TILE, LANES), 2) < HEAD_DIM
    for g, (half_span, dil) in enumerate(ATTN_GROUPS):
        cls = seq // dil
        klen = min(cls, Q_TILE + 2 * ATTN_SIDE)
        rel = (lax.broadcasted_iota(jnp.int32, (Q_TILE, klen), 1)
               - lax.broadcasted_iota(jnp.int32, (Q_TILE, klen), 0))

        def tiles(it, carry, g=g, cls=cls, klen=klen, rel=rel):
            row0 = pl.multiple_of(it * (nt * Q_TILE), nt * Q_TILE)
            keys, vals, valid = [], [], []
            for j in range(nt):
                qrow = row0 + j * Q_TILE
                base = (qrow // cls) * cls
                krow = pl.multiple_of(base + jnp.clip(qrow - base - ATTN_SIDE, 0, cls - klen), ATTN_SIDE)
                keys.append(kp_ref[g, pl.ds(krow, klen), :])
                vals.append(vp_ref[g, pl.ds(krow, klen), :])
                valid.append(jnp.abs(rel + (krow - qrow)) <= ATTN_SIDE)
            q = qp_ref[g, pl.ds(row0, nt * Q_TILE), :].reshape(nt, Q_TILE, LANES)
            zero = jnp.zeros_like(q)
            qm = jnp.concatenate([jnp.where(first, q, zero), jnp.where(first, zero, q)], axis=0)
            keys = jnp.stack(keys + keys)
            vals = jnp.stack(vals + vals)
            valid = jnp.stack(valid + valid)
            s = jnp.where(valid, _bmm_nt(qm, keys), NEG_BIG)
            mx = jnp.max(s, axis=2, keepdims=True)
            e = jnp.exp(s - mx)
            den = jnp.sum(e, axis=2, keepdims=True)
            o = _bmm(e, vals) / den
            lse = jnp.broadcast_to(mx + jnp.log(den), o.shape)
            rows = pl.ds(row0, nt * Q_TILE)
            op_ref[g, rows, :] = jnp.where(first, o[:nt], o[nt:]).reshape(nt * Q_TILE, LANES)
            lp_ref[g, rows, :] = jnp.where(first, lse[:nt], lse[nt:]).reshape(nt * Q_TILE, LANES)
            return carry

        lax.fori_loop(0, seq // (nt * Q_TILE), tiles, 0)
        if dil > 1:
            for r in range(dil):
                src = pl.ds(r * cls, cls)
                on_ref[g - 1, pl.ds(r, cls, stride=dil), :] = op_ref[g, src, :]
                ln_ref[g - 1, pl.ds(r, cls, stride=dil), :] = lp_ref[g, src, :]

    def combine(i, carry):
        rows = pl.ds(pl.multiple_of(i * COMBINE_ROWS, COMBINE_ROWS), COMBINE_ROWS)
        ls = [lp_ref[0, rows, :], ln_ref[0, rows, :], ln_ref[1, rows, :]]
        os_ = [op_ref[0, rows, :], on_ref[0, rows, :], on_ref[1, rows, :]]
        top = jnp.maximum(jnp.maximum(ls[0], ls[1]), ls[2])
        ws = [jnp.exp(l - top) for l in ls]
        tot = ws[0] + ws[1] + ws[2]
        out = (ws[0] * os_[0] + ws[1] * os_[1] + ws[2] * os_[2]) / tot * zc_ref[rows, :].astype(F32)
        o_ref[rows, :] = out.astype(o_ref.dtype)
        return carry

    lax.fori_loop(0, seq // COMBINE_ROWS, combine, 0)


def _attention(qkv, zc, cos, sin, batch, seq):
    m = qkv.shape[0]
    ncol = D_C // LANES
    per_group = D_C_OUT // LANES
    ngroup = len(ATTN_GROUPS)
    spec = lambda sec, g: pl.BlockSpec((seq, LANES), lambda b, jp: (b, sec * ncol + g * per_group + jp))
    tab = pl.BlockSpec((seq, LANES), lambda b, jp: (0, 0))
    oz = pl.BlockSpec((seq, LANES), lambda b, jp: (b, jp))
    perm_bf = pltpu.VMEM((ngroup, seq, LANES), BF16)
    perm_f32 = pltpu.VMEM((ngroup, seq, LANES), F32)
    nat_f32 = pltpu.VMEM((ngroup - 1, seq, LANES), F32)
    return pl.pallas_call(
        functools.partial(_attn_body, seq=seq),
        grid=(batch, per_group),
        in_specs=[spec(0, 0), spec(0, 1), spec(0, 2), spec(1, 0), spec(1, 1), spec(1, 2),
                  spec(2, 0), spec(2, 1), spec(2, 2), tab, tab, oz],
        out_specs=oz,
        out_shape=jax.ShapeDtypeStruct((m, D_C_OUT), BF16),
        scratch_shapes=[perm_bf, perm_bf, perm_bf, perm_f32, perm_f32, nat_f32, nat_f32],
        compiler_params=_cparams(("parallel", "parallel")),
        name="dilated_attention",
    )(qkv, qkv, qkv, qkv, qkv, qkv, qkv, qkv, qkv, cos, sin, zc)


def _merge_body(x_ref, yf_ref, yb_ref, bonus_ref, za_ref, ob_ref, oc_ref, g_ref,
                gng_ref, gnb_ref, pa_ref, pb_ref, pc_ref, wo_ref, lng_ref, lnb_ref, o_ref, ob16_ref):
    ones = _head_ones(LANES)
    y = yf_ref[...].astype(F32) + yb_ref[...].astype(F32)
    parts = []
    for cb in range(D_A // LANES):
        sl = slice(cb * LANES, (cb + 1) * LANES)
        yc = y[:, sl]
        mu = _bdot(yc, ones) * (1.0 / HEAD_DIM)
        d = yc - mu
        var = _bdot(d * d, ones) * (1.0 / HEAD_DIM)
        parts.append(d * lax.rsqrt(var + RWKV_GN_EPS))
    yn = jnp.concatenate(parts, axis=1)
    o_a = (yn * gng_ref[...] + gnb_ref[...] + bonus_ref[...].astype(F32)) * za_ref[...].astype(F32)
    merged = (g_ref[:, 0:D_MODEL].astype(F32) * _bdot(o_a, pa_ref[...])
              + g_ref[:, D_MODEL:2 * D_MODEL].astype(F32) * _bdot(ob_ref[...], pb_ref[...])
              + g_ref[:, 2 * D_MODEL:3 * D_MODEL].astype(F32) * _bdot(oc_ref[...], pc_ref[...]))
    t = DEEPNORM_ALPHA * x_ref[...] + _bdot(merged, wo_ref[...])
    mean = jnp.mean(t, axis=1, keepdims=True)
    d = t - mean
    var = jnp.mean(d * d, axis=1, keepdims=True)
    out = d * lax.rsqrt(var + LN_EPS) * lng_ref[...] + lnb_ref[...]
    o_ref[...] = out
    ob16_ref[...] = out.astype(BF16)


def _merge(x, yf, yb, bonus, za, ob, oc, gates, gn_g, gn_b, pa, pb, pc, wo, ln_g, ln_b):
    m = x.shape[0]
    tm = min(256, m)
    rows = lambda n: pl.BlockSpec((tm, n), lambda i: (i, 0))
    full = lambda a: pl.BlockSpec(a.shape, lambda i: (0, 0))
    return pl.pallas_call(
        _merge_body,
        grid=(m // tm,),
        in_specs=[rows(D_MODEL), rows(D_A), rows(D_A), rows(D_A), rows(D_A), rows(D_B), rows(D_C_OUT),
                  rows(3 * D_MODEL), full(gn_g), full(gn_b), full(pa), full(pb), full(pc), full(wo),
                  full(ln_g), full(ln_b)],
        out_specs=[rows(D_MODEL), rows(D_MODEL)],
        out_shape=[jax.ShapeDtypeStruct((m, D_MODEL), F32), jax.ShapeDtypeStruct((m, D_MODEL), BF16)],
        compiler_params=_cparams(("parallel",)),
        name="merge_norm",
    )(x, yf, yb, bonus, za, ob, oc, gates, gn_g, gn_b, pa, pb, pc, wo, ln_g, ln_b)


def _rope_tables(seq):
    inv = jnp.power(ROPE_THETA, -jnp.arange(0, HEAD_DIM, 2, dtype=F32) / HEAD_DIM)
    ang = jnp.arange(seq, dtype=F32)[:, None] * inv[None, :]
    ang = jnp.concatenate([ang, ang, ang, ang], axis=-1)
    return jnp.cos(ang), jnp.sin(ang)


def _pad_lora(w_up):
    z = jnp.zeros_like(w_up[0])
    return jnp.stack([jnp.concatenate([w_up[0], z], axis=0), jnp.concatenate([z, w_up[1]], axis=0)])


def _block_diag(w_g):
    g, c, _ = w_g.shape
    eye = jnp.eye(g, dtype=w_g.dtype)
    return (eye[:, None, :, None] * w_g[:, :, None, :]).reshape(g * c, g * c)


def kernel(x, w_in, b_in, rwkv_mu, rwkv_w0, rwkv_w_up, rwkv_a0, rwkv_a_up, rwkv_k_k, rwkv_k_a, rwkv_r_k, rwkv_gn_g, rwkv_gn_b, pool_w, pool_b, pool_scale, proj_a, proj_b, proj_c, w_out, ln_g, ln_b):
    batch, seq, _ = x.shape
    m = batch * seq
    cos, sin = _rope_tables(seq)
    inv_count = _pool_inv_count(seq)
    row = lambda t: t.reshape(1, -1)
    bounds = [0]
    for width in (N_SHIFT, D_A, D_B, D_B, 3 * D_C, D_C_OUT, 3 * D_MODEL):
        bounds.append(bounds[-1] + width)
    acts = (None, "silu", None, "silu", None, "silu", "sigmoid")
    dtypes = (F32, BF16, BF16, BF16, F32, BF16, BF16)

    xf = x.reshape(m, D_MODEL)
    xb = xf.astype(BF16)
    for l in range(DEPTH):
        w_l = w_in[l].astype(BF16)
        b_l = b_in[l]
        sec = [_linear(xb, w_l[:, lo:hi], row(b_l[lo:hi]), act, dt)
               for lo, hi, act, dt in zip(bounds[:-1], bounds[1:], acts, dtypes)]
        u, za, p, zb, qkv, zc, gates = sec

        r, k, v, kkn, lw, a, bonus = _rwkv_prep(
            u, rwkv_mu[l], rwkv_w0[l], rwkv_a0[l],
            _pad_lora(rwkv_w_up[l]).astype(BF16), _pad_lora(rwkv_a_up[l]).astype(BF16),
            row(rwkv_k_k[l]), row(rwkv_r_k[l]), seq)
        yf, yb = _rwkv_scan(r, k, v, kkn, lw, a, row(rwkv_k_a[l]), batch, seq)
        ob = _pool(p, zb, _block_diag(pool_w[l]).astype(BF16), inv_count, row(pool_b[l]), row(pool_scale[l]),
                   batch, seq)
        oc = _attention(qkv, zc, cos, sin, batch, seq)
        xf, xb = _merge(xf, yf, yb, bonus, za, ob, oc, gates, row(rwkv_gn_g[l]), row(rwkv_gn_b[l]),
                        proj_a[l].astype(BF16), proj_b[l].astype(BF16), proj_c[l].astype(BF16),
                        w_out[l].astype(BF16), row(ln_g[l]), row(ln_b[l]))
    return xf.reshape(batch, seq, D_MODEL)
```

```python
import functools
import math

import jax
import jax.numpy as jnp
from jax import lax
from jax.experimental import pallas as pl
from jax.experimental.pallas import tpu as pltpu

F32 = jnp.float32
BF16 = jnp.bfloat16

D_MODEL = 1024
DEPTH = 4
HEAD_DIM = 64
LANES = 128
D_A = 1024
LORA = 64
N_SHIFT = 3 * D_A + 4 * LORA
RWKV_GN_EPS = 64e-5
LOG_DECAY_SCALE = -math.exp(-0.5)
POOL_HALF = (1, 2, 4, 8)
POOL_CH = 192
D_B = 768
D_C = 768
D_C_OUT = 256
ATTN_GROUPS = ((64, 1), (256, 4), (1024, 16))
ROPE_THETA = 10000.0
DEEPNORM_ALPHA = (2 * DEPTH) ** 0.25
LN_EPS = 1e-5
CHUNK = 64
Q_TILE = 128
NEG_BIG = -1e30
VMEM_LIMIT = 52 * 1024 * 1024


def _cparams(sem):
    return pltpu.CompilerParams(dimension_semantics=sem, vmem_limit_bytes=VMEM_LIMIT)


def _bdot(a, b):
    return jnp.dot(a.astype(BF16), b.astype(BF16), preferred_element_type=F32)


def _bdot_nt(a, b):
    return lax.dot_general(a.astype(BF16), b.astype(BF16), (((1,), (1,)), ((), ())),
                           preferred_element_type=F32)


def _split_dot(a, b_exact, terms):
    acc = None
    rem = a
    for _ in range(terms):
        piece = rem.astype(BF16)
        part = jnp.dot(piece, b_exact, preferred_element_type=F32)
        acc = part if acc is None else acc + part
        rem = rem - piece.astype(F32)
    return acc


def _head_ones(n):
    r = lax.broadcasted_iota(jnp.int32, (n, n), 0) // HEAD_DIM
    c = lax.broadcasted_iota(jnp.int32, (n, n), 1) // HEAD_DIM
    return (r == c).astype(BF16)


def _sigmoid(x):
    return 0.5 + 0.5 * jnp.tanh(0.5 * x)


def _linear_body(x_ref, w_ref, b_ref, o_ref, wb_ref, *, act):
    @pl.when(pl.program_id(1) == 0)
    def _():
        wb_ref[...] = w_ref[...].astype(BF16)

    acc = jnp.dot(x_ref[...], wb_ref[...], preferred_element_type=F32) + b_ref[...]
    if act == "silu":
        acc = acc * _sigmoid(acc)
    elif act == "sigmoid":
        acc = _sigmoid(acc)
    o_ref[...] = acc.astype(o_ref.dtype)


def _pick_tile(n, cap):
    best = LANES
    for t in range(LANES, cap + 1, LANES):
        if n % t == 0:
            best = t
    return best


def _linear(x, w, b, lo, n, act, out_dtype):
    m, k = x.shape
    tm = min(1024, m)
    tn = _pick_tile(n, 1664)
    col = lambda j: pl.multiple_of(lo + j * tn, LANES)
    return pl.pallas_call(
        functools.partial(_linear_body, act=act),
        grid=(n // tn, m // tm),
        in_specs=[pl.BlockSpec((tm, k), lambda j, i: (i, 0)),
                  pl.BlockSpec((pl.Element(k), pl.Element(tn)), lambda j, i: (0, col(j))),
                  pl.BlockSpec((pl.Element(1), pl.Element(tn)), lambda j, i: (0, col(j)))],
        out_specs=pl.BlockSpec((tm, tn), lambda j, i: (i, j)),
        out_shape=jax.ShapeDtypeStruct((m, n), out_dtype),
        scratch_shapes=[pltpu.VMEM((k, tn), BF16)],
        compiler_params=_cparams(("parallel", "arbitrary")),
        name="linear_" + (act or "id"),
    )(x, w, b)


def _prep_body(u_ref, up_ref, un_ref, mu_ref, w0_ref, a0_ref, wup_ref, aup_ref, kk_ref, rk_ref,
               r_o, k_o, v_o, kkn_o, lw_o, a_o, bonus_o, *, tm, seq):
    i = pl.program_id(0)
    u = u_ref[...]
    row = lax.broadcasted_iota(jnp.int32, (8, 1), 0)
    t0 = (i * tm) % seq
    prev_edge = jnp.where(t0 == 0, 0.0, up_ref[7:8, :])
    next_edge = jnp.where(t0 + tm == seq, 0.0, un_ref[0:1, :])
    prev = pltpu.roll(u, 1, 0)
    prev = jnp.concatenate([jnp.where(row == 0, prev_edge, prev[0:8]), prev[8:]], axis=0)
    nxt = pltpu.roll(u, tm - 1, 0)
    nxt = jnp.concatenate([nxt[:tm - 8], jnp.where(row == 7, next_edge, nxt[tm - 8:])], axis=0)
    mu_prev = mu_ref[0:1, :]
    mu_next = mu_ref[1:2, :]
    u = (1.0 - mu_prev - mu_next) * u + mu_prev * prev + mu_next * nxt

    r = u[:, 0:D_A]
    k = u[:, D_A:2 * D_A]
    v = u[:, 2 * D_A:3 * D_A]
    wd = jnp.tanh(u[:, 3 * D_A:3 * D_A + 2 * LORA])
    ad = u[:, 3 * D_A + 2 * LORA:N_SHIFT]
    r_o[...] = r.astype(r_o.dtype)
    k_o[...] = k.astype(k_o.dtype)
    v_o[...] = v.astype(v_o.dtype)
    for z in range(2):
        warg = w0_ref[z:z + 1, :] + _bdot(wd, wup_ref[z])
        lw_o[z] = LOG_DECAY_SCALE * _sigmoid(warg)
        a_o[z] = _sigmoid(a0_ref[z:z + 1, :] + _bdot(ad, aup_ref[z])).astype(a_o.dtype)

    ones = _head_ones(LANES)
    for cb in range(D_A // LANES):
        sl = slice(cb * LANES, (cb + 1) * LANES)
        kk = k[:, sl] * kk_ref[:, sl]
        ss = _bdot(kk * kk, ones)
        kkn_o[:, sl] = (kk * lax.rsqrt(ss + 1e-12)).astype(kkn_o.dtype)
        rk = _bdot(r[:, sl] * k[:, sl] * rk_ref[:, sl], ones)
        bonus_o[:, sl] = (rk * v[:, sl]).astype(bonus_o.dtype)


def _rwkv_prep(u, mu, w0, a0, wup, aup, k_k, r_k, seq):
    m = u.shape[0]
    tm = min(256, seq)
    nb8 = m // 8
    full = lambda shape: pl.BlockSpec(shape, lambda i: (0,) * len(shape))
    row_spec = pl.BlockSpec((tm, D_A), lambda i: (i, 0))
    dir_spec = pl.BlockSpec((2, tm, D_A), lambda i: (0, i, 0))
    act = jax.ShapeDtypeStruct((m, D_A), BF16)
    act2 = jax.ShapeDtypeStruct((2, m, D_A), BF16)
    logw = jax.ShapeDtypeStruct((2, m, D_A), F32)
    return pl.pallas_call(
        functools.partial(_prep_body, tm=tm, seq=seq),
        grid=(m // tm,),
        in_specs=[pl.BlockSpec((tm, N_SHIFT), lambda i: (i, 0)),
                  pl.BlockSpec((8, N_SHIFT), lambda i: (jnp.maximum(i * (tm // 8) - 1, 0), 0)),
                  pl.BlockSpec((8, N_SHIFT), lambda i: (jnp.minimum((i + 1) * (tm // 8), nb8 - 1), 0)),
                  full((2, N_SHIFT)), full((2, D_A)), full((2, D_A)),
                  full((2, LANES, D_A)), full((2, LANES, D_A)), full((1, D_A)), full((1, D_A))],
        out_specs=[row_spec, row_spec, row_spec, row_spec, dir_spec, dir_spec, row_spec],
        out_shape=[act, act, act, act, logw, act2, act],
        compiler_params=_cparams(("parallel",)),
        name="rwkv_prep",
    )(u, u, u, mu, w0, a0, wup, aup, k_k, r_k)


SCAN_CHUNKS = 8
SCAN_GROUPS = 1
SCAN_SPLIT = 2


def _bmm(a, b):
    return lax.dot_general(a.astype(BF16), b.astype(BF16), (((2,), (1,)), ((0,), (0,))),
                           preferred_element_type=F32)


def _bmm_nt(a, b):
    return lax.dot_general(a.astype(BF16), b.astype(BF16), (((2,), (2,)), ((0,), (0,))),
                           preferred_element_type=F32)


def _stack_heads(t):
    lane = lax.broadcasted_iota(jnp.int32, t.shape, 2)
    first = lane < HEAD_DIM
    return jnp.concatenate([jnp.where(first, t, 0.0), jnp.where(first, 0.0, t)], axis=1)


def _chunk_terms(r, k, v, kk, lw, a, k_a, nfwd, interleave):
    n, c, _ = r.shape
    n2 = 2 * c
    ri = lax.broadcasted_iota(jnp.int32, (c, 3 * c), 0)
    ci = lax.broadcasted_iota(jnp.int32, (c, 3 * c), 1) % c
    rr = lax.broadcasted_iota(jnp.int32, (n2, n2), 0)
    cc = lax.broadcasted_iota(jnp.int32, (n2, n2), 1)
    same = (rr // c) == (cc // c)
    diag = (rr == cc)[None]
    before = (same & (cc % c < rr % c))[None]
    after = (same & (cc % c > rr % c))[None]

    def scores(lo, hi, done):
        rev = lo >= nfwd
        sl = slice(lo, hi)
        pieces = []
        rem = lw[sl]
        for _ in range(3):
            pieces.append(rem.astype(BF16))
            rem = rem - pieces[-1].astype(F32)
        pieces = jnp.concatenate(pieces, axis=1)
        tri = ((ci >= ri) if rev else (ci <= ri)).astype(BF16)
        wide = jnp.dot(tri, jnp.concatenate([pieces[i] for i in range(hi - lo)], axis=1),
                       preferred_element_type=F32)
        yield
        cum = jnp.stack([wide[:, i * LANES:(i + 1) * LANES] for i in range(hi - lo)])
        total = cum[:, 0:1, :] if rev else cum[:, c - 1:c, :]
        kd = k[sl] * (1.0 + (a[sl] - 1.0) * k_a)
        bb = kk[sl] * a[sl]
        e_neg = jnp.exp(-cum)
        bt = bb * e_neg
        kt = kd * e_neg
        xn = _stack_heads(-kk[sl] * jnp.exp(cum - lw[sl]))
        xr = _stack_heads(r[sl] * jnp.exp(cum))
        aa = _bmm_nt(jnp.concatenate([xn, xr], axis=1), jnp.concatenate([bt, bt, kt, kt], axis=1))
        yield
        strict = after if rev else before
        a_ab = jnp.where(strict, aa[:, :n2, :n2], 0.0)
        a_ak = jnp.where(strict, aa[:, :n2, n2:], 0.0)
        a_rb = jnp.where(strict | diag, aa[:, n2:, :n2], 0.0)
        a_rk = jnp.where(strict | diag, aa[:, n2:, n2:], 0.0)
        apow = _bmm(a_ab, a_ab)
        yield
        e_rem = jnp.exp(total - cum)
        vd = _stack_heads(v[sl])
        b2d = _stack_heads(bb * e_rem)
        k2d = _stack_heads(kd * e_rem)
        done.append((xn, xr, a_ab, a_ak, a_rb, a_rk, apow, vd, b2d, k2d, total))

    sub = n // SCAN_SPLIT
    parts = []
    waiting = [scores(lo, lo + sub, parts) for lo in range(0, n, sub)]
    running = []
    while waiting or running:
        if waiting:
            running.append(waiting.pop(0))
        running = [gen for gen in running if next(gen, True) is None]
        interleave()
    xn, xr, a_ab, a_ak, a_rb, a_rk, apow, vd, b2d, k2d, total = (
        jnp.concatenate([p[i] for p in parts], axis=0) for i in range(len(parts[0])))

    tinv = jnp.where(diag, 1.0, a_ab)
    for step in range(max(1, (c - 1).bit_length()) - 2):
        both = _bmm(apow, jnp.concatenate([apow, tinv], axis=2))
        apow = both[:, :, :n2]
        tinv = tinv + both[:, :, n2:]
        interleave()
    tinv = tinv + _bmm(apow, tinv)

    rhs_q = _bmm(a_ak, vd)
    pq = _bmm(tinv, jnp.concatenate([xn, rhs_q], axis=2))
    lhs = jnp.concatenate([jnp.concatenate([a_rb, a_rk], axis=2),
                           jnp.concatenate([jnp.swapaxes(b2d, 1, 2), jnp.swapaxes(k2d, 1, 2)], axis=2)], axis=1)
    rhs = jnp.concatenate([pq, jnp.concatenate([jnp.zeros_like(vd), vd], axis=2)], axis=1)
    out = _bmm(lhs, rhs)
    r2 = xr + out[:, :n2, :LANES]
    y0 = out[:, :n2, LANES:]
    gm = out[:, n2:, :LANES]
    h0 = out[:, n2:, LANES:]
    gcol = jnp.sum(jnp.where(diag, jnp.exp(total), 0.0), axis=2, keepdims=True)
    return jnp.concatenate([r2, gm], axis=1), y0, gcol, h0


def _scan_body(rf, kf, vf, kkf, lwf, af, rb, kb, vb, kkb, lwb, ab, ka_ref, yf_o, yb_o,
               st_ref, rg_ref, y0_ref, gc_ref, h0_ref, *, steps_per_seq):
    g = pl.program_id(0)

    @pl.when(g == 0)
    def _():
        st_ref[...] = jnp.zeros_like(st_ref)
        rg_ref[...] = jnp.zeros_like(rg_ref)
        y0_ref[...] = jnp.zeros_like(y0_ref)
        gc_ref[...] = jnp.zeros_like(gc_ref)
        h0_ref[...] = jnp.zeros_like(h0_ref)

    nch = SCAN_CHUNKS
    rows = nch * CHUNK
    n2 = 2 * CHUNK
    rd = (g + 1) % 2
    wr = g % 2
    fresh = (g % steps_per_seq) == (1 % steps_per_seq)
    state = [jnp.where(fresh, 0.0, st_ref[0]), jnp.where(fresh, 0.0, st_ref[1])]
    pending = [(grp, j) for grp in range(SCAN_GROUPS) for j in range(nch)]

    def chain_step(drain=False):
        if not pending:
            return
        if drain:
            while len(pending) > 1:
                chain_step()
        grp, j = pending.pop(0)
        f0 = grp * rows
        b0 = (SCAN_GROUPS - 1 - grp) * rows
        for d, jc, row0 in ((0, j, f0 + j * CHUNK), (1, 2 * nch - 1 - j, b0 + (nch - 1 - j) * CHUNK)):
            out = jnp.dot(rg_ref[rd, grp, jc], state[d].astype(BF16), preferred_element_type=F32)
            y = out[:n2] + y0_ref[rd, grp, jc]
            state[d] = state[d] * gc_ref[rd, grp, jc] + out[n2:] + h0_ref[rd, grp, jc]
            o_ref = yf_o if d == 0 else yb_o
            o_ref[row0:row0 + CHUNK, :] = (y[:CHUNK] + y[CHUNK:]).astype(o_ref.dtype)

    for grp in range(SCAN_GROUPS):
        f0 = grp * rows
        b0 = (SCAN_GROUPS - 1 - grp) * rows

        def both(f, b, f0=f0, b0=b0):
            return jnp.concatenate([f[f0:f0 + rows, :].astype(F32).reshape(nch, CHUNK, LANES),
                                    b[b0:b0 + rows, :].astype(F32).reshape(nch, CHUNK, LANES)], axis=0)

        rg, y0, gcol, h0 = _chunk_terms(both(rf, rb), both(kf, kb), both(vf, vb), both(kkf, kkb),
                                        both(lwf, lwb), both(af, ab), ka_ref[...], nch, chain_step)
        rg_ref[wr, grp] = rg.astype(BF16)
        y0_ref[wr, grp] = y0
        gc_ref[wr, grp] = jnp.broadcast_to(gcol, h0.shape)
        h0_ref[wr, grp] = h0
    while pending:
        chain_step()
    st_ref[0] = state[0]
    st_ref[1] = state[1]


def _rwkv_scan(r, k, v, kkn, lw, a, k_a, batch, seq):
    m = r.shape[0]
    rows = SCAN_GROUPS * SCAN_CHUNKS * CHUNK
    ns = seq // rows
    npair = D_A // LANES
    nblocks = batch * npair * ns

    def place(blk_id, rev):
        s = blk_id % ns
        p = (blk_id // ns) % npair
        b = blk_id // (ns * npair)
        return (b * ns + (ns - 1 - s if rev else s), p)

    cur = lambda g: jnp.minimum(g, nblocks - 1)
    prev = lambda g: jnp.maximum(g - 1, 0)
    blk = (rows, LANES)
    fspec = pl.BlockSpec(blk, lambda g: place(cur(g), False))
    bspec = pl.BlockSpec(blk, lambda g: place(cur(g), True))
    fdir = pl.BlockSpec((None,) + blk, lambda g: (0,) + place(cur(g), False))
    bdir = pl.BlockSpec((None,) + blk, lambda g: (1,) + place(cur(g), True))
    out = jax.ShapeDtypeStruct((m, D_A), BF16)
    nchain = 2 * SCAN_CHUNKS
    terms = lambda rws, dt: pltpu.VMEM((2, SCAN_GROUPS, nchain, rws, LANES), dt)
    return pl.pallas_call(
        functools.partial(_scan_body, steps_per_seq=ns),
        grid=(nblocks + 1,),
        in_specs=[fspec, fspec, fspec, fspec, fdir, fdir, bspec, bspec, bspec, bspec, bdir, bdir,
                  pl.BlockSpec((1, LANES), lambda g: (0, place(cur(g), False)[1]))],
        out_specs=[pl.BlockSpec(blk, lambda g: place(prev(g), False)),
                   pl.BlockSpec(blk, lambda g: place(prev(g), True))],
        out_shape=[out, out],
        scratch_shapes=[pltpu.VMEM((2, LANES, LANES), F32), terms(2 * CHUNK + LANES, BF16),
                        terms(2 * CHUNK, F32), terms(LANES, F32), terms(LANES, F32)],
        compiler_params=_cparams(("arbitrary",)),
        name="rwkv_scan",
    )(r, k, v, kkn, lw, a, r, k, v, kkn, lw, a, k_a)


POOL_PAD = 8
POOL_COLS = 256
POOL_ROWS = 256


def _pool_body(p_ref, w_ref, inv_ref, z_ref, b_ref, s_ref, o_ref, pad_ref, acc_ref, *, seq):
    cb = pl.program_id(1)
    ncb = pl.num_programs(1)
    pad_ref[0:POOL_PAD, :] = jnp.zeros((POOL_PAD, POOL_COLS), F32)
    pad_ref[POOL_PAD + seq:, :] = jnp.zeros((POOL_PAD, POOL_COLS), F32)
    pad_ref[POOL_PAD:POOL_PAD + seq, :] = p_ref[...].astype(F32)

    @pl.when(cb == 0)
    def _():
        acc_ref[...] = jnp.zeros_like(acc_ref)

    lane = lax.broadcasted_iota(jnp.int32, (1, POOL_COLS), 1) + cb * POOL_COLS
    group = lane // POOL_CH
    w = w_ref[...]

    def rows(i, carry):
        base = pl.multiple_of(i * POOL_ROWS, POOL_ROWS)
        nwin = POOL_ROWS + 2 * POOL_PAD
        win = pad_ref[pl.ds(base, nwin), :]
        pair = win + pltpu.roll(win, nwin - 1, 0)
        at = lambda a, d: a[POOL_PAD + d:POOL_PAD + d + POOL_ROWS]
        x = at(win, 0)
        s1 = x + at(win, -1) + at(win, 1)
        s2 = s1 + at(win, -2) + at(win, 2)
        s4 = s2 + at(pair, -4) + at(pair, 3)
        s8 = s4 + at(pair, -8) + at(pair, -6) + at(pair, 5) + at(pair, 7)
        total = jnp.where(group == 0, s1, jnp.where(group == 1, s2, jnp.where(group == 2, s4, s8)))
        mean = total * inv_ref[pl.ds(base, POOL_ROWS), :]
        acc_ref[pl.ds(base, POOL_ROWS), :] += _bdot(mean - x, w)
        return carry

    lax.fori_loop(0, seq // POOL_ROWS, rows, 0)

    @pl.when(cb == ncb - 1)
    def _():
        o_ref[...] = ((acc_ref[...] + b_ref[...]) * s_ref[...] * z_ref[...].astype(F32)).astype(o_ref.dtype)


def _pool_inv_count(seq):
    half = jnp.repeat(jnp.array(POOL_HALF, jnp.int32), POOL_CH)[None, :]
    t = jnp.arange(seq, dtype=jnp.int32)[:, None]
    count = jnp.minimum(t + half, seq - 1) - jnp.maximum(t - half, 0) + 1
    return 1.0 / count.astype(F32)


def _pool(p, zb, w_bd, inv_count, b, scale, batch, seq):
    m = p.shape[0]
    ncb = D_B // POOL_COLS
    full = pl.BlockSpec((seq, D_B), lambda bi, cb: (bi, 0))
    vec = pl.BlockSpec((1, D_B), lambda bi, cb: (0, 0))
    return pl.pallas_call(
        functools.partial(_pool_body, seq=seq),
        grid=(batch, ncb),
        in_specs=[pl.BlockSpec((seq, POOL_COLS), lambda bi, cb: (bi, cb)),
                  pl.BlockSpec((POOL_COLS, D_B), lambda bi, cb: (cb, 0)),
                  pl.BlockSpec((seq, POOL_COLS), lambda bi, cb: (0, cb)),
                  full, vec, vec],
        out_specs=full,
        out_shape=jax.ShapeDtypeStruct((m, D_B), BF16),
        scratch_shapes=[pltpu.VMEM((seq + 2 * POOL_PAD, POOL_COLS), F32), pltpu.VMEM((seq, D_B), F32)],
        compiler_params=_cparams(("parallel", "arbitrary")),
        name="pool_mixer",
    )(p, w_bd, inv_count, zb, b, scale)


def _rope_matrix():
    half = HEAD_DIM // 2
    src = lax.broadcasted_iota(jnp.int32, (LANES, LANES), 0)
    dst = lax.broadcasted_iota(jnp.int32, (LANES, LANES), 1)
    low = (dst % HEAD_DIM) < half
    return jnp.where(low & (src == dst + half), -1.0, jnp.where(~low & (src == dst - half), 1.0, 0.0)).astype(BF16)


def _rope(t, cos, sin, rot_matrix):
    return t * cos + _split_dot(t, rot_matrix, 2) * sin


ATTN_SIDE = 64
ATTN_TILES = 4
COMBINE_ROWS = 256


def _attn_body(q0, q1, q2, k0, k1, k2, v0, v1, v2, cos_ref, sin_ref, zc_ref, o_ref,
               qp_ref, kp_ref, vp_ref, op_ref, lp_ref, on_ref, ln_ref, *, seq):
    q_refs = (q0, q1, q2)
    k_refs = (k0, k1, k2)
    v_refs = (v0, v1, v2)
    scale = HEAD_DIM ** -0.5
    rot_matrix = _rope_matrix()
    for g, (half_span, dil) in enumerate(ATTN_GROUPS):
        cls = seq // dil
        for r in range(dil):
            rows = pl.ds(r, cls, stride=dil) if dil > 1 else pl.ds(0, seq)
            dst = pl.ds(r * cls, cls)
            cos = cos_ref[rows, :]
            sin = sin_ref[rows, :]
            qp_ref[g, dst, :] = (_rope(q_refs[g][rows, :], cos, sin, rot_matrix) * scale).astype(BF16)
            kp_ref[g, dst, :] = _rope(k_refs[g][rows, :], cos, sin, rot_matrix).astype(BF16)
            vp_ref[g, dst, :] = v_refs[g][rows, :].astype(BF16)

    nt = ATTN_TILES
    first = lax.broadcasted_iota(jnp.int32, (nt, Q_TILE, LANES), 2) < HEAD_DIM
    for g, (half_span, dil) in enumerate(ATTN_GROUPS):
        cls = seq // dil
        klen = min(cls, Q_TILE + 2 * ATTN_SIDE)
        rel = (lax.broadcasted_iota(jnp.int32, (Q_TILE, klen), 1)
               - lax.broadcasted_iota(jnp.int32, (Q_TILE, klen), 0))

        def tiles(it, carry, g=g, cls=cls, klen=klen, rel=rel):
            row0 = pl.multiple_of(it * (nt * Q_TILE), nt * Q_TILE)
            keys, vals, valid = [], [], []
            for j in range(nt):
                qrow = row0 + j * Q_TILE
                base = (qrow // cls) * cls
                krow = pl.multiple_of(base + jnp.clip(qrow - base - ATTN_SIDE, 0, cls - klen), ATTN_SIDE)
                keys.append(kp_ref[g, pl.ds(krow, klen), :])
                vals.append(vp_ref[g, pl.ds(krow, klen), :])
                valid.append(jnp.abs(rel + (krow - qrow)) <= ATTN_SIDE)
            q = qp_ref[g, pl.ds(row0, nt * Q_TILE), :].reshape(nt, Q_TILE, LANES)
            zero = jnp.zeros_like(q)
            qm = jnp.concatenate([jnp.where(first, q, zero), jnp.where(first, zero, q)], axis=0)
            keys = jnp.stack(keys + keys)
            vals = jnp.stack(vals + vals)
            valid = jnp.stack(valid + valid)
            s = jnp.where(valid, _bmm_nt(qm, keys), NEG_BIG)
            mx = jnp.max(s, axis=2, keepdims=True)
            e = jnp.exp(s - mx)
            den = jnp.sum(e, axis=2, keepdims=True)
            o = _bmm(e, vals) / den
            lse = jnp.broadcast_to(mx + jnp.log(den), o.shape)
            rows = pl.ds(row0, nt * Q_TILE)
            op_ref[g, rows, :] = jnp.where(first, o[:nt], o[nt:]).reshape(nt * Q_TILE, LANES)
            lp_ref[g, rows, :] = jnp.where(first, lse[:nt], lse[nt:]).reshape(nt * Q_TILE, LANES)
            return carry

        lax.fori_loop(0, seq // (nt * Q_TILE), tiles, 0)
        if dil > 1:
            for r in range(dil):
                src = pl.ds(r * cls, cls)
                on_ref[g - 1, pl.ds(r, cls, stride=dil), :] = op_ref[g, src, :]
                ln_ref[g - 1, pl.ds(r, cls, stride=dil), :] = lp_ref[g, src, :]

    def combine(i, carry):
        rows = pl.ds(pl.multiple_of(i * COMBINE_ROWS, COMBINE_ROWS), COMBINE_ROWS)
        ls = [lp_ref[0, rows, :], ln_ref[0, rows, :], ln_ref[1, rows, :]]
        os_ = [op_ref[0, rows, :], on_ref[0, rows, :], on_ref[1, rows, :]]
        top = jnp.maximum(jnp.maximum(ls[0], ls[1]), ls[2])
        ws = [jnp.exp(l - top) for l in ls]
        tot = ws[0] + ws[1] + ws[2]
        out = (ws[0] * os_[0] + ws[1] * os_[1] + ws[2] * os_[2]) / tot * zc_ref[rows, :].astype(F32)
        o_ref[rows, :] = out.astype(o_ref.dtype)
        return carry

    lax.fori_loop(0, seq // COMBINE_ROWS, combine, 0)


def _attention(qkv, zc, cos, sin, batch, seq):
    m = qkv.shape[0]
    ncol = D_C // LANES
    per_group = D_C_OUT // LANES
    ngroup = len(ATTN_GROUPS)
    spec = lambda sec, g: pl.BlockSpec((seq, LANES), lambda b, jp: (b, sec * ncol + g * per_group + jp))
    tab = pl.BlockSpec((seq, LANES), lambda b, jp: (0, 0))
    oz = pl.BlockSpec((seq, LANES), lambda b, jp: (b, jp))
    perm_bf = pltpu.VMEM((ngroup, seq, LANES), BF16)
    perm_f32 = pltpu.VMEM((ngroup, seq, LANES), F32)
    nat_f32 = pltpu.VMEM((ngroup - 1, seq, LANES), F32)
    return pl.pallas_call(
        functools.partial(_attn_body, seq=seq),
        grid=(batch, per_group),
        in_specs=[spec(0, 0), spec(0, 1), spec(0, 2), spec(1, 0), spec(1, 1), spec(1, 2),
                  spec(2, 0), spec(2, 1), spec(2, 2), tab, tab, oz],
        out_specs=oz,
        out_shape=jax.ShapeDtypeStruct((m, D_C_OUT), BF16),
        scratch_shapes=[perm_bf, perm_bf, perm_bf, perm_f32, perm_f32, nat_f32, nat_f32],
        compiler_params=_cparams(("parallel", "parallel")),
        name="dilated_attention",
    )(qkv, qkv, qkv, qkv, qkv, qkv, qkv, qkv, qkv, cos, sin, zc)


def _merge_body(x_ref, yf_ref, yb_ref, bonus_ref, za_ref, ob_ref, oc_ref, g_ref,
                gng_ref, gnb_ref, pa_ref, pb_ref, pc_ref, wo_ref, lng_ref, lnb_ref, o_ref, ob16_ref):
    ones = _head_ones(LANES)
    y = yf_ref[...].astype(F32) + yb_ref[...].astype(F32)
    parts = []
    for cb in range(D_A // LANES):
        sl = slice(cb * LANES, (cb + 1) * LANES)
        yc = y[:, sl]
        mu = _bdot(yc, ones) * (1.0 / HEAD_DIM)
        d = yc - mu
        var = _bdot(d * d, ones) * (1.0 / HEAD_DIM)
        parts.append(d * lax.rsqrt(var + RWKV_GN_EPS))
    yn = jnp.concatenate(parts, axis=1)
    o_a = (yn * gng_ref[...] + gnb_ref[...] + bonus_ref[...].astype(F32)) * za_ref[...].astype(F32)
    merged = (g_ref[:, 0:D_MODEL].astype(F32) * _bdot(o_a, pa_ref[...])
              + g_ref[:, D_MODEL:2 * D_MODEL].astype(F32) * _bdot(ob_ref[...], pb_ref[...])
              + g_ref[:, 2 * D_MODEL:3 * D_MODEL].astype(F32) * _bdot(oc_ref[...], pc_ref[...]))
    t = DEEPNORM_ALPHA * x_ref[...] + _bdot(merged, wo_ref[...])
    mean = jnp.mean(t, axis=1, keepdims=True)
    d = t - mean
    var = jnp.mean(d * d, axis=1, keepdims=True)
    out = d * lax.rsqrt(var + LN_EPS) * lng_ref[...] + lnb_ref[...]
    o_ref[...] = out
    ob16_ref[...] = out.astype(BF16)


def _merge(x, yf, yb, bonus, za, ob, oc, gates, gn_g, gn_b, pa, pb, pc, wo, ln_g, ln_b):
    m = x.shape[0]
    tm = min(256, m)
    rows = lambda n: pl.BlockSpec((tm, n), lambda i: (i, 0))
    full = lambda a: pl.BlockSpec(a.shape, lambda i: (0, 0))
    return pl.pallas_call(
        _merge_body,
        grid=(m // tm,),
        in_specs=[rows(D_MODEL), rows(D_A), rows(D_A), rows(D_A), rows(D_A), rows(D_B), rows(D_C_OUT),
                  rows(3 * D_MODEL), full(gn_g), full(gn_b), full(pa), full(pb), full(pc), full(wo),
                  full(ln_g), full(ln_b)],
        out_specs=[rows(D_MODEL), rows(D_MODEL)],
        out_shape=[jax.ShapeDtypeStruct((m, D_MODEL), F32), jax.ShapeDtypeStruct((m, D_MODEL), BF16)],
        compiler_params=_cparams(("parallel",)),
        name="merge_norm",
    )(x, yf, yb, bonus, za, ob, oc, gates, gn_g, gn_b, pa, pb, pc, wo, ln_g, ln_b)


def _rope_tables(seq):
    inv = jnp.power(ROPE_THETA, -jnp.arange(0, HEAD_DIM, 2, dtype=F32) / HEAD_DIM)
    ang = jnp.arange(seq, dtype=F32)[:, None] * inv[None, :]
    ang = jnp.concatenate([ang, ang, ang, ang], axis=-1)
    return jnp.cos(ang), jnp.sin(ang)


def _pad_lora(w_up):
    z = jnp.zeros_like(w_up[0])
    return jnp.stack([jnp.concatenate([w_up[0], z], axis=0), jnp.concatenate([z, w_up[1]], axis=0)])


def _block_diag(w_g):
    g, c, _ = w_g.shape
    eye = jnp.eye(g, dtype=w_g.dtype)
    return (eye[:, None, :, None] * w_g[:, :, None, :]).reshape(g * c, g * c)


def kernel(x, w_in, b_in, rwkv_mu, rwkv_w0, rwkv_w_up, rwkv_a0, rwkv_a_up, rwkv_k_k, rwkv_k_a, rwkv_r_k, rwkv_gn_g, rwkv_gn_b, pool_w, pool_b, pool_scale, proj_a, proj_b, proj_c, w_out, ln_g, ln_b):
    batch, seq, _ = x.shape
    m = batch * seq
    cos, sin = _rope_tables(seq)
    inv_count = _pool_inv_count(seq)
    row = lambda t: t.reshape(1, -1)
    bounds = [0]
    for width in (N_SHIFT, D_A, D_B, D_B, 3 * D_C, D_C_OUT, 3 * D_MODEL):
        bounds.append(bounds[-1] + width)
    acts = (None, "silu", None, "silu", None, "silu", "sigmoid")
    dtypes = (F32, BF16, BF16, BF16, F32, BF16, BF16)

    xf = x.reshape(m, D_MODEL)
    xb = xf.astype(BF16)
    for l in range(DEPTH):
        sec = [_linear(xb, w_in[l], row(b_in[l]), lo, hi - lo, act, dt)
               for lo, hi, act, dt in zip(bounds[:-1], bounds[1:], acts, dtypes)]
        u, za, p, zb, qkv, zc, gates = sec

        r, k, v, kkn, lw, a, bonus = _rwkv_prep(
            u, rwkv_mu[l], rwkv_w0[l], rwkv_a0[l],
            _pad_lora(rwkv_w_up[l]).astype(BF16), _pad_lora(rwkv_a_up[l]).astype(BF16),
            row(rwkv_k_k[l]), row(rwkv_r_k[l]), seq)
        yf, yb = _rwkv_scan(r, k, v, kkn, lw, a, row(rwkv_k_a[l]), batch, seq)
        ob = _pool(p, zb, _block_diag(pool_w[l]).astype(BF16), inv_count, row(pool_b[l]), row(pool_scale[l]),
                   batch, seq)
        oc = _attention(qkv, zc, cos, sin, batch, seq)
        xf, xb = _merge(xf, yf, yb, bonus, za, ob, oc, gates, row(rwkv_gn_g[l]), row(rwkv_gn_b[l]),
                        proj_a[l].astype(BF16), proj_b[l].astype(BF16), proj_c[l].astype(BF16),
                        w_out[l].astype(BF16), row(ln_g[l]), row(ln_b[l]))
    return xf.reshape(batch, seq, D_MODEL)
```

```python
import functools
import math

import jax
import jax.numpy as jnp
from jax import lax
from jax.experimental import pallas as pl
from jax.experimental.pallas import tpu as pltpu

F32 = jnp.float32
BF16 = jnp.bfloat16

D_MODEL = 1024
DEPTH = 4
HEAD_DIM = 64
LANES = 128
D_A = 1024
LORA = 64
N_SHIFT = 3 * D_A + 4 * LORA
RWKV_GN_EPS = 64e-5
LOG_DECAY_SCALE = -math.exp(-0.5)
POOL_HALF = (1, 2, 4, 8)
POOL_CH = 192
D_B = 768
D_C = 768
D_C_OUT = 256
ATTN_GROUPS = ((64, 1), (256, 4), (1024, 16))
ROPE_THETA = 10000.0
DEEPNORM_ALPHA = (2 * DEPTH) ** 0.25
LN_EPS = 1e-5
CHUNK = 64
Q_TILE = 128
NEG_BIG = -1e30
VMEM_LIMIT = 52 * 1024 * 1024


def _cparams(sem):
    return pltpu.CompilerParams(dimension_semantics=sem, vmem_limit_bytes=VMEM_LIMIT)


def _bdot(a, b):
    return jnp.dot(a.astype(BF16), b.astype(BF16), preferred_element_type=F32)


def _bdot_nt(a, b):
    return lax.dot_general(a.astype(BF16), b.astype(BF16), (((1,), (1,)), ((), ())),
                           preferred_element_type=F32)


def _split_dot(a, b_exact, terms):
    acc = None
    rem = a
    for _ in range(terms):
        piece = rem.astype(BF16)
        part = jnp.dot(piece, b_exact, preferred_element_type=F32)
        acc = part if acc is None else acc + part
        rem = rem - piece.astype(F32)
    return acc


def _head_ones(n):
    r = lax.broadcasted_iota(jnp.int32, (n, n), 0) // HEAD_DIM
    c = lax.broadcasted_iota(jnp.int32, (n, n), 1) // HEAD_DIM
    return (r == c).astype(BF16)


def _sigmoid(x):
    return 0.5 + 0.5 * jnp.tanh(0.5 * x)


def _linear_body(x_ref, w_ref, b_ref, o_ref, wb_ref, *, act, layer):
    @pl.when(pl.program_id(1) == 0)
    def _():
        wb_ref[...] = w_ref[0].astype(BF16)

    acc = jnp.dot(x_ref[...], wb_ref[...], preferred_element_type=F32) + b_ref[layer:layer + 1, :]
    if act == "silu":
        acc = acc * _sigmoid(acc)
    elif act == "sigmoid":
        acc = _sigmoid(acc)
    o_ref[...] = acc.astype(o_ref.dtype)


def _pick_tile(n, cap):
    best = LANES
    for t in range(LANES, cap + 1, LANES):
        if n % t == 0:
            best = t
    return best


def _linear(x, w, b, layer, lo, n, act, out_dtype):
    m, k = x.shape
    tm = min(1024, m)
    tn = _pick_tile(n, 1664)
    col = lambda j: pl.multiple_of(lo + j * tn, LANES)
    return pl.pallas_call(
        functools.partial(_linear_body, act=act, layer=layer),
        grid=(n // tn, m // tm),
        in_specs=[pl.BlockSpec((tm, k), lambda j, i: (i, 0)),
                  pl.BlockSpec((pl.Element(1), pl.Element(k), pl.Element(tn)), lambda j, i: (layer, 0, col(j))),
                  pl.BlockSpec((pl.Element(b.shape[0]), pl.Element(tn)), lambda j, i: (0, col(j)))],
        out_specs=pl.BlockSpec((tm, tn), lambda j, i: (i, j)),
        out_shape=jax.ShapeDtypeStruct((m, n), out_dtype),
        scratch_shapes=[pltpu.VMEM((k, tn), BF16)],
        compiler_params=_cparams(("parallel", "arbitrary")),
        name="linear_" + (act or "id"),
    )(x, w, b)


def _prep_body(u_ref, up_ref, un_ref, mu_ref, w0_ref, a0_ref, wup_ref, aup_ref, kk_ref, rk_ref,
               r_o, k_o, v_o, kkn_o, lw_o, a_o, bonus_o, *, tm, seq):
    i = pl.program_id(0)
    u = u_ref[...]
    row = lax.broadcasted_iota(jnp.int32, (8, 1), 0)
    t0 = (i * tm) % seq
    prev_edge = jnp.where(t0 == 0, 0.0, up_ref[7:8, :])
    next_edge = jnp.where(t0 + tm == seq, 0.0, un_ref[0:1, :])
    prev = pltpu.roll(u, 1, 0)
    prev = jnp.concatenate([jnp.where(row == 0, prev_edge, prev[0:8]), prev[8:]], axis=0)
    nxt = pltpu.roll(u, tm - 1, 0)
    nxt = jnp.concatenate([nxt[:tm - 8], jnp.where(row == 7, next_edge, nxt[tm - 8:])], axis=0)
    mu_prev = mu_ref[0:1, :]
    mu_next = mu_ref[1:2, :]
    u = (1.0 - mu_prev - mu_next) * u + mu_prev * prev + mu_next * nxt

    r = u[:, 0:D_A]
    k = u[:, D_A:2 * D_A]
    v = u[:, 2 * D_A:3 * D_A]
    wd = jnp.tanh(u[:, 3 * D_A:3 * D_A + 2 * LORA])
    ad = u[:, 3 * D_A + 2 * LORA:N_SHIFT]
    r_o[...] = r.astype(r_o.dtype)
    k_o[...] = k.astype(k_o.dtype)
    v_o[...] = v.astype(v_o.dtype)
    for z in range(2):
        warg = w0_ref[z:z + 1, :] + _bdot(wd, wup_ref[z])
        lw_o[z] = LOG_DECAY_SCALE * _sigmoid(warg)
        a_o[z] = _sigmoid(a0_ref[z:z + 1, :] + _bdot(ad, aup_ref[z])).astype(a_o.dtype)

    ones = _head_ones(LANES)
    for cb in range(D_A // LANES):
        sl = slice(cb * LANES, (cb + 1) * LANES)
        kk = k[:, sl] * kk_ref[:, sl]
        ss = _bdot(kk * kk, ones)
        kkn_o[:, sl] = (kk * lax.rsqrt(ss + 1e-12)).astype(kkn_o.dtype)
        rk = _bdot(r[:, sl] * k[:, sl] * rk_ref[:, sl], ones)
        bonus_o[:, sl] = (rk * v[:, sl]).astype(bonus_o.dtype)


def _rwkv_prep(u, mu, w0, a0, wup, aup, k_k, r_k, seq):
    m = u.shape[0]
    tm = min(256, seq)
    nb8 = m // 8
    full = lambda shape: pl.BlockSpec(shape, lambda i: (0,) * len(shape))
    row_spec = pl.BlockSpec((tm, D_A), lambda i: (i, 0))
    dir_spec = pl.BlockSpec((2, tm, D_A), lambda i: (0, i, 0))
    act = jax.ShapeDtypeStruct((m, D_A), BF16)
    act2 = jax.ShapeDtypeStruct((2, m, D_A), BF16)
    logw = jax.ShapeDtypeStruct((2, m, D_A), F32)
    return pl.pallas_call(
        functools.partial(_prep_body, tm=tm, seq=seq),
        grid=(m // tm,),
        in_specs=[pl.BlockSpec((tm, N_SHIFT), lambda i: (i, 0)),
                  pl.BlockSpec((8, N_SHIFT), lambda i: (jnp.maximum(i * (tm // 8) - 1, 0), 0)),
                  pl.BlockSpec((8, N_SHIFT), lambda i: (jnp.minimum((i + 1) * (tm // 8), nb8 - 1), 0)),
                  full((2, N_SHIFT)), full((2, D_A)), full((2, D_A)),
                  full((2, LANES, D_A)), full((2, LANES, D_A)), full((1, D_A)), full((1, D_A))],
        out_specs=[row_spec, row_spec, row_spec, row_spec, dir_spec, dir_spec, row_spec],
        out_shape=[act, act, act, act, logw, act2, act],
        compiler_params=_cparams(("parallel",)),
        name="rwkv_prep",
    )(u, u, u, mu, w0, a0, wup, aup, k_k, r_k)


SCAN_CHUNKS = 8
SCAN_GROUPS = 2
SCAN_SPLIT = 2


def _bmm(a, b):
    return lax.dot_general(a.astype(BF16), b.astype(BF16), (((2,), (1,)), ((0,), (0,))),
                           preferred_element_type=F32)


def _bmm_nt(a, b):
    return lax.dot_general(a.astype(BF16), b.astype(BF16), (((2,), (2,)), ((0,), (0,))),
                           preferred_element_type=F32)


def _stack_heads(t):
    lane = lax.broadcasted_iota(jnp.int32, t.shape, 2)
    first = lane < HEAD_DIM
    return jnp.concatenate([jnp.where(first, t, 0.0), jnp.where(first, 0.0, t)], axis=1)


def _chunk_terms(r, k, v, kk, lw, a, k_a, nfwd, interleave):
    n, c, _ = r.shape
    n2 = 2 * c
    ri = lax.broadcasted_iota(jnp.int32, (c, 3 * c), 0)
    ci = lax.broadcasted_iota(jnp.int32, (c, 3 * c), 1) % c
    rr = lax.broadcasted_iota(jnp.int32, (n2, n2), 0)
    cc = lax.broadcasted_iota(jnp.int32, (n2, n2), 1)
    same = (rr // c) == (cc // c)
    diag = (rr == cc)[None]
    before = (same & (cc % c < rr % c))[None]
    after = (same & (cc % c > rr % c))[None]

    def scores(lo, hi, done):
        rev = lo >= nfwd
        sl = slice(lo, hi)
        pieces = []
        rem = lw[sl]
        for _ in range(3):
            pieces.append(rem.astype(BF16))
            rem = rem - pieces[-1].astype(F32)
        pieces = jnp.concatenate(pieces, axis=1)
        tri = ((ci >= ri) if rev else (ci <= ri)).astype(BF16)
        wide = jnp.dot(tri, jnp.concatenate([pieces[i] for i in range(hi - lo)], axis=1),
                       preferred_element_type=F32)
        yield
        cum = jnp.stack([wide[:, i * LANES:(i + 1) * LANES] for i in range(hi - lo)])
        total = cum[:, 0:1, :] if rev else cum[:, c - 1:c, :]
        kd = k[sl] * (1.0 + (a[sl] - 1.0) * k_a)
        bb = kk[sl] * a[sl]
        e_neg = jnp.exp(-cum)
        bt = bb * e_neg
        kt = kd * e_neg
        xn = _stack_heads(-kk[sl] * jnp.exp(cum - lw[sl]))
        xr = _stack_heads(r[sl] * jnp.exp(cum))
        aa = _bmm_nt(jnp.concatenate([xn, xr], axis=1), jnp.concatenate([bt, bt, kt, kt], axis=1))
        yield
        strict = after if rev else before
        a_ab = jnp.where(strict, aa[:, :n2, :n2], 0.0)
        a_ak = jnp.where(strict, aa[:, :n2, n2:], 0.0)
        a_rb = jnp.where(strict | diag, aa[:, n2:, :n2], 0.0)
        a_rk = jnp.where(strict | diag, aa[:, n2:, n2:], 0.0)
        apow = _bmm(a_ab, a_ab)
        yield
        e_rem = jnp.exp(total - cum)
        vd = _stack_heads(v[sl])
        b2d = _stack_heads(bb * e_rem)
        k2d = _stack_heads(kd * e_rem)
        done.append((xn, xr, a_ab, a_ak, a_rb, a_rk, apow, vd, b2d, k2d, total))

    sub = n // SCAN_SPLIT
    parts = []
    waiting = [scores(lo, lo + sub, parts) for lo in range(0, n, sub)]
    running = []
    while waiting or running:
        if waiting:
            running.append(waiting.pop(0))
        running = [gen for gen in running if next(gen, True) is None]
        interleave()
    xn, xr, a_ab, a_ak, a_rb, a_rk, apow, vd, b2d, k2d, total = (
        jnp.concatenate([p[i] for p in parts], axis=0) for i in range(len(parts[0])))

    tinv = jnp.where(diag, 1.0, a_ab)
    for step in range(max(1, (c - 1).bit_length()) - 2):
        both = _bmm(apow, jnp.concatenate([apow, tinv], axis=2))
        apow = both[:, :, :n2]
        tinv = tinv + both[:, :, n2:]
        interleave()
    tinv = tinv + _bmm(apow, tinv)

    rhs_q = _bmm(a_ak, vd)
    pq = _bmm(tinv, jnp.concatenate([xn, rhs_q], axis=2))
    lhs = jnp.concatenate([jnp.concatenate([a_rb, a_rk], axis=2),
                           jnp.concatenate([jnp.swapaxes(b2d, 1, 2), jnp.swapaxes(k2d, 1, 2)], axis=2)], axis=1)
    rhs = jnp.concatenate([pq, jnp.concatenate([jnp.zeros_like(vd), vd], axis=2)], axis=1)
    out = _bmm(lhs, rhs)
    r2 = xr + out[:, :n2, :LANES]
    y0 = out[:, :n2, LANES:]
    gm = out[:, n2:, :LANES]
    h0 = out[:, n2:, LANES:]
    gcol = jnp.sum(jnp.where(diag, jnp.exp(total), 0.0), axis=2, keepdims=True)
    return jnp.concatenate([r2, gm], axis=1), y0, gcol, h0


def _scan_body(rf, kf, vf, kkf, lwf, af, rb, kb, vb, kkb, lwb, ab, ka_ref, yf_o, yb_o,
               st_ref, rg_ref, y0_ref, gc_ref, h0_ref, *, steps_per_seq):
    g = pl.program_id(0)

    @pl.when(g == 0)
    def _():
        st_ref[...] = jnp.zeros_like(st_ref)
        rg_ref[...] = jnp.zeros_like(rg_ref)
        y0_ref[...] = jnp.zeros_like(y0_ref)
        gc_ref[...] = jnp.zeros_like(gc_ref)
        h0_ref[...] = jnp.zeros_like(h0_ref)

    nch = SCAN_CHUNKS
    rows = nch * CHUNK
    n2 = 2 * CHUNK
    rd = (g + 1) % 2
    wr = g % 2
    fresh = (g % steps_per_seq) == (1 % steps_per_seq)
    state = [jnp.where(fresh, 0.0, st_ref[0]), jnp.where(fresh, 0.0, st_ref[1])]
    pending = [(grp, j) for grp in range(SCAN_GROUPS) for j in range(nch)]

    def chain_step(drain=False):
        if not pending:
            return
        if drain:
            while len(pending) > 1:
                chain_step()
        grp, j = pending.pop(0)
        f0 = grp * rows
        b0 = (SCAN_GROUPS - 1 - grp) * rows
        for d, jc, row0 in ((0, j, f0 + j * CHUNK), (1, 2 * nch - 1 - j, b0 + (nch - 1 - j) * CHUNK)):
            out = jnp.dot(rg_ref[rd, grp, jc], state[d].astype(BF16), preferred_element_type=F32)
            y = out[:n2] + y0_ref[rd, grp, jc]
            state[d] = state[d] * gc_ref[rd, grp, jc] + out[n2:] + h0_ref[rd, grp, jc]
            o_ref = yf_o if d == 0 else yb_o
            o_ref[row0:row0 + CHUNK, :] = (y[:CHUNK] + y[CHUNK:]).astype(o_ref.dtype)

    for grp in range(SCAN_GROUPS):
        f0 = grp * rows
        b0 = (SCAN_GROUPS - 1 - grp) * rows

        def both(f, b, f0=f0, b0=b0):
            return jnp.concatenate([f[f0:f0 + rows, :].astype(F32).reshape(nch, CHUNK, LANES),
                                    b[b0:b0 + rows, :].astype(F32).reshape(nch, CHUNK, LANES)], axis=0)

        rg, y0, gcol, h0 = _chunk_terms(both(rf, rb), both(kf, kb), both(vf, vb), both(kkf, kkb),
                                        both(lwf, lwb), both(af, ab), ka_ref[...], nch, chain_step)
        rg_ref[wr, grp] = rg.astype(BF16)
        y0_ref[wr, grp] = y0
        gc_ref[wr, grp] = jnp.broadcast_to(gcol, h0.shape)
        h0_ref[wr, grp] = h0
    while pending:
        chain_step()
    st_ref[0] = state[0]
    st_ref[1] = state[1]


def _rwkv_scan(r, k, v, kkn, lw, a, k_a, batch, seq):
    m = r.shape[0]
    rows = SCAN_GROUPS * SCAN_CHUNKS * CHUNK
    ns = seq // rows
    npair = D_A // LANES
    nblocks = batch * npair * ns

    def place(blk_id, rev):
        s = blk_id % ns
        p = (blk_id // ns) % npair
        b = blk_id // (ns * npair)
        return (b * ns + (ns - 1 - s if rev else s), p)

    cur = lambda g: jnp.minimum(g, nblocks - 1)
    prev = lambda g: jnp.maximum(g - 1, 0)
    blk = (rows, LANES)
    fspec = pl.BlockSpec(blk, lambda g: place(cur(g), False))
    bspec = pl.BlockSpec(blk, lambda g: place(cur(g), True))
    fdir = pl.BlockSpec((None,) + blk, lambda g: (0,) + place(cur(g), False))
    bdir = pl.BlockSpec((None,) + blk, lambda g: (1,) + place(cur(g), True))
    out = jax.ShapeDtypeStruct((m, D_A), BF16)
    nchain = 2 * SCAN_CHUNKS
    terms = lambda rws, dt: pltpu.VMEM((2, SCAN_GROUPS, nchain, rws, LANES), dt)
    return pl.pallas_call(
        functools.partial(_scan_body, steps_per_seq=ns),
        grid=(nblocks + 1,),
        in_specs=[fspec, fspec, fspec, fspec, fdir, fdir, bspec, bspec, bspec, bspec, bdir, bdir,
                  pl.BlockSpec((1, LANES), lambda g: (0, place(cur(g), False)[1]))],
        out_specs=[pl.BlockSpec(blk, lambda g: place(prev(g), False)),
                   pl.BlockSpec(blk, lambda g: place(prev(g), True))],
        out_shape=[out, out],
        scratch_shapes=[pltpu.VMEM((2, LANES, LANES), F32), terms(2 * CHUNK + LANES, BF16),
                        terms(2 * CHUNK, F32), terms(LANES, F32), terms(LANES, F32)],
        compiler_params=_cparams(("arbitrary",)),
        name="rwkv_scan",
    )(r, k, v, kkn, lw, a, r, k, v, kkn, lw, a, k_a)


POOL_PAD = 8
POOL_COLS = 256
POOL_ROWS = 256


def _pool_body(p_ref, w_ref, inv_ref, z_ref, b_ref, s_ref, o_ref, pad_ref, acc_ref, *, seq):
    cb = pl.program_id(1)
    ncb = pl.num_programs(1)
    pad_ref[0:POOL_PAD, :] = jnp.zeros((POOL_PAD, POOL_COLS), F32)
    pad_ref[POOL_PAD + seq:, :] = jnp.zeros((POOL_PAD, POOL_COLS), F32)
    pad_ref[POOL_PAD:POOL_PAD + seq, :] = p_ref[...].astype(F32)

    @pl.when(cb == 0)
    def _():
        acc_ref[...] = jnp.zeros_like(acc_ref)

    lane = lax.broadcasted_iota(jnp.int32, (1, POOL_COLS), 1) + cb * POOL_COLS
    group = lane // POOL_CH
    w = w_ref[...]

    def rows(i, carry):
        base = pl.multiple_of(i * POOL_ROWS, POOL_ROWS)
        nwin = POOL_ROWS + 2 * POOL_PAD
        win = pad_ref[pl.ds(base, nwin), :]
        pair = win + pltpu.roll(win, nwin - 1, 0)
        at = lambda a, d: a[POOL_PAD + d:POOL_PAD + d + POOL_ROWS]
        x = at(win, 0)
        s1 = x + at(win, -1) + at(win, 1)
        s2 = s1 + at(win, -2) + at(win, 2)
        s4 = s2 + at(pair, -4) + at(pair, 3)
        s8 = s4 + at(pair, -8) + at(pair, -6) + at(pair, 5) + at(pair, 7)
        total = jnp.where(group == 0, s1, jnp.where(group == 1, s2, jnp.where(group == 2, s4, s8)))
        mean = total * inv_ref[pl.ds(base, POOL_ROWS), :]
        acc_ref[pl.ds(base, POOL_ROWS), :] += _bdot(mean - x, w)
        return carry

    lax.fori_loop(0, seq // POOL_ROWS, rows, 0)

    @pl.when(cb == ncb - 1)
    def _():
        o_ref[...] = ((acc_ref[...] + b_ref[...]) * s_ref[...] * z_ref[...].astype(F32)).astype(o_ref.dtype)


def _pool_inv_count(seq):
    half = jnp.repeat(jnp.array(POOL_HALF, jnp.int32), POOL_CH)[None, :]
    t = jnp.arange(seq, dtype=jnp.int32)[:, None]
    count = jnp.minimum(t + half, seq - 1) - jnp.maximum(t - half, 0) + 1
    return 1.0 / count.astype(F32)


def _pool(p, zb, w_bd, inv_count, b, scale, batch, seq):
    m = p.shape[0]
    ncb = D_B // POOL_COLS
    full = pl.BlockSpec((seq, D_B), lambda bi, cb: (bi, 0))
    vec = pl.BlockSpec((1, D_B), lambda bi, cb: (0, 0))
    return pl.pallas_call(
        functools.partial(_pool_body, seq=seq),
        grid=(batch, ncb),
        in_specs=[pl.BlockSpec((seq, POOL_COLS), lambda bi, cb: (bi, cb)),
                  pl.BlockSpec((POOL_COLS, D_B), lambda bi, cb: (cb, 0)),
                  pl.BlockSpec((seq, POOL_COLS), lambda bi, cb: (0, cb)),
                  full, vec, vec],
        out_specs=full,
        out_shape=jax.ShapeDtypeStruct((m, D_B), BF16),
        scratch_shapes=[pltpu.VMEM((seq + 2 * POOL_PAD, POOL_COLS), F32), pltpu.VMEM((seq, D_B), F32)],
        compiler_params=_cparams(("parallel", "arbitrary")),
        name="pool_mixer",
    )(p, w_bd, inv_count, zb, b, scale)


def _rope_matrix():
    half = HEAD_DIM // 2
    src = lax.broadcasted_iota(jnp.int32, (LANES, LANES), 0)
    dst = lax.broadcasted_iota(jnp.int32, (LANES, LANES), 1)
    low = (dst % HEAD_DIM) < half
    return jnp.where(low & (src == dst + half), -1.0, jnp.where(~low & (src == dst - half), 1.0, 0.0)).astype(BF16)


def _rope(t, cos, sin, rot_matrix):
    return t * cos + _split_dot(t, rot_matrix, 2) * sin


ATTN_SIDE = 64
ATTN_TILES = 4
COMBINE_ROWS = 256


def _attn_body(q0, q1, q2, k0, k1, k2, v0, v1, v2, cos_ref, sin_ref, zc_ref, o_ref,
               qp_ref, kp_ref, vp_ref, op_ref, lp_ref, on_ref, ln_ref, *, seq):
    q_refs = (q0, q1, q2)
    k_refs = (k0, k1, k2)
    v_refs = (v0, v1, v2)
    scale = HEAD_DIM ** -0.5
    rot_matrix = _rope_matrix()
    for g, (half_span, dil) in enumerate(ATTN_GROUPS):
        cls = seq // dil
        for r in range(dil):
            rows = pl.ds(r, cls, stride=dil) if dil > 1 else pl.ds(0, seq)
            dst = pl.ds(r * cls, cls)
            cos = cos_ref[rows, :]
            sin = sin_ref[rows, :]
            qp_ref[g, dst, :] = (_rope(q_refs[g][rows, :], cos, sin, rot_matrix) * scale).astype(BF16)
            kp_ref[g, dst, :] = _rope(k_refs[g][rows, :], cos, sin, rot_matrix).astype(BF16)
            vp_ref[g, dst, :] = v_refs[g][rows, :].astype(BF16)

    nt = ATTN_TILES
    first = lax.broadcasted_iota(jnp.int32, (nt, Q_TILE, LANES), 2) < HEAD_DIM
    for g, (half_span, dil) in enumerate(ATTN_GROUPS):
        cls = seq // dil
        klen = min(cls, Q_TILE + 2 * ATTN_SIDE)
        rel = (lax.broadcasted_iota(jnp.int32, (Q_TILE, klen), 1)
               - lax.broadcasted_iota(jnp.int32, (Q_TILE, klen), 0))

        def tiles(it, carry, g=g, cls=cls, klen=klen, rel=rel):
            row0 = pl.multiple_of(it * (nt * Q_TILE), nt * Q_TILE)
            keys, vals, valid = [], [], []
            for j in range(nt):
                qrow = row0 + j * Q_TILE
                base = (qrow // cls) * cls
                krow = pl.multiple_of(base + jnp.clip(qrow - base - ATTN_SIDE, 0, cls - klen), ATTN_SIDE)
                keys.append(kp_ref[g, pl.ds(krow, klen), :])
                vals.append(vp_ref[g, pl.ds(krow, klen), :])
                valid.append(jnp.abs(rel + (krow - qrow)) <= ATTN_SIDE)
            q = qp_ref[g, pl.ds(row0, nt * Q_TILE), :].reshape(nt, Q_TILE, LANES)
            zero = jnp.zeros_like(q)
            qm = jnp.concatenate([jnp.where(first, q, zero), jnp.where(first, zero, q)], axis=0)
            keys = jnp.stack(keys + keys)
            vals = jnp.stack(vals + vals)
            valid = jnp.stack(valid + valid)
            s = jnp.where(valid, _bmm_nt(qm, keys), NEG_BIG)
            mx = jnp.max(s, axis=2, keepdims=True)
            e = jnp.exp(s - mx)
            den = jnp.sum(e, axis=2, keepdims=True)
            o = _bmm(e, vals) / den
            lse = jnp.broadcast_to(mx + jnp.log(den), o.shape)
            rows = pl.ds(row0, nt * Q_TILE)
            op_ref[g, rows, :] = jnp.where(first, o[:nt], o[nt:]).reshape(nt * Q_TILE, LANES)
            lp_ref[g, rows, :] = jnp.where(first, lse[:nt], lse[nt:]).reshape(nt * Q_TILE, LANES)
            return carry

        lax.fori_loop(0, seq // (nt * Q_TILE), tiles, 0)
        if dil > 1:
            for r in range(dil):
                src = pl.ds(r * cls, cls)
                on_ref[g - 1, pl.ds(r, cls, stride=dil), :] = op_ref[g, src, :]
                ln_ref[g - 1, pl.ds(r, cls, stride=dil), :] = lp_ref[g, src, :]

    def combine(i, carry):
        rows = pl.ds(pl.multiple_of(i * COMBINE_ROWS, COMBINE_ROWS), COMBINE_ROWS)
        ls = [lp_ref[0, rows, :], ln_ref[0, rows, :], ln_ref[1, rows, :]]
        os_ = [op_ref[0, rows, :], on_ref[0, rows, :], on_ref[1, rows, :]]
        top = jnp.maximum(jnp.maximum(ls[0], ls[1]), ls[2])
        ws = [jnp.exp(l - top) for l in ls]
        tot = ws[0] + ws[1] + ws[2]
        out = (ws[0] * os_[0] + ws[1] * os_[1] + ws[2] * os_[2]) / tot * zc_ref[rows, :].astype(F32)
        o_ref[rows, :] = out.astype(o_ref.dtype)
        return carry

    lax.fori_loop(0, seq // COMBINE_ROWS, combine, 0)


def _attention(qkv, zc, cos, sin, batch, seq):
    m = qkv.shape[0]
    ncol = D_C // LANES
    per_group = D_C_OUT // LANES
    ngroup = len(ATTN_GROUPS)
    spec = lambda sec, g: pl.BlockSpec((seq, LANES), lambda b, jp: (b, sec * ncol + g * per_group + jp))
    tab = pl.BlockSpec((seq, LANES), lambda b, jp: (0, 0))
    oz = pl.BlockSpec((seq, LANES), lambda b, jp: (b, jp))
    perm_bf = pltpu.VMEM((ngroup, seq, LANES), BF16)
    perm_f32 = pltpu.VMEM((ngroup, seq, LANES), F32)
    nat_f32 = pltpu.VMEM((ngroup - 1, seq, LANES), F32)
    return pl.pallas_call(
        functools.partial(_attn_body, seq=seq),
        grid=(batch, per_group),
        in_specs=[spec(0, 0), spec(0, 1), spec(0, 2), spec(1, 0), spec(1, 1), spec(1, 2),
                  spec(2, 0), spec(2, 1), spec(2, 2), tab, tab, oz],
        out_specs=oz,
        out_shape=jax.ShapeDtypeStruct((m, D_C_OUT), BF16),
        scratch_shapes=[perm_bf, perm_bf, perm_bf, perm_f32, perm_f32, nat_f32, nat_f32],
        compiler_params=_cparams(("parallel", "parallel")),
        name="dilated_attention",
    )(qkv, qkv, qkv, qkv, qkv, qkv, qkv, qkv, qkv, cos, sin, zc)


def _merge_body(x_ref, yf_ref, yb_ref, bonus_ref, za_ref, ob_ref, oc_ref, g_ref,
                gng_ref, gnb_ref, pa_ref, pb_ref, pc_ref, wo_ref, lng_ref, lnb_ref, o_ref, ob16_ref):
    ones = _head_ones(LANES)
    y = yf_ref[...].astype(F32) + yb_ref[...].astype(F32)
    parts = []
    for cb in range(D_A // LANES):
        sl = slice(cb * LANES, (cb + 1) * LANES)
        yc = y[:, sl]
        mu = _bdot(yc, ones) * (1.0 / HEAD_DIM)
        d = yc - mu
        var = _bdot(d * d, ones) * (1.0 / HEAD_DIM)
        parts.append(d * lax.rsqrt(var + RWKV_GN_EPS))
    yn = jnp.concatenate(parts, axis=1)
    o_a = (yn * gng_ref[...] + gnb_ref[...] + bonus_ref[...].astype(F32)) * za_ref[...].astype(F32)
    merged = (g_ref[:, 0:D_MODEL].astype(F32) * _bdot(o_a, pa_ref[...])
              + g_ref[:, D_MODEL:2 * D_MODEL].astype(F32) * _bdot(ob_ref[...], pb_ref[...])
              + g_ref[:, 2 * D_MODEL:3 * D_MODEL].astype(F32) * _bdot(oc_ref[...], pc_ref[...]))
    t = DEEPNORM_ALPHA * x_ref[...] + _bdot(merged, wo_ref[...])
    mean = jnp.mean(t, axis=1, keepdims=True)
    d = t - mean
    var = jnp.mean(d * d, axis=1, keepdims=True)
    out = d * lax.rsqrt(var + LN_EPS) * lng_ref[...] + lnb_ref[...]
    o_ref[...] = out
    ob16_ref[...] = out.astype(BF16)


def _merge(x, yf, yb, bonus, za, ob, oc, gates, gn_g, gn_b, pa, pb, pc, wo, ln_g, ln_b):
    m = x.shape[0]
    tm = min(256, m)
    rows = lambda n: pl.BlockSpec((tm, n), lambda i: (i, 0))
    full = lambda a: pl.BlockSpec(a.shape, lambda i: (0, 0))
    return pl.pallas_call(
        _merge_body,
        grid=(m // tm,),
        in_specs=[rows(D_MODEL), rows(D_A), rows(D_A), rows(D_A), rows(D_A), rows(D_B), rows(D_C_OUT),
                  rows(3 * D_MODEL), full(gn_g), full(gn_b), full(pa), full(pb), full(pc), full(wo),
                  full(ln_g), full(ln_b)],
        out_specs=[rows(D_MODEL), rows(D_MODEL)],
        out_shape=[jax.ShapeDtypeStruct((m, D_MODEL), F32), jax.ShapeDtypeStruct((m, D_MODEL), BF16)],
        compiler_params=_cparams(("parallel",)),
        name="merge_norm",
    )(x, yf, yb, bonus, za, ob, oc, gates, gn_g, gn_b, pa, pb, pc, wo, ln_g, ln_b)


def _rope_tables(seq):
    inv = jnp.power(ROPE_THETA, -jnp.arange(0, HEAD_DIM, 2, dtype=F32) / HEAD_DIM)
    ang = jnp.arange(seq, dtype=F32)[:, None] * inv[None, :]
    ang = jnp.concatenate([ang, ang, ang, ang], axis=-1)
    return jnp.cos(ang), jnp.sin(ang)


def _pad_lora(w_up):
    z = jnp.zeros_like(w_up[0])
    return jnp.stack([jnp.concatenate([w_up[0], z], axis=0), jnp.concatenate([z, w_up[1]], axis=0)])


def _block_diag(w_g):
    g, c, _ = w_g.shape
    eye = jnp.eye(g, dtype=w_g.dtype)
    return (eye[:, None, :, None] * w_g[:, :, None, :]).reshape(g * c, g * c)


def kernel(x, w_in, b_in, rwkv_mu, rwkv_w0, rwkv_w_up, rwkv_a0, rwkv_a_up, rwkv_k_k, rwkv_k_a, rwkv_r_k, rwkv_gn_g, rwkv_gn_b, pool_w, pool_b, pool_scale, proj_a, proj_b, proj_c, w_out, ln_g, ln_b):
    batch, seq, _ = x.shape
    m = batch * seq
    cos, sin = _rope_tables(seq)
    inv_count = _pool_inv_count(seq)
    row = lambda t: t.reshape(1, -1)
    bounds = [0]
    for width in (N_SHIFT, D_A, D_B, D_B, 3 * D_C, D_C_OUT, 3 * D_MODEL):
        bounds.append(bounds[-1] + width)
    acts = (None, "silu", None, "silu", None, "silu", "sigmoid")
    dtypes = (F32, BF16, BF16, BF16, F32, BF16, BF16)

    xf = x.reshape(m, D_MODEL)
    xb = xf.astype(BF16)
    for l in range(DEPTH):
        sec = [_linear(xb, w_in, b_in, l, lo, hi - lo, act, dt)
               for lo, hi, act, dt in zip(bounds[:-1], bounds[1:], acts, dtypes)]
        u, za, p, zb, qkv, zc, gates = sec

        r, k, v, kkn, lw, a, bonus = _rwkv_prep(
            u, rwkv_mu[l], rwkv_w0[l], rwkv_a0[l],
            _pad_lora(rwkv_w_up[l]).astype(BF16), _pad_lora(rwkv_a_up[l]).astype(BF16),
            row(rwkv_k_k[l]), row(rwkv_r_k[l]), seq)
        yf, yb = _rwkv_scan(r, k, v, kkn, lw, a, row(rwkv_k_a[l]), batch, seq)
        ob = _pool(p, zb, _block_diag(pool_w[l]).astype(BF16), inv_count, row(pool_b[l]), row(pool_scale[l]),
                   batch, seq)
        oc = _attention(qkv, zc, cos, sin, batch, seq)
        xf, xb = _merge(xf, yf, yb, bonus, za, ob, oc, gates, row(rwkv_gn_g[l]), row(rwkv_gn_b[l]),
                        proj_a[l].astype(BF16), proj_b[l].astype(BF16), proj_c[l].astype(BF16),
                        w_out[l].astype(BF16), row(ln_g[l]), row(ln_b[l]))
    return xf.reshape(batch, seq, D_MODEL)
```

```python
import functools
import math

import jax
import jax.numpy as jnp
from jax import lax
from jax.experimental import pallas as pl
from jax.experimental.pallas import tpu as pltpu

F32 = jnp.float32
BF16 = jnp.bfloat16

D_MODEL = 1024
DEPTH = 4
HEAD_DIM = 64
LANES = 128
D_A = 1024
LORA = 64
N_SHIFT = 3 * D_A + 4 * LORA
RWKV_GN_EPS = 64e-5
LOG_DECAY_SCALE = -math.exp(-0.5)
POOL_HALF = (1, 2, 4, 8)
POOL_CH = 192
D_B = 768
D_C = 768
D_C_OUT = 256
ATTN_GROUPS = ((64, 1), (256, 4), (1024, 16))
ROPE_THETA = 10000.0
DEEPNORM_ALPHA = (2 * DEPTH) ** 0.25
LN_EPS = 1e-5
CHUNK = 64
Q_TILE = 128
NEG_BIG = -1e30
VMEM_LIMIT = 52 * 1024 * 1024


def _cparams(sem):
    return pltpu.CompilerParams(dimension_semantics=sem, vmem_limit_bytes=VMEM_LIMIT)


def _bdot(a, b):
    return jnp.dot(a.astype(BF16), b.astype(BF16), preferred_element_type=F32)


def _bdot_nt(a, b):
    return lax.dot_general(a.astype(BF16), b.astype(BF16), (((1,), (1,)), ((), ())),
                           preferred_element_type=F32)


def _split_dot(a, b_exact, terms):
    acc = None
    rem = a
    for _ in range(terms):
        piece = rem.astype(BF16)
        part = jnp.dot(piece, b_exact, preferred_element_type=F32)
        acc = part if acc is None else acc + part
        rem = rem - piece.astype(F32)
    return acc


def _head_ones(n):
    r = lax.broadcasted_iota(jnp.int32, (n, n), 0) // HEAD_DIM
    c = lax.broadcasted_iota(jnp.int32, (n, n), 1) // HEAD_DIM
    return (r == c).astype(BF16)


def _sigmoid(x):
    return 0.5 + 0.5 * jnp.tanh(0.5 * x)


def _linear_body(x_ref, w_ref, b_ref, o_ref, wb_ref, *, act, layer):
    @pl.when(pl.program_id(1) == 0)
    def _():
        wb_ref[...] = w_ref[0].astype(BF16)

    acc = jnp.dot(x_ref[...], wb_ref[...], preferred_element_type=F32) + b_ref[layer:layer + 1, :]
    if act == "silu":
        acc = acc * _sigmoid(acc)
    elif act == "sigmoid":
        acc = _sigmoid(acc)
    o_ref[...] = acc.astype(o_ref.dtype)


def _pick_tile(n, cap):
    best = LANES
    for t in range(LANES, cap + 1, LANES):
        if n % t == 0:
            best = t
    return best


def _linear(x, w, b, layer, lo, n, act, out_dtype):
    m, k = x.shape
    tm = min(1024, m)
    tn = _pick_tile(n, 1664)
    col = lambda j: pl.multiple_of(lo + j * tn, LANES)
    return pl.pallas_call(
        functools.partial(_linear_body, act=act, layer=layer),
        grid=(n // tn, m // tm),
        in_specs=[pl.BlockSpec((tm, k), lambda j, i: (i, 0)),
                  pl.BlockSpec((pl.Element(1), pl.Element(k), pl.Element(tn)), lambda j, i: (layer, 0, col(j))),
                  pl.BlockSpec((pl.Element(b.shape[0]), pl.Element(tn)), lambda j, i: (0, col(j)))],
        out_specs=pl.BlockSpec((tm, tn), lambda j, i: (i, j)),
        out_shape=jax.ShapeDtypeStruct((m, n), out_dtype),
        scratch_shapes=[pltpu.VMEM((k, tn), BF16)],
        compiler_params=_cparams(("parallel", "arbitrary")),
        name="linear_" + (act or "id"),
    )(x, w, b)


def _prep_body(u_ref, up_ref, un_ref, mu_ref, w0_ref, a0_ref, wup_ref, aup_ref, kk_ref, rk_ref,
               r_o, k_o, v_o, kkn_o, lw_o, a_o, bonus_o, *, tm, seq):
    i = pl.program_id(0)
    u = u_ref[...]
    row = lax.broadcasted_iota(jnp.int32, (8, 1), 0)
    t0 = (i * tm) % seq
    prev_edge = jnp.where(t0 == 0, 0.0, up_ref[7:8, :])
    next_edge = jnp.where(t0 + tm == seq, 0.0, un_ref[0:1, :])
    prev = pltpu.roll(u, 1, 0)
    prev = jnp.concatenate([jnp.where(row == 0, prev_edge, prev[0:8]), prev[8:]], axis=0)
    nxt = pltpu.roll(u, tm - 1, 0)
    nxt = jnp.concatenate([nxt[:tm - 8], jnp.where(row == 7, next_edge, nxt[tm - 8:])], axis=0)
    mu_prev = mu_ref[0:1, :]
    mu_next = mu_ref[1:2, :]
    u = (1.0 - mu_prev - mu_next) * u + mu_prev * prev + mu_next * nxt

    r = u[:, 0:D_A]
    k = u[:, D_A:2 * D_A]
    v = u[:, 2 * D_A:3 * D_A]
    wd = jnp.tanh(u[:, 3 * D_A:3 * D_A + 2 * LORA])
    ad = u[:, 3 * D_A + 2 * LORA:N_SHIFT]
    r_o[...] = r.astype(r_o.dtype)
    k_o[...] = k.astype(k_o.dtype)
    v_o[...] = v.astype(v_o.dtype)
    for z in range(2):
        warg = w0_ref[z:z + 1, :] + _bdot(wd, wup_ref[z])
        lw_o[z] = LOG_DECAY_SCALE * _sigmoid(warg)
        a_o[z] = _sigmoid(a0_ref[z:z + 1, :] + _bdot(ad, aup_ref[z])).astype(a_o.dtype)

    ones = _head_ones(LANES)
    for cb in range(D_A // LANES):
        sl = slice(cb * LANES, (cb + 1) * LANES)
        kk = k[:, sl] * kk_ref[:, sl]
        ss = _bdot(kk * kk, ones)
        kkn_o[:, sl] = (kk * lax.rsqrt(ss + 1e-12)).astype(kkn_o.dtype)
        rk = _bdot(r[:, sl] * k[:, sl] * rk_ref[:, sl], ones)
        bonus_o[:, sl] = (rk * v[:, sl]).astype(bonus_o.dtype)


def _rwkv_prep(u, mu, w0, a0, wup, aup, k_k, r_k, seq):
    m = u.shape[0]
    tm = min(256, seq)
    nb8 = m // 8
    full = lambda shape: pl.BlockSpec(shape, lambda i: (0,) * len(shape))
    row_spec = pl.BlockSpec((tm, D_A), lambda i: (i, 0))
    dir_spec = pl.BlockSpec((2, tm, D_A), lambda i: (0, i, 0))
    act = jax.ShapeDtypeStruct((m, D_A), BF16)
    act2 = jax.ShapeDtypeStruct((2, m, D_A), BF16)
    logw = jax.ShapeDtypeStruct((2, m, D_A), F32)
    return pl.pallas_call(
        functools.partial(_prep_body, tm=tm, seq=seq),
        grid=(m // tm,),
        in_specs=[pl.BlockSpec((tm, N_SHIFT), lambda i: (i, 0)),
                  pl.BlockSpec((8, N_SHIFT), lambda i: (jnp.maximum(i * (tm // 8) - 1, 0), 0)),
                  pl.BlockSpec((8, N_SHIFT), lambda i: (jnp.minimum((i + 1) * (tm // 8), nb8 - 1), 0)),
                  full((2, N_SHIFT)), full((2, D_A)), full((2, D_A)),
                  full((2, LANES, D_A)), full((2, LANES, D_A)), full((1, D_A)), full((1, D_A))],
        out_specs=[row_spec, row_spec, row_spec, row_spec, dir_spec, dir_spec, row_spec],
        out_shape=[act, act, act, act, logw, act2, act],
        compiler_params=_cparams(("parallel",)),
        name="rwkv_prep",
    )(u, u, u, mu, w0, a0, wup, aup, k_k, r_k)


SCAN_CHUNKS = 8
SCAN_GROUPS = 2
SCAN_STAGGER = 11
SCAN_SPLIT = 2


def _bmm(a, b):
    return lax.dot_general(a.astype(BF16), b.astype(BF16), (((2,), (1,)), ((0,), (0,))),
                           preferred_element_type=F32)


def _bmm_nt(a, b):
    return lax.dot_general(a.astype(BF16), b.astype(BF16), (((2,), (2,)), ((0,), (0,))),
                           preferred_element_type=F32)


def _stack_heads(t):
    lane = lax.broadcasted_iota(jnp.int32, t.shape, 2)
    first = lane < HEAD_DIM
    return jnp.concatenate([jnp.where(first, t, 0.0), jnp.where(first, 0.0, t)], axis=1)


def _chunk_terms(load, k_a, nfwd, finish):
    r, k, v, kk, lw, a = load()
    n, c, _ = r.shape
    n2 = 2 * c
    ri = lax.broadcasted_iota(jnp.int32, (c, 3 * c), 0)
    ci = lax.broadcasted_iota(jnp.int32, (c, 3 * c), 1) % c
    rr = lax.broadcasted_iota(jnp.int32, (n2, n2), 0)
    cc = lax.broadcasted_iota(jnp.int32, (n2, n2), 1)
    same = (rr // c) == (cc // c)
    diag = (rr == cc)[None]
    before = (same & (cc % c < rr % c))[None]
    after = (same & (cc % c > rr % c))[None]

    def scores(lo, hi, done):
        rev = lo >= nfwd
        sl = slice(lo, hi)
        pieces = []
        rem = lw[sl]
        for _ in range(3):
            pieces.append(rem.astype(BF16))
            rem = rem - pieces[-1].astype(F32)
        pieces = jnp.concatenate(pieces, axis=1)
        tri = ((ci >= ri) if rev else (ci <= ri)).astype(BF16)
        wide = jnp.dot(tri, jnp.concatenate([pieces[i] for i in range(hi - lo)], axis=1),
                       preferred_element_type=F32)
        yield
        cum = jnp.stack([wide[:, i * LANES:(i + 1) * LANES] for i in range(hi - lo)])
        total = cum[:, 0:1, :] if rev else cum[:, c - 1:c, :]
        kd = k[sl] * (1.0 + (a[sl] - 1.0) * k_a)
        bb = kk[sl] * a[sl]
        e_neg = jnp.exp(-cum)
        bt = bb * e_neg
        kt = kd * e_neg
        xn = _stack_heads(-kk[sl] * jnp.exp(cum - lw[sl])).astype(BF16)
        xr = _stack_heads(r[sl] * jnp.exp(cum))
        aa = _bmm_nt(jnp.concatenate([xn, xr.astype(BF16)], axis=1), jnp.concatenate([bt, kt], axis=1))
        yield
        low = lax.broadcasted_iota(jnp.int32, (1, n2), 1) < c
        strict = (after if rev else before)[0]
        causal = strict | diag[0]
        masked = []
        for i in range(hi - lo):
            swapped = pltpu.roll(aa[i], c, 1)
            ab = jnp.where(low, aa[i], swapped)
            ak = jnp.where(low, swapped, aa[i])
            masked.append((jnp.where(strict, ab[:n2], 0.0), jnp.where(strict, ak[:n2], 0.0).astype(BF16),
                           jnp.where(causal, ab[n2:], 0.0).astype(BF16),
                           jnp.where(causal, ak[n2:], 0.0).astype(BF16)))
        a_ab, a_ak, a_rb, a_rk = (jnp.stack([m[j] for m in masked]) for j in range(4))
        apow = _bmm(a_ab, a_ab)
        yield
        e_rem = jnp.exp(total - cum)
        vd = _stack_heads(v[sl]).astype(BF16)
        b2d = _stack_heads(bb * e_rem)
        k2d = _stack_heads(kd * e_rem)
        done.append((xn, xr, a_ab, a_ak, a_rb, a_rk, apow, vd, b2d, k2d, total))

    sub = n // SCAN_SPLIT
    parts = []
    waiting = [scores(lo, lo + sub, parts) for lo in range(0, n, sub)]
    running = []
    while waiting or running:
        if waiting:
            running.append(waiting.pop(0))
        running = [gen for gen in running if next(gen, True) is None]
        yield
    xn, xr, a_ab, a_ak, a_rb, a_rk, apow, vd, b2d, k2d, total = (
        jnp.concatenate([p[i] for p in parts], axis=0) for i in range(len(parts[0])))

    tinv = jnp.where(diag, 1.0, a_ab)
    for step in range(max(1, (c - 1).bit_length()) - 2):
        both = _bmm(apow, jnp.concatenate([apow, tinv], axis=2))
        apow = both[:, :, :n2]
        tinv = tinv + both[:, :, n2:]
        yield
    tinv = tinv + _bmm(apow, tinv)

    rhs_q = _bmm(a_ak, vd).astype(BF16)
    pq = _bmm(tinv, jnp.concatenate([xn, rhs_q], axis=2)).astype(BF16)
    yield
    b2t = jnp.swapaxes(b2d, 1, 2).astype(BF16)
    k2t = jnp.swapaxes(k2d, 1, 2).astype(BF16)
    lhs = jnp.concatenate([jnp.concatenate([a_rb, a_rk], axis=2), jnp.concatenate([b2t, k2t], axis=2)], axis=1)
    rhs = jnp.concatenate([pq, jnp.concatenate([jnp.zeros_like(vd), vd], axis=2)], axis=1)
    out = _bmm(lhs, rhs)
    r2 = xr + out[:, :n2, :LANES]
    y0 = out[:, :n2, LANES:]
    gm = out[:, n2:, :LANES]
    h0 = out[:, n2:, LANES:]
    gcol = jnp.sum(jnp.where(diag, jnp.exp(total), 0.0), axis=2, keepdims=True)
    finish(jnp.concatenate([r2, gm], axis=1), y0, gcol, h0)


def _scan_body(rf, kf, vf, kkf, lwf, af, rb, kb, vb, kkb, lwb, ab, ka_ref, yf_o, yb_o,
               st_ref, rg_ref, y0_ref, gc_ref, h0_ref, *, steps_per_seq):
    g = pl.program_id(0)

    @pl.when(g == 0)
    def _():
        st_ref[...] = jnp.zeros_like(st_ref)
        rg_ref[...] = jnp.zeros_like(rg_ref)
        y0_ref[...] = jnp.zeros_like(y0_ref)
        gc_ref[...] = jnp.zeros_like(gc_ref)
        h0_ref[...] = jnp.zeros_like(h0_ref)

    nch = SCAN_CHUNKS
    rows = nch * CHUNK
    n2 = 2 * CHUNK
    rd = (g + 1) % 2
    wr = g % 2
    fresh = (g % steps_per_seq) == (1 % steps_per_seq)
    state = [jnp.where(fresh, 0.0, st_ref[0]), jnp.where(fresh, 0.0, st_ref[1])]
    pending = [(grp, j) for grp in range(SCAN_GROUPS) for j in range(nch)]

    def chain_step():
        if not pending:
            return
        grp, j = pending.pop(0)
        f0 = grp * rows
        b0 = (SCAN_GROUPS - 1 - grp) * rows
        for d, jc, row0 in ((0, j, f0 + j * CHUNK), (1, 2 * nch - 1 - j, b0 + (nch - 1 - j) * CHUNK)):
            out = jnp.dot(rg_ref[rd, grp, jc], state[d].astype(BF16), preferred_element_type=F32)
            y = out[:n2] + y0_ref[rd, grp, jc]
            state[d] = state[d] * gc_ref[rd, grp, jc] + out[n2:] + h0_ref[rd, grp, jc]
            o_ref = yf_o if d == 0 else yb_o
            o_ref[row0:row0 + CHUNK, :] = (y[:CHUNK] + y[CHUNK:]).astype(o_ref.dtype)

    def group_terms(grp):
        f0 = grp * rows
        b0 = (SCAN_GROUPS - 1 - grp) * rows

        def both(f, b):
            return jnp.concatenate([f[f0:f0 + rows, :].astype(F32).reshape(nch, CHUNK, LANES),
                                    b[b0:b0 + rows, :].astype(F32).reshape(nch, CHUNK, LANES)], axis=0)

        def load():
            return (both(rf, rb), both(kf, kb), both(vf, vb), both(kkf, kkb), both(lwf, lwb), both(af, ab))

        def finish(rg, y0, gcol, h0):
            rg_ref[wr, grp] = rg.astype(BF16)
            y0_ref[wr, grp] = y0
            gc_ref[wr, grp] = jnp.broadcast_to(gcol, h0.shape)
            h0_ref[wr, grp] = h0

        return _chunk_terms(load, ka_ref[...], nch, finish)

    waiting = list(range(SCAN_GROUPS))
    running = []
    tick = 0
    while waiting or running or pending:
        if waiting and tick % SCAN_STAGGER == 0:
            running.append(group_terms(waiting.pop(0)))
        running = [gen for gen in running if next(gen, True) is None]
        chain_step()
        tick += 1
    st_ref[0] = state[0]
    st_ref[1] = state[1]


def _rwkv_scan(r, k, v, kkn, lw, a, k_a, batch, seq):
    m = r.shape[0]
    rows = SCAN_GROUPS * SCAN_CHUNKS * CHUNK
    ns = seq // rows
    npair = D_A // LANES
    nblocks = batch * npair * ns

    def place(blk_id, rev):
        s = blk_id % ns
        p = (blk_id // ns) % npair
        b = blk_id // (ns * npair)
        return (b * ns + (ns - 1 - s if rev else s), p)

    cur = lambda g: jnp.minimum(g, nblocks - 1)
    prev = lambda g: jnp.maximum(g - 1, 0)
    blk = (rows, LANES)
    fspec = pl.BlockSpec(blk, lambda g: place(cur(g), False))
    bspec = pl.BlockSpec(blk, lambda g: place(cur(g), True))
    fdir = pl.BlockSpec((None,) + blk, lambda g: (0,) + place(cur(g), False))
    bdir = pl.BlockSpec((None,) + blk, lambda g: (1,) + place(cur(g), True))
    out = jax.ShapeDtypeStruct((m, D_A), BF16)
    nchain = 2 * SCAN_CHUNKS
    terms = lambda rws, dt: pltpu.VMEM((2, SCAN_GROUPS, nchain, rws, LANES), dt)
    return pl.pallas_call(
        functools.partial(_scan_body, steps_per_seq=ns),
        grid=(nblocks + 1,),
        in_specs=[fspec, fspec, fspec, fspec, fdir, fdir, bspec, bspec, bspec, bspec, bdir, bdir,
                  pl.BlockSpec((1, LANES), lambda g: (0, place(cur(g), False)[1]))],
        out_specs=[pl.BlockSpec(blk, lambda g: place(prev(g), False)),
                   pl.BlockSpec(blk, lambda g: place(prev(g), True))],
        out_shape=[out, out],
        scratch_shapes=[pltpu.VMEM((2, LANES, LANES), F32), terms(2 * CHUNK + LANES, BF16),
                        terms(2 * CHUNK, F32), terms(LANES, F32), terms(LANES, F32)],
        compiler_params=_cparams(("arbitrary",)),
        name="rwkv_scan",
    )(r, k, v, kkn, lw, a, r, k, v, kkn, lw, a, k_a)


POOL_PAD = 8
POOL_COLS = 256
POOL_ROWS = 256


def _pool_body(p_ref, w_ref, inv_ref, z_ref, b_ref, s_ref, o_ref, pad_ref, acc_ref, *, seq):
    cb = pl.program_id(1)
    ncb = pl.num_programs(1)
    pad_ref[0:POOL_PAD, :] = jnp.zeros((POOL_PAD, POOL_COLS), F32)
    pad_ref[POOL_PAD + seq:, :] = jnp.zeros((POOL_PAD, POOL_COLS), F32)
    pad_ref[POOL_PAD:POOL_PAD + seq, :] = p_ref[...].astype(F32)

    @pl.when(cb == 0)
    def _():
        acc_ref[...] = jnp.zeros_like(acc_ref)

    lane = lax.broadcasted_iota(jnp.int32, (1, POOL_COLS), 1) + cb * POOL_COLS
    group = lane // POOL_CH
    w = w_ref[...]

    def rows(i, carry):
        base = pl.multiple_of(i * POOL_ROWS, POOL_ROWS)
        nwin = POOL_ROWS + 2 * POOL_PAD
        win = pad_ref[pl.ds(base, nwin), :]
        pair = win + pltpu.roll(win, nwin - 1, 0)
        at = lambda a, d: a[POOL_PAD + d:POOL_PAD + d + POOL_ROWS]
        x = at(win, 0)
        s1 = x + at(win, -1) + at(win, 1)
        s2 = s1 + at(win, -2) + at(win, 2)
        s4 = s2 + at(pair, -4) + at(pair, 3)
        s8 = s4 + at(pair, -8) + at(pair, -6) + at(pair, 5) + at(pair, 7)
        total = jnp.where(group == 0, s1, jnp.where(group == 1, s2, jnp.where(group == 2, s4, s8)))
        mean = total * inv_ref[pl.ds(base, POOL_ROWS), :]
        acc_ref[pl.ds(base, POOL_ROWS), :] += _bdot(mean - x, w)
        return carry

    lax.fori_loop(0, seq // POOL_ROWS, rows, 0)

    @pl.when(cb == ncb - 1)
    def _():
        o_ref[...] = ((acc_ref[...] + b_ref[...]) * s_ref[...] * z_ref[...].astype(F32)).astype(o_ref.dtype)


def _pool_inv_count(seq):
    half = jnp.repeat(jnp.array(POOL_HALF, jnp.int32), POOL_CH)[None, :]
    t = jnp.arange(seq, dtype=jnp.int32)[:, None]
    count = jnp.minimum(t + half, seq - 1) - jnp.maximum(t - half, 0) + 1
    return 1.0 / count.astype(F32)


def _pool(p, zb, w_bd, inv_count, b, scale, batch, seq):
    m = p.shape[0]
    ncb = D_B // POOL_COLS
    full = pl.BlockSpec((seq, D_B), lambda bi, cb: (bi, 0))
    vec = pl.BlockSpec((1, D_B), lambda bi, cb: (0, 0))
    return pl.pallas_call(
        functools.partial(_pool_body, seq=seq),
        grid=(batch, ncb),
        in_specs=[pl.BlockSpec((seq, POOL_COLS), lambda bi, cb: (bi, cb)),
                  pl.BlockSpec((POOL_COLS, D_B), lambda bi, cb: (cb, 0)),
                  pl.BlockSpec((seq, POOL_COLS), lambda bi, cb: (0, cb)),
                  full, vec, vec],
        out_specs=full,
        out_shape=jax.ShapeDtypeStruct((m, D_B), BF16),
        scratch_shapes=[pltpu.VMEM((seq + 2 * POOL_PAD, POOL_COLS), F32), pltpu.VMEM((seq, D_B), F32)],
        compiler_params=_cparams(("parallel", "arbitrary")),
        name="pool_mixer",
    )(p, w_bd, inv_count, zb, b, scale)


def _rope_matrix():
    half = HEAD_DIM // 2
    src = lax.broadcasted_iota(jnp.int32, (LANES, LANES), 0)
    dst = lax.broadcasted_iota(jnp.int32, (LANES, LANES), 1)
    low = (dst % HEAD_DIM) < half
    return jnp.where(low & (src == dst + half), -1.0, jnp.where(~low & (src == dst - half), 1.0, 0.0)).astype(BF16)


def _rope(t, cos, sin, rot_matrix):
    return t * cos + _split_dot(t, rot_matrix, 2) * sin


ATTN_SIDE = 64
ATTN_TILES = 4
COMBINE_ROWS = 256


def _attn_body(q0, q1, q2, k0, k1, k2, v0, v1, v2, cos_ref, sin_ref, zc_ref, o_ref,
               qp_ref, kp_ref, vp_ref, op_ref, lp_ref, on_ref, ln_ref, *, seq):
    q_refs = (q0, q1, q2)
    k_refs = (k0, k1, k2)
    v_refs = (v0, v1, v2)
    scale = HEAD_DIM ** -0.5
    rot_matrix = _rope_matrix()
    for g, (half_span, dil) in enumerate(ATTN_GROUPS):
        cls = seq // dil
        for r in range(dil):
            rows = pl.ds(r, cls, stride=dil) if dil > 1 else pl.ds(0, seq)
            dst = pl.ds(r * cls, cls)
            cos = cos_ref[rows, :]
            sin = sin_ref[rows, :]
            qp_ref[g, dst, :] = (_rope(q_refs[g][rows, :], cos, sin, rot_matrix) * scale).astype(BF16)
            kp_ref[g, dst, :] = _rope(k_refs[g][rows, :], cos, sin, rot_matrix).astype(BF16)
            vp_ref[g, dst, :] = v_refs[g][rows, :].astype(BF16)

    nt = ATTN_TILES
    first = lax.broadcasted_iota(jnp.int32, (nt, Q_TILE, LANES), 2) < HEAD_DIM
    for g, (half_span, dil) in enumerate(ATTN_GROUPS):
        cls = seq // dil
        klen = min(cls, Q_TILE + 2 * ATTN_SIDE)
        rel = (lax.broadcasted_iota(jnp.int32, (Q_TILE, klen), 1)
               - lax.broadcasted_iota(jnp.int32, (Q_TILE, klen), 0))

        def tiles(it, carry, g=g, cls=cls, klen=klen, rel=rel):
            row0 = pl.multiple_of(it * (nt * Q_TILE), nt * Q_TILE)
            keys, vals, valid = [], [], []
            for j in range(nt):
                qrow = row0 + j * Q_TILE
                base = (qrow // cls) * cls
                krow = pl.multiple_of(base + jnp.clip(qrow - base - ATTN_SIDE, 0, cls - klen), ATTN_SIDE)
                keys.append(kp_ref[g, pl.ds(krow, klen), :])
                vals.append(vp_ref[g, pl.ds(krow, klen), :])
                valid.append(jnp.abs(rel + (krow - qrow)) <= ATTN_SIDE)
            q = qp_ref[g, pl.ds(row0, nt * Q_TILE), :].reshape(nt, Q_TILE, LANES)
            zero = jnp.zeros_like(q)
            qm = jnp.concatenate([jnp.where(first, q, zero), jnp.where(first, zero, q)], axis=0)
            keys = jnp.stack(keys + keys)
            vals = jnp.stack(vals + vals)
            valid = jnp.stack(valid + valid)
            s = jnp.where(valid, _bmm_nt(qm, keys), NEG_BIG)
            mx = jnp.max(s, axis=2, keepdims=True)
            e = jnp.exp(s - mx)
            den = jnp.sum(e, axis=2, keepdims=True)
            o = _bmm(e, vals) / den
            lse = jnp.broadcast_to(mx + jnp.log(den), o.shape)
            rows = pl.ds(row0, nt * Q_TILE)
            op_ref[g, rows, :] = jnp.where(first, o[:nt], o[nt:]).reshape(nt * Q_TILE, LANES)
            lp_ref[g, rows, :] = jnp.where(first, lse[:nt], lse[nt:]).reshape(nt * Q_TILE, LANES)
            return carry

        lax.fori_loop(0, seq // (nt * Q_TILE), tiles, 0)
        if dil > 1:
            for r in range(dil):
                src = pl.ds(r * cls, cls)
                on_ref[g - 1, pl.ds(r, cls, stride=dil), :] = op_ref[g, src, :]
                ln_ref[g - 1, pl.ds(r, cls, stride=dil), :] = lp_ref[g, src, :]

    def combine(i, carry):
        rows = pl.ds(pl.multiple_of(i * COMBINE_ROWS, COMBINE_ROWS), COMBINE_ROWS)
        ls = [lp_ref[0, rows, :], ln_ref[0, rows, :], ln_ref[1, rows, :]]
        os_ = [op_ref[0, rows, :], on_ref[0, rows, :], on_ref[1, rows, :]]
        top = jnp.maximum(jnp.maximum(ls[0], ls[1]), ls[2])
        ws = [jnp.exp(l - top) for l in ls]
        tot = ws[0] + ws[1] + ws[2]
        out = (ws[0] * os_[0] + ws[1] * os_[1] + ws[2] * os_[2]) / tot * zc_ref[rows, :].astype(F32)
        o_ref[rows, :] = out.astype(o_ref.dtype)
        return carry

    lax.fori_loop(0, seq // COMBINE_ROWS, combine, 0)


def _attention(qkv, zc, cos, sin, batch, seq):
    m = qkv.shape[0]
    ncol = D_C // LANES
    per_group = D_C_OUT // LANES
    ngroup = len(ATTN_GROUPS)
    spec = lambda sec, g: pl.BlockSpec((seq, LANES), lambda b, jp: (b, sec * ncol + g * per_group + jp))
    tab = pl.BlockSpec((seq, LANES), lambda b, jp: (0, 0))
    oz = pl.BlockSpec((seq, LANES), lambda b, jp: (b, jp))
    perm_bf = pltpu.VMEM((ngroup, seq, LANES), BF16)
    perm_f32 = pltpu.VMEM((ngroup, seq, LANES), F32)
    nat_f32 = pltpu.VMEM((ngroup - 1, seq, LANES), F32)
    return pl.pallas_call(
        functools.partial(_attn_body, seq=seq),
        grid=(batch, per_group),
        in_specs=[spec(0, 0), spec(0, 1), spec(0, 2), spec(1, 0), spec(1, 1), spec(1, 2),
                  spec(2, 0), spec(2, 1), spec(2, 2), tab, tab, oz],
        out_specs=oz,
        out_shape=jax.ShapeDtypeStruct((m, D_C_OUT), BF16),
        scratch_shapes=[perm_bf, perm_bf, perm_bf, perm_f32, perm_f32, nat_f32, nat_f32],
        compiler_params=_cparams(("parallel", "parallel")),
        name="dilated_attention",
    )(qkv, qkv, qkv, qkv, qkv, qkv, qkv, qkv, qkv, cos, sin, zc)


def _merge_body(x_ref, yf_ref, yb_ref, bonus_ref, za_ref, ob_ref, oc_ref, g_ref,
                gng_ref, gnb_ref, pa_ref, pb_ref, pc_ref, wo_ref, lng_ref, lnb_ref, o_ref, ob16_ref):
    ones = _head_ones(LANES)
    y = yf_ref[...].astype(F32) + yb_ref[...].astype(F32)
    parts = []
    for cb in range(D_A // LANES):
        sl = slice(cb * LANES, (cb + 1) * LANES)
        yc = y[:, sl]
        mu = _bdot(yc, ones) * (1.0 / HEAD_DIM)
        d = yc - mu
        var = _bdot(d * d, ones) * (1.0 / HEAD_DIM)
        parts.append(d * lax.rsqrt(var + RWKV_GN_EPS))
    yn = jnp.concatenate(parts, axis=1)
    o_a = (yn * gng_ref[...] + gnb_ref[...] + bonus_ref[...].astype(F32)) * za_ref[...].astype(F32)
    merged = (g_ref[:, 0:D_MODEL].astype(F32) * _bdot(o_a, pa_ref[...])
              + g_ref[:, D_MODEL:2 * D_MODEL].astype(F32) * _bdot(ob_ref[...], pb_ref[...])
              + g_ref[:, 2 * D_MODEL:3 * D_MODEL].astype(F32) * _bdot(oc_ref[...], pc_ref[...]))
    t = DEEPNORM_ALPHA * x_ref[...] + _bdot(merged, wo_ref[...])
    mean = jnp.mean(t, axis=1, keepdims=True)
    d = t - mean
    var = jnp.mean(d * d, axis=1, keepdims=True)
    out = d * lax.rsqrt(var + LN_EPS) * lng_ref[...] + lnb_ref[...]
    o_ref[...] = out
    ob16_ref[...] = out.astype(BF16)


def _merge(x, yf, yb, bonus, za, ob, oc, gates, gn_g, gn_b, pa, pb, pc, wo, ln_g, ln_b):
    m = x.shape[0]
    tm = min(256, m)
    rows = lambda n: pl.BlockSpec((tm, n), lambda i: (i, 0))
    full = lambda a: pl.BlockSpec(a.shape, lambda i: (0, 0))
    return pl.pallas_call(
        _merge_body,
        grid=(m // tm,),
        in_specs=[rows(D_MODEL), rows(D_A), rows(D_A), rows(D_A), rows(D_A), rows(D_B), rows(D_C_OUT),
                  rows(3 * D_MODEL), full(gn_g), full(gn_b), full(pa), full(pb), full(pc), full(wo),
                  full(ln_g), full(ln_b)],
        out_specs=[rows(D_MODEL), rows(D_MODEL)],
        out_shape=[jax.ShapeDtypeStruct((m, D_MODEL), F32), jax.ShapeDtypeStruct((m, D_MODEL), BF16)],
        compiler_params=_cparams(("parallel",)),
        name="merge_norm",
    )(x, yf, yb, bonus, za, ob, oc, gates, gn_g, gn_b, pa, pb, pc, wo, ln_g, ln_b)


def _rope_tables(seq):
    inv = jnp.power(ROPE_THETA, -jnp.arange(0, HEAD_DIM, 2, dtype=F32) / HEAD_DIM)
    ang = jnp.arange(seq, dtype=F32)[:, None] * inv[None, :]
    ang = jnp.concatenate([ang, ang, ang, ang], axis=-1)
    return jnp.cos(ang), jnp.sin(ang)


def _pad_lora(w_up):
    z = jnp.zeros_like(w_up[0])
    return jnp.stack([jnp.concatenate([w_up[0], z], axis=0), jnp.concatenate([z, w_up[1]], axis=0)])


def _block_diag(w_g):
    g, c, _ = w_g.shape
    eye = jnp.eye(g, dtype=w_g.dtype)
    return (eye[:, None, :, None] * w_g[:, :, None, :]).reshape(g * c, g * c)


def kernel(x, w_in, b_in, rwkv_mu, rwkv_w0, rwkv_w_up, rwkv_a0, rwkv_a_up, rwkv_k_k, rwkv_k_a, rwkv_r_k, rwkv_gn_g, rwkv_gn_b, pool_w, pool_b, pool_scale, proj_a, proj_b, proj_c, w_out, ln_g, ln_b):
    batch, seq, _ = x.shape
    m = batch * seq
    cos, sin = _rope_tables(seq)
    inv_count = _pool_inv_count(seq)
    row = lambda t: t.reshape(1, -1)
    bounds = [0]
    for width in (N_SHIFT, D_A, D_B, D_B, 3 * D_C, D_C_OUT, 3 * D_MODEL):
        bounds.append(bounds[-1] + width)
    acts = (None, "silu", None, "silu", None, "silu", "sigmoid")
    dtypes = (F32, BF16, BF16, BF16, F32, BF16, BF16)

    xf = x.reshape(m, D_MODEL)
    xb = xf.astype(BF16)
    for l in range(DEPTH):
        sec = [_linear(xb, w_in, b_in, l, lo, hi - lo, act, dt)
               for lo, hi, act, dt in zip(bounds[:-1], bounds[1:], acts, dtypes)]
        u, za, p, zb, qkv, zc, gates = sec

        r, k, v, kkn, lw, a, bonus = _rwkv_prep(
            u, rwkv_mu[l], rwkv_w0[l], rwkv_a0[l],
            _pad_lora(rwkv_w_up[l]).astype(BF16), _pad_lora(rwkv_a_up[l]).astype(BF16),
            row(rwkv_k_k[l]), row(rwkv_r_k[l]), seq)
        yf, yb = _rwkv_scan(r, k, v, kkn, lw, a, row(rwkv_k_a[l]), batch, seq)
        ob = _pool(p, zb, _block_diag(pool_w[l]).astype(BF16), inv_count, row(pool_b[l]), row(pool_scale[l]),
                   batch, seq)
        oc = _attention(qkv, zc, cos, sin, batch, seq)
        xf, xb = _merge(xf, yf, yb, bonus, za, ob, oc, gates, row(rwkv_gn_g[l]), row(rwkv_gn_b[l]),
                        proj_a[l].astype(BF16), proj_b[l].astype(BF16), proj_c[l].astype(BF16),
                        w_out[l].astype(BF16), row(ln_g[l]), row(ln_b[l]))
    return xf.reshape(batch, seq, D_MODEL)
```

```python
import functools
import math

import jax
import jax.numpy as jnp
from jax import lax
from jax.experimental import pallas as pl
from jax.experimental.pallas import tpu as pltpu

F32 = jnp.float32
BF16 = jnp.bfloat16

D_MODEL = 1024
DEPTH = 4
HEAD_DIM = 64
LANES = 128
D_A = 1024
LORA = 64
N_SHIFT = 3 * D_A + 4 * LORA
RWKV_GN_EPS = 64e-5
LOG_DECAY_SCALE = -math.exp(-0.5)
POOL_HALF = (1, 2, 4, 8)
POOL_CH = 192
D_B = 768
D_C = 768
D_C_OUT = 256
ATTN_GROUPS = ((64, 1), (256, 4), (1024, 16))
ROPE_THETA = 10000.0
DEEPNORM_ALPHA = (2 * DEPTH) ** 0.25
LN_EPS = 1e-5
CHUNK = 64
Q_TILE = 128
NEG_BIG = -1e30
VMEM_LIMIT = 52 * 1024 * 1024


def _cparams(sem):
    return pltpu.CompilerParams(dimension_semantics=sem, vmem_limit_bytes=VMEM_LIMIT)


def _bdot(a, b):
    return jnp.dot(a.astype(BF16), b.astype(BF16), preferred_element_type=F32)


def _bdot_nt(a, b):
    return lax.dot_general(a.astype(BF16), b.astype(BF16), (((1,), (1,)), ((), ())),
                           preferred_element_type=F32)


def _split_dot(a, b_exact, terms):
    acc = None
    rem = a
    for _ in range(terms):
        piece = rem.astype(BF16)
        part = jnp.dot(piece, b_exact, preferred_element_type=F32)
        acc = part if acc is None else acc + part
        rem = rem - piece.astype(F32)
    return acc


def _head_ones(n):
    r = lax.broadcasted_iota(jnp.int32, (n, n), 0) // HEAD_DIM
    c = lax.broadcasted_iota(jnp.int32, (n, n), 1) // HEAD_DIM
    return (r == c).astype(BF16)


def _sigmoid(x):
    return 0.5 + 0.5 * jnp.tanh(0.5 * x)


def _linear_body(x_ref, w_ref, b_ref, o_ref, wb_ref, *, act, layer):
    @pl.when(pl.program_id(1) == 0)
    def _():
        wb_ref[...] = w_ref[0].astype(BF16)

    acc = jnp.dot(x_ref[...], wb_ref[...], preferred_element_type=F32) + b_ref[layer:layer + 1, :]
    if act == "silu":
        acc = acc * _sigmoid(acc)
    elif act == "sigmoid":
        acc = _sigmoid(acc)
    o_ref[...] = acc.astype(o_ref.dtype)


def _pick_tile(n, cap):
    best = LANES
    for t in range(LANES, cap + 1, LANES):
        if n % t == 0:
            best = t
    return best


def _linear(x, w, b, layer, lo, n, act, out_dtype):
    m, k = x.shape
    tm = min(1024, m)
    tn = _pick_tile(n, 1664)
    col = lambda j: pl.multiple_of(lo + j * tn, LANES)
    return pl.pallas_call(
        functools.partial(_linear_body, act=act, layer=layer),
        grid=(n // tn, m // tm),
        in_specs=[pl.BlockSpec((tm, k), lambda j, i: (i, 0)),
                  pl.BlockSpec((pl.Element(1), pl.Element(k), pl.Element(tn)), lambda j, i: (layer, 0, col(j))),
                  pl.BlockSpec((pl.Element(b.shape[0]), pl.Element(tn)), lambda j, i: (0, col(j)))],
        out_specs=pl.BlockSpec((tm, tn), lambda j, i: (i, j)),
        out_shape=jax.ShapeDtypeStruct((m, n), out_dtype),
        scratch_shapes=[pltpu.VMEM((k, tn), BF16)],
        compiler_params=_cparams(("parallel", "arbitrary")),
        name="linear_" + (act or "id"),
    )(x, w, b)


def _prep_body(u_ref, up_ref, un_ref, mu_ref, w0_ref, a0_ref, wup_ref, aup_ref, kk_ref, rk_ref,
               r_o, k_o, v_o, kkn_o, lw_o, a_o, bonus_o, *, tm, seq):
    i = pl.program_id(0)
    u = u_ref[...]
    row = lax.broadcasted_iota(jnp.int32, (8, 1), 0)
    t0 = (i * tm) % seq
    prev_edge = jnp.where(t0 == 0, 0.0, up_ref[7:8, :])
    next_edge = jnp.where(t0 + tm == seq, 0.0, un_ref[0:1, :])
    prev = pltpu.roll(u, 1, 0)
    prev = jnp.concatenate([jnp.where(row == 0, prev_edge, prev[0:8]), prev[8:]], axis=0)
    nxt = pltpu.roll(u, tm - 1, 0)
    nxt = jnp.concatenate([nxt[:tm - 8], jnp.where(row == 7, next_edge, nxt[tm - 8:])], axis=0)
    mu_prev = mu_ref[0:1, :]
    mu_next = mu_ref[1:2, :]
    u = (1.0 - mu_prev - mu_next) * u + mu_prev * prev + mu_next * nxt

    r = u[:, 0:D_A]
    k = u[:, D_A:2 * D_A]
    v = u[:, 2 * D_A:3 * D_A]
    wd = jnp.tanh(u[:, 3 * D_A:3 * D_A + 2 * LORA])
    ad = u[:, 3 * D_A + 2 * LORA:N_SHIFT]
    r_o[...] = r.astype(r_o.dtype)
    k_o[...] = k.astype(k_o.dtype)
    v_o[...] = v.astype(v_o.dtype)
    for z in range(2):
        warg = w0_ref[z:z + 1, :] + _bdot(wd, wup_ref[z])
        lw_o[z] = LOG_DECAY_SCALE * _sigmoid(warg)
        a_o[z] = _sigmoid(a0_ref[z:z + 1, :] + _bdot(ad, aup_ref[z])).astype(a_o.dtype)

    ones = _head_ones(LANES)
    for cb in range(D_A // LANES):
        sl = slice(cb * LANES, (cb + 1) * LANES)
        kk = k[:, sl] * kk_ref[:, sl]
        ss = _bdot(kk * kk, ones)
        kkn_o[:, sl] = (kk * lax.rsqrt(ss + 1e-12)).astype(kkn_o.dtype)
        rk = _bdot(r[:, sl] * k[:, sl] * rk_ref[:, sl], ones)
        bonus_o[:, sl] = (rk * v[:, sl]).astype(bonus_o.dtype)


def _rwkv_prep(u, mu, w0, a0, wup, aup, k_k, r_k, seq):
    m = u.shape[0]
    tm = min(512, seq)
    nb8 = m // 8
    full = lambda shape: pl.BlockSpec(shape, lambda i: (0,) * len(shape))
    row_spec = pl.BlockSpec((tm, D_A), lambda i: (i, 0))
    dir_spec = pl.BlockSpec((2, tm, D_A), lambda i: (0, i, 0))
    act = jax.ShapeDtypeStruct((m, D_A), BF16)
    act2 = jax.ShapeDtypeStruct((2, m, D_A), BF16)
    logw = jax.ShapeDtypeStruct((2, m, D_A), F32)
    return pl.pallas_call(
        functools.partial(_prep_body, tm=tm, seq=seq),
        grid=(m // tm,),
        in_specs=[pl.BlockSpec((tm, N_SHIFT), lambda i: (i, 0)),
                  pl.BlockSpec((8, N_SHIFT), lambda i: (jnp.maximum(i * (tm // 8) - 1, 0), 0)),
                  pl.BlockSpec((8, N_SHIFT), lambda i: (jnp.minimum((i + 1) * (tm // 8), nb8 - 1), 0)),
                  full((2, N_SHIFT)), full((2, D_A)), full((2, D_A)),
                  full((2, LANES, D_A)), full((2, LANES, D_A)), full((1, D_A)), full((1, D_A))],
        out_specs=[row_spec, row_spec, row_spec, row_spec, dir_spec, dir_spec, row_spec],
        out_shape=[act, act, act, act, logw, act2, act],
        compiler_params=_cparams(("parallel",)),
        name="rwkv_prep",
    )(u, u, u, mu, w0, a0, wup, aup, k_k, r_k)


SCAN_CHUNKS = 8
SCAN_GROUPS = 2
SCAN_STAGGER = 11
SCAN_SPLIT = 2


def _bmm(a, b):
    return lax.dot_general(a.astype(BF16), b.astype(BF16), (((2,), (1,)), ((0,), (0,))),
                           preferred_element_type=F32)


def _bmm_nt(a, b):
    return lax.dot_general(a.astype(BF16), b.astype(BF16), (((2,), (2,)), ((0,), (0,))),
                           preferred_element_type=F32)


def _stack_heads(t):
    lane = lax.broadcasted_iota(jnp.int32, t.shape, 2)
    first = lane < HEAD_DIM
    return jnp.concatenate([jnp.where(first, t, 0.0), jnp.where(first, 0.0, t)], axis=1)


def _chunk_terms(load, k_a, nfwd, finish):
    r, k, v, kk, lw, a = load()
    n, c, _ = r.shape
    n2 = 2 * c
    ri = lax.broadcasted_iota(jnp.int32, (c, 3 * c), 0)
    ci = lax.broadcasted_iota(jnp.int32, (c, 3 * c), 1) % c
    rr = lax.broadcasted_iota(jnp.int32, (n2, n2), 0)
    cc = lax.broadcasted_iota(jnp.int32, (n2, n2), 1)
    same = (rr // c) == (cc // c)
    diag = (rr == cc)[None]
    before = (same & (cc % c < rr % c))[None]
    after = (same & (cc % c > rr % c))[None]

    def scores(lo, hi, done):
        rev = lo >= nfwd
        sl = slice(lo, hi)
        pieces = []
        rem = lw[sl]
        for _ in range(3):
            pieces.append(rem.astype(BF16))
            rem = rem - pieces[-1].astype(F32)
        pieces = jnp.concatenate(pieces, axis=1)
        tri = ((ci >= ri) if rev else (ci <= ri)).astype(BF16)
        wide = jnp.dot(tri, jnp.concatenate([pieces[i] for i in range(hi - lo)], axis=1),
                       preferred_element_type=F32)
        yield
        cum = jnp.stack([wide[:, i * LANES:(i + 1) * LANES] for i in range(hi - lo)])
        total = cum[:, 0:1, :] if rev else cum[:, c - 1:c, :]
        kd = k[sl] * (1.0 + (a[sl] - 1.0) * k_a)
        bb = kk[sl] * a[sl]
        e_neg = jnp.exp(-cum)
        bt = bb * e_neg
        kt = kd * e_neg
        xn = _stack_heads(-kk[sl] * jnp.exp(cum - lw[sl])).astype(BF16)
        xr = _stack_heads(r[sl] * jnp.exp(cum))
        aa = _bmm_nt(jnp.concatenate([xn, xr.astype(BF16)], axis=1), jnp.concatenate([bt, kt], axis=1))
        yield
        low = lax.broadcasted_iota(jnp.int32, (1, n2), 1) < c
        strict = (after if rev else before)[0]
        causal = strict | diag[0]
        masked = []
        for i in range(hi - lo):
            swapped = pltpu.roll(aa[i], c, 1)
            ab = jnp.where(low, aa[i], swapped)
            ak = jnp.where(low, swapped, aa[i])
            masked.append((jnp.where(strict, ab[:n2], 0.0), jnp.where(strict, ak[:n2], 0.0).astype(BF16),
                           jnp.where(causal, ab[n2:], 0.0).astype(BF16),
                           jnp.where(causal, ak[n2:], 0.0).astype(BF16)))
        a_ab, a_ak, a_rb, a_rk = (jnp.stack([m[j] for m in masked]) for j in range(4))
        apow = _bmm(a_ab, a_ab)
        yield
        e_rem = jnp.exp(total - cum)
        vd = _stack_heads(v[sl]).astype(BF16)
        b2d = _stack_heads(bb * e_rem)
        k2d = _stack_heads(kd * e_rem)
        done.append((xn, xr, a_ab, a_ak, a_rb, a_rk, apow, vd, b2d, k2d, total))

    sub = n // SCAN_SPLIT
    parts = []
    waiting = [scores(lo, lo + sub, parts) for lo in range(0, n, sub)]
    running = []
    while waiting or running:
        if waiting:
            running.append(waiting.pop(0))
        running = [gen for gen in running if next(gen, True) is None]
        yield
    xn, xr, a_ab, a_ak, a_rb, a_rk, apow, vd, b2d, k2d, total = (
        jnp.concatenate([p[i] for p in parts], axis=0) for i in range(len(parts[0])))

    tinv = jnp.where(diag, 1.0, a_ab)
    for step in range(max(1, (c - 1).bit_length()) - 2):
        both = _bmm(apow, jnp.concatenate([apow, tinv], axis=2))
        apow = both[:, :, :n2]
        tinv = tinv + both[:, :, n2:]
        yield
    tinv = tinv + _bmm(apow, tinv)

    rhs_q = _bmm(a_ak, vd).astype(BF16)
    pq = _bmm(tinv, jnp.concatenate([xn, rhs_q], axis=2)).astype(BF16)
    yield
    b2t = jnp.swapaxes(b2d, 1, 2).astype(BF16)
    k2t = jnp.swapaxes(k2d, 1, 2).astype(BF16)
    lhs = jnp.concatenate([jnp.concatenate([a_rb, a_rk], axis=2), jnp.concatenate([b2t, k2t], axis=2)], axis=1)
    rhs = jnp.concatenate([pq, jnp.concatenate([jnp.zeros_like(vd), vd], axis=2)], axis=1)
    out = _bmm(lhs, rhs)
    r2 = xr + out[:, :n2, :LANES]
    y0 = out[:, :n2, LANES:]
    gm = out[:, n2:, :LANES]
    h0 = out[:, n2:, LANES:]
    gcol = jnp.sum(jnp.where(diag, jnp.exp(total), 0.0), axis=2, keepdims=True)
    finish(jnp.concatenate([r2, gm], axis=1), y0, gcol, h0)


def _scan_body(rf, kf, vf, kkf, lwf, af, rb, kb, vb, kkb, lwb, ab, ka_ref, yf_o, yb_o,
               st_ref, rg_ref, y0_ref, gc_ref, h0_ref, *, steps_per_seq):
    g = pl.program_id(0)

    @pl.when(g == 0)
    def _():
        st_ref[...] = jnp.zeros_like(st_ref)
        rg_ref[...] = jnp.zeros_like(rg_ref)
        y0_ref[...] = jnp.zeros_like(y0_ref)
        gc_ref[...] = jnp.zeros_like(gc_ref)
        h0_ref[...] = jnp.zeros_like(h0_ref)

    nch = SCAN_CHUNKS
    rows = nch * CHUNK
    n2 = 2 * CHUNK
    rd = (g + 1) % 2
    wr = g % 2
    fresh = (g % steps_per_seq) == (1 % steps_per_seq)
    state = [jnp.where(fresh, 0.0, st_ref[0]), jnp.where(fresh, 0.0, st_ref[1])]
    pending = [(grp, j) for grp in range(SCAN_GROUPS) for j in range(nch)]

    def chain_step():
        if not pending:
            return
        grp, j = pending.pop(0)
        f0 = grp * rows
        b0 = (SCAN_GROUPS - 1 - grp) * rows
        for d, jc, row0 in ((0, j, f0 + j * CHUNK), (1, 2 * nch - 1 - j, b0 + (nch - 1 - j) * CHUNK)):
            out = jnp.dot(rg_ref[rd, grp, jc], state[d].astype(BF16), preferred_element_type=F32)
            y = out[:n2] + y0_ref[rd, grp, jc]
            state[d] = state[d] * gc_ref[rd, grp, jc] + out[n2:] + h0_ref[rd, grp, jc]
            o_ref = yf_o if d == 0 else yb_o
            o_ref[row0:row0 + CHUNK, :] = (y[:CHUNK] + y[CHUNK:]).astype(o_ref.dtype)

    def group_terms(grp):
        f0 = grp * rows
        b0 = (SCAN_GROUPS - 1 - grp) * rows

        def both(f, b):
            return jnp.concatenate([f[f0:f0 + rows, :].astype(F32).reshape(nch, CHUNK, LANES),
                                    b[b0:b0 + rows, :].astype(F32).reshape(nch, CHUNK, LANES)], axis=0)

        def load():
            return (both(rf, rb), both(kf, kb), both(vf, vb), both(kkf, kkb), both(lwf, lwb), both(af, ab))

        def finish(rg, y0, gcol, h0):
            rg_ref[wr, grp] = rg.astype(BF16)
            y0_ref[wr, grp] = y0
            gc_ref[wr, grp] = jnp.broadcast_to(gcol, h0.shape)
            h0_ref[wr, grp] = h0

        return _chunk_terms(load, ka_ref[...], nch, finish)

    waiting = list(range(SCAN_GROUPS))
    running = []
    tick = 0
    while waiting or running or pending:
        if waiting and tick % SCAN_STAGGER == 0:
            running.append(group_terms(waiting.pop(0)))
        running = [gen for gen in running if next(gen, True) is None]
        chain_step()
        tick += 1
    st_ref[0] = state[0]
    st_ref[1] = state[1]


def _rwkv_scan(r, k, v, kkn, lw, a, k_a, batch, seq):
    m = r.shape[0]
    rows = SCAN_GROUPS * SCAN_CHUNKS * CHUNK
    ns = seq // rows
    npair = D_A // LANES
    nblocks = batch * npair * ns

    def place(blk_id, rev):
        s = blk_id % ns
        p = (blk_id // ns) % npair
        b = blk_id // (ns * npair)
        return (b * ns + (ns - 1 - s if rev else s), p)

    cur = lambda g: jnp.minimum(g, nblocks - 1)
    prev = lambda g: jnp.maximum(g - 1, 0)
    blk = (rows, LANES)
    fspec = pl.BlockSpec(blk, lambda g: place(cur(g), False))
    bspec = pl.BlockSpec(blk, lambda g: place(cur(g), True))
    fdir = pl.BlockSpec((None,) + blk, lambda g: (0,) + place(cur(g), False))
    bdir = pl.BlockSpec((None,) + blk, lambda g: (1,) + place(cur(g), True))
    out = jax.ShapeDtypeStruct((m, D_A), BF16)
    nchain = 2 * SCAN_CHUNKS
    terms = lambda rws, dt: pltpu.VMEM((2, SCAN_GROUPS, nchain, rws, LANES), dt)
    return pl.pallas_call(
        functools.partial(_scan_body, steps_per_seq=ns),
        grid=(nblocks + 1,),
        in_specs=[fspec, fspec, fspec, fspec, fdir, fdir, bspec, bspec, bspec, bspec, bdir, bdir,
                  pl.BlockSpec((1, LANES), lambda g: (0, place(cur(g), False)[1]))],
        out_specs=[pl.BlockSpec(blk, lambda g: place(prev(g), False)),
                   pl.BlockSpec(blk, lambda g: place(prev(g), True))],
        out_shape=[out, out],
        scratch_shapes=[pltpu.VMEM((2, LANES, LANES), F32), terms(2 * CHUNK + LANES, BF16),
                        terms(2 * CHUNK, F32), terms(LANES, F32), terms(LANES, F32)],
        compiler_params=_cparams(("arbitrary",)),
        name="rwkv_scan",
    )(r, k, v, kkn, lw, a, r, k, v, kkn, lw, a, k_a)


POOL_PAD = 8
POOL_COLS = 256
POOL_ROWS = 256


def _pool_body(p_ref, w_ref, inv_ref, z_ref, b_ref, s_ref, o_ref, pad_ref, acc_ref, *, seq):
    cb = pl.program_id(1)
    ncb = pl.num_programs(1)
    pad_ref[0:POOL_PAD, :] = jnp.zeros((POOL_PAD, POOL_COLS), F32)
    pad_ref[POOL_PAD + seq:, :] = jnp.zeros((POOL_PAD, POOL_COLS), F32)
    pad_ref[POOL_PAD:POOL_PAD + seq, :] = p_ref[...].astype(F32)

    @pl.when(cb == 0)
    def _():
        acc_ref[...] = jnp.zeros_like(acc_ref)

    lane = lax.broadcasted_iota(jnp.int32, (1, POOL_COLS), 1) + cb * POOL_COLS
    group = lane // POOL_CH
    w = w_ref[...]

    def rows(i, carry):
        base = pl.multiple_of(i * POOL_ROWS, POOL_ROWS)
        nwin = POOL_ROWS + 2 * POOL_PAD
        win = pad_ref[pl.ds(base, nwin), :]
        pair = win + pltpu.roll(win, nwin - 1, 0)
        at = lambda a, d: a[POOL_PAD + d:POOL_PAD + d + POOL_ROWS]
        x = at(win, 0)
        s1 = x + at(win, -1) + at(win, 1)
        s2 = s1 + at(win, -2) + at(win, 2)
        s4 = s2 + at(pair, -4) + at(pair, 3)
        s8 = s4 + at(pair, -8) + at(pair, -6) + at(pair, 5) + at(pair, 7)
        total = jnp.where(group == 0, s1, jnp.where(group == 1, s2, jnp.where(group == 2, s4, s8)))
        mean = total * inv_ref[pl.ds(base, POOL_ROWS), :]
        acc_ref[pl.ds(base, POOL_ROWS), :] += _bdot(mean - x, w)
        return carry

    lax.fori_loop(0, seq // POOL_ROWS, rows, 0)

    @pl.when(cb == ncb - 1)
    def _():
        o_ref[...] = ((acc_ref[...] + b_ref[...]) * s_ref[...] * z_ref[...].astype(F32)).astype(o_ref.dtype)


def _pool_inv_count(seq):
    half = jnp.repeat(jnp.array(POOL_HALF, jnp.int32), POOL_CH)[None, :]
    t = jnp.arange(seq, dtype=jnp.int32)[:, None]
    count = jnp.minimum(t + half, seq - 1) - jnp.maximum(t - half, 0) + 1
    return 1.0 / count.astype(F32)


def _pool(p, zb, w_bd, inv_count, b, scale, batch, seq):
    m = p.shape[0]
    ncb = D_B // POOL_COLS
    full = pl.BlockSpec((seq, D_B), lambda bi, cb: (bi, 0))
    vec = pl.BlockSpec((1, D_B), lambda bi, cb: (0, 0))
    return pl.pallas_call(
        functools.partial(_pool_body, seq=seq),
        grid=(batch, ncb),
        in_specs=[pl.BlockSpec((seq, POOL_COLS), lambda bi, cb: (bi, cb)),
                  pl.BlockSpec((POOL_COLS, D_B), lambda bi, cb: (cb, 0)),
                  pl.BlockSpec((seq, POOL_COLS), lambda bi, cb: (0, cb)),
                  full, vec, vec],
        out_specs=full,
        out_shape=jax.ShapeDtypeStruct((m, D_B), BF16),
        scratch_shapes=[pltpu.VMEM((seq + 2 * POOL_PAD, POOL_COLS), F32), pltpu.VMEM((seq, D_B), F32)],
        compiler_params=_cparams(("parallel", "arbitrary")),
        name="pool_mixer",
    )(p, w_bd, inv_count, zb, b, scale)


def _rope_matrix():
    half = HEAD_DIM // 2
    src = lax.broadcasted_iota(jnp.int32, (LANES, LANES), 0)
    dst = lax.broadcasted_iota(jnp.int32, (LANES, LANES), 1)
    low = (dst % HEAD_DIM) < half
    return jnp.where(low & (src == dst + half), -1.0, jnp.where(~low & (src == dst - half), 1.0, 0.0)).astype(BF16)


def _rope(t, cos, sin, rot_matrix):
    return t * cos + _split_dot(t, rot_matrix, 2) * sin


ATTN_SIDE = 64
ATTN_TILES = 4
COMBINE_ROWS = 256


def _attn_body(q0, q1, q2, k0, k1, k2, v0, v1, v2, cos_ref, sin_ref, zc_ref, o_ref,
               qp_ref, kp_ref, vp_ref, op_ref, lp_ref, on_ref, ln_ref, *, seq):
    q_refs = (q0, q1, q2)
    k_refs = (k0, k1, k2)
    v_refs = (v0, v1, v2)
    scale = HEAD_DIM ** -0.5
    rot_matrix = _rope_matrix()
    for g, (half_span, dil) in enumerate(ATTN_GROUPS):
        cls = seq // dil
        for r in range(dil):
            rows = pl.ds(r, cls, stride=dil) if dil > 1 else pl.ds(0, seq)
            dst = pl.ds(r * cls, cls)
            cos = cos_ref[rows, :]
            sin = sin_ref[rows, :]
            qp_ref[g, dst, :] = (_rope(q_refs[g][rows, :], cos, sin, rot_matrix) * scale).astype(BF16)
            kp_ref[g, dst, :] = _rope(k_refs[g][rows, :], cos, sin, rot_matrix).astype(BF16)
            vp_ref[g, dst, :] = v_refs[g][rows, :].astype(BF16)

    nt = ATTN_TILES
    first = lax.broadcasted_iota(jnp.int32, (nt, Q_TILE, LANES), 2) < HEAD_DIM
    for g, (half_span, dil) in enumerate(ATTN_GROUPS):
        cls = seq // dil
        klen = min(cls, Q_TILE + 2 * ATTN_SIDE)
        rel = (lax.broadcasted_iota(jnp.int32, (Q_TILE, klen), 1)
               - lax.broadcasted_iota(jnp.int32, (Q_TILE, klen), 0))

        def tiles(it, carry, g=g, cls=cls, klen=klen, rel=rel):
            row0 = pl.multiple_of(it * (nt * Q_TILE), nt * Q_TILE)
            keys, vals, valid = [], [], []
            for j in range(nt):
                qrow = row0 + j * Q_TILE
                base = (qrow // cls) * cls
                krow = pl.multiple_of(base + jnp.clip(qrow - base - ATTN_SIDE, 0, cls - klen), ATTN_SIDE)
                keys.append(kp_ref[g, pl.ds(krow, klen), :])
                vals.append(vp_ref[g, pl.ds(krow, klen), :])
                valid.append(jnp.abs(rel + (krow - qrow)) <= ATTN_SIDE)
            q = qp_ref[g, pl.ds(row0, nt * Q_TILE), :].reshape(nt, Q_TILE, LANES)
            zero = jnp.zeros_like(q)
            qm = jnp.concatenate([jnp.where(first, q, zero), jnp.where(first, zero, q)], axis=0)
            keys = jnp.stack(keys + keys)
            vals = jnp.stack(vals + vals)
            valid = jnp.stack(valid + valid)
            s = jnp.where(valid, _bmm_nt(qm, keys), NEG_BIG)
            mx = jnp.max(s, axis=2, keepdims=True)
            e = jnp.exp(s - mx)
            den = jnp.sum(e, axis=2, keepdims=True)
            o = _bmm(e, vals) / den
            lse = jnp.broadcast_to(mx + jnp.log(den), o.shape)
            rows = pl.ds(row0, nt * Q_TILE)
            op_ref[g, rows, :] = jnp.where(first, o[:nt], o[nt:]).reshape(nt * Q_TILE, LANES)
            lp_ref[g, rows, :] = jnp.where(first, lse[:nt], lse[nt:]).reshape(nt * Q_TILE, LANES)
            return carry

        lax.fori_loop(0, seq // (nt * Q_TILE), tiles, 0)
        if dil > 1:
            for r in range(dil):
                src = pl.ds(r * cls, cls)
                on_ref[g - 1, pl.ds(r, cls, stride=dil), :] = op_ref[g, src, :]
                ln_ref[g - 1, pl.ds(r, cls, stride=dil), :] = lp_ref[g, src, :]

    def combine(i, carry):
        rows = pl.ds(pl.multiple_of(i * COMBINE_ROWS, COMBINE_ROWS), COMBINE_ROWS)
        ls = [lp_ref[0, rows, :], ln_ref[0, rows, :], ln_ref[1, rows, :]]
        os_ = [op_ref[0, rows, :], on_ref[0, rows, :], on_ref[1, rows, :]]
        top = jnp.maximum(jnp.maximum(ls[0], ls[1]), ls[2])
        ws = [jnp.exp(l - top) for l in ls]
        tot = ws[0] + ws[1] + ws[2]
        out = (ws[0] * os_[0] + ws[1] * os_[1] + ws[2] * os_[2]) / tot * zc_ref[rows, :].astype(F32)
        o_ref[rows, :] = out.astype(o_ref.dtype)
        return carry

    lax.fori_loop(0, seq // COMBINE_ROWS, combine, 0)


def _attention(qkv, zc, cos, sin, batch, seq):
    m = qkv.shape[0]
    ncol = D_C // LANES
    per_group = D_C_OUT // LANES
    ngroup = len(ATTN_GROUPS)
    spec = lambda sec, g: pl.BlockSpec((seq, LANES), lambda b, jp: (b, sec * ncol + g * per_group + jp))
    tab = pl.BlockSpec((seq, LANES), lambda b, jp: (0, 0))
    oz = pl.BlockSpec((seq, LANES), lambda b, jp: (b, jp))
    perm_bf = pltpu.VMEM((ngroup, seq, LANES), BF16)
    perm_f32 = pltpu.VMEM((ngroup, seq, LANES), F32)
    nat_f32 = pltpu.VMEM((ngroup - 1, seq, LANES), F32)
    return pl.pallas_call(
        functools.partial(_attn_body, seq=seq),
        grid=(batch, per_group),
        in_specs=[spec(0, 0), spec(0, 1), spec(0, 2), spec(1, 0), spec(1, 1), spec(1, 2),
                  spec(2, 0), spec(2, 1), spec(2, 2), tab, tab, oz],
        out_specs=oz,
        out_shape=jax.ShapeDtypeStruct((m, D_C_OUT), BF16),
        scratch_shapes=[perm_bf, perm_bf, perm_bf, perm_f32, perm_f32, nat_f32, nat_f32],
        compiler_params=_cparams(("parallel", "parallel")),
        name="dilated_attention",
    )(qkv, qkv, qkv, qkv, qkv, qkv, qkv, qkv, qkv, cos, sin, zc)


def _merge_body(x_ref, yf_ref, yb_ref, bonus_ref, za_ref, ob_ref, oc_ref, g_ref,
                gng_ref, gnb_ref, pa_ref, pb_ref, pc_ref, wo_ref, lng_ref, lnb_ref, o_ref, ob16_ref):
    ones = _head_ones(LANES)
    y = yf_ref[...].astype(F32) + yb_ref[...].astype(F32)
    parts = []
    for cb in range(D_A // LANES):
        sl = slice(cb * LANES, (cb + 1) * LANES)
        yc = y[:, sl]
        mu = _bdot(yc, ones) * (1.0 / HEAD_DIM)
        d = yc - mu
        var = _bdot(d * d, ones) * (1.0 / HEAD_DIM)
        parts.append(d * lax.rsqrt(var + RWKV_GN_EPS))
    yn = jnp.concatenate(parts, axis=1)
    o_a = (yn * gng_ref[...] + gnb_ref[...] + bonus_ref[...].astype(F32)) * za_ref[...].astype(F32)
    merged = (g_ref[:, 0:D_MODEL].astype(F32) * _bdot(o_a, pa_ref[...])
              + g_ref[:, D_MODEL:2 * D_MODEL].astype(F32) * _bdot(ob_ref[...], pb_ref[...])
              + g_ref[:, 2 * D_MODEL:3 * D_MODEL].astype(F32) * _bdot(oc_ref[...], pc_ref[...]))
    t = DEEPNORM_ALPHA * x_ref[...] + _bdot(merged, wo_ref[...])
    mean = jnp.mean(t, axis=1, keepdims=True)
    d = t - mean
    var = jnp.mean(d * d, axis=1, keepdims=True)
    out = d * lax.rsqrt(var + LN_EPS) * lng_ref[...] + lnb_ref[...]
    o_ref[...] = out
    ob16_ref[...] = out.astype(BF16)


def _merge(x, yf, yb, bonus, za, ob, oc, gates, gn_g, gn_b, pa, pb, pc, wo, ln_g, ln_b):
    m = x.shape[0]
    tm = min(512, m)
    rows = lambda n: pl.BlockSpec((tm, n), lambda i: (i, 0))
    full = lambda a: pl.BlockSpec(a.shape, lambda i: (0, 0))
    return pl.pallas_call(
        _merge_body,
        grid=(m // tm,),
        in_specs=[rows(D_MODEL), rows(D_A), rows(D_A), rows(D_A), rows(D_A), rows(D_B), rows(D_C_OUT),
                  rows(3 * D_MODEL), full(gn_g), full(gn_b), full(pa), full(pb), full(pc), full(wo),
                  full(ln_g), full(ln_b)],
        out_specs=[rows(D_MODEL), rows(D_MODEL)],
        out_shape=[jax.ShapeDtypeStruct((m, D_MODEL), F32), jax.ShapeDtypeStruct((m, D_MODEL), BF16)],
        compiler_params=_cparams(("parallel",)),
        name="merge_norm",
    )(x, yf, yb, bonus, za, ob, oc, gates, gn_g, gn_b, pa, pb, pc, wo, ln_g, ln_b)


def _rope_tables(seq):
    inv = jnp.power(ROPE_THETA, -jnp.arange(0, HEAD_DIM, 2, dtype=F32) / HEAD_DIM)
    ang = jnp.arange(seq, dtype=F32)[:, None] * inv[None, :]
    ang = jnp.concatenate([ang, ang, ang, ang], axis=-1)
    return jnp.cos(ang), jnp.sin(ang)


def _pad_lora(w_up):
    z = jnp.zeros_like(w_up[0])
    return jnp.stack([jnp.concatenate([w_up[0], z], axis=0), jnp.concatenate([z, w_up[1]], axis=0)])


def _block_diag(w_g):
    g, c, _ = w_g.shape
    eye = jnp.eye(g, dtype=w_g.dtype)
    return (eye[:, None, :, None] * w_g[:, :, None, :]).reshape(g * c, g * c)


def kernel(x, w_in, b_in, rwkv_mu, rwkv_w0, rwkv_w_up, rwkv_a0, rwkv_a_up, rwkv_k_k, rwkv_k_a, rwkv_r_k, rwkv_gn_g, rwkv_gn_b, pool_w, pool_b, pool_scale, proj_a, proj_b, proj_c, w_out, ln_g, ln_b):
    batch, seq, _ = x.shape
    m = batch * seq
    cos, sin = _rope_tables(seq)
    inv_count = _pool_inv_count(seq)
    row = lambda t: t.reshape(1, -1)
    bounds = [0]
    for width in (N_SHIFT, D_A, D_B, D_B, 3 * D_C, D_C_OUT, 3 * D_MODEL):
        bounds.append(bounds[-1] + width)
    acts = (None, "silu", None, "silu", None, "silu", "sigmoid")
    dtypes = (F32, BF16, BF16, BF16, F32, BF16, BF16)

    xf = x.reshape(m, D_MODEL)
    xb = xf.astype(BF16)
    for l in range(DEPTH):
        sec = [_linear(xb, w_in, b_in, l, lo, hi - lo, act, dt)
               for lo, hi, act, dt in zip(bounds[:-1], bounds[1:], acts, dtypes)]
        u, za, p, zb, qkv, zc, gates = sec

        r, k, v, kkn, lw, a, bonus = _rwkv_prep(
            u, rwkv_mu[l], rwkv_w0[l], rwkv_a0[l],
            _pad_lora(rwkv_w_up[l]).astype(BF16), _pad_lora(rwkv_a_up[l]).astype(BF16),
            row(rwkv_k_k[l]), row(rwkv_r_k[l]), seq)
        yf, yb = _rwkv_scan(r, k, v, kkn, lw, a, row(rwkv_k_a[l]), batch, seq)
        ob = _pool(p, zb, _block_diag(pool_w[l]).astype(BF16), inv_count, row(pool_b[l]), row(pool_scale[l]),
                   batch, seq)
        oc = _attention(qkv, zc, cos, sin, batch, seq)
        xf, xb = _merge(xf, yf, yb, bonus, za, ob, oc, gates, row(rwkv_gn_g[l]), row(rwkv_gn_b[l]),
                        proj_a[l].astype(BF16), proj_b[l].astype(BF16), proj_c[l].astype(BF16),
                        w_out[l].astype(BF16), row(ln_g[l]), row(ln_b[l]))
    return xf.reshape(batch, seq, D_MODEL)
```

```python
import functools
import math

import jax
import jax.numpy as jnp
from jax import lax
from jax.experimental import pallas as pl
from jax.experimental.pallas import tpu as pltpu

F32 = jnp.float32
BF16 = jnp.bfloat16

D_MODEL = 1024
DEPTH = 4
HEAD_DIM = 64
LANES = 128
D_A = 1024
LORA = 64
N_SHIFT = 3 * D_A + 4 * LORA
RWKV_GN_EPS = 64e-5
LOG_DECAY_SCALE = -math.exp(-0.5)
POOL_HALF = (1, 2, 4, 8)
POOL_CH = 192
D_B = 768
D_C = 768
D_C_OUT = 256
ATTN_GROUPS = ((64, 1), (256, 4), (1024, 16))
ROPE_THETA = 10000.0
DEEPNORM_ALPHA = (2 * DEPTH) ** 0.25
LN_EPS = 1e-5
CHUNK = 64
Q_TILE = 128
NEG_BIG = -1e30
VMEM_LIMIT = 52 * 1024 * 1024


def _cparams(sem):
    return pltpu.CompilerParams(dimension_semantics=sem, vmem_limit_bytes=VMEM_LIMIT)


def _bdot(a, b):
    return jnp.dot(a.astype(BF16), b.astype(BF16), preferred_element_type=F32)


def _bdot_nt(a, b):
    return lax.dot_general(a.astype(BF16), b.astype(BF16), (((1,), (1,)), ((), ())),
                           preferred_element_type=F32)


def _split_dot(a, b_exact, terms):
    acc = None
    rem = a
    for _ in range(terms):
        piece = rem.astype(BF16)
        part = jnp.dot(piece, b_exact, preferred_element_type=F32)
        acc = part if acc is None else acc + part
        rem = rem - piece.astype(F32)
    return acc


def _head_ones(n):
    r = lax.broadcasted_iota(jnp.int32, (n, n), 0) // HEAD_DIM
    c = lax.broadcasted_iota(jnp.int32, (n, n), 1) // HEAD_DIM
    return (r == c).astype(BF16)


def _sigmoid(x):
    return 0.5 + 0.5 * jnp.tanh(0.5 * x)


def _linear_body(x_ref, w_ref, b_ref, o_ref, wb_ref, *, act, layer):
    @pl.when(pl.program_id(1) == 0)
    def _():
        wb_ref[...] = w_ref[0].astype(BF16)

    acc = jnp.dot(x_ref[...], wb_ref[...], preferred_element_type=F32) + b_ref[layer:layer + 1, :]
    if act == "silu":
        acc = acc * _sigmoid(acc)
    elif act == "sigmoid":
        acc = _sigmoid(acc)
    o_ref[...] = acc.astype(o_ref.dtype)


def _pick_tile(n, cap):
    best = LANES
    for t in range(LANES, cap + 1, LANES):
        if n % t == 0:
            best = t
    return best


def _linear(x, w, b, layer, lo, n, act, out_dtype):
    m, k = x.shape
    tm = min(1024, m)
    tn = _pick_tile(n, 1664)
    col = lambda j: pl.multiple_of(lo + j * tn, LANES)
    return pl.pallas_call(
        functools.partial(_linear_body, act=act, layer=layer),
        grid=(n // tn, m // tm),
        in_specs=[pl.BlockSpec((tm, k), lambda j, i: (i, 0)),
                  pl.BlockSpec((pl.Element(1), pl.Element(k), pl.Element(tn)), lambda j, i: (layer, 0, col(j))),
                  pl.BlockSpec((pl.Element(b.shape[0]), pl.Element(tn)), lambda j, i: (0, col(j)))],
        out_specs=pl.BlockSpec((tm, tn), lambda j, i: (i, j)),
        out_shape=jax.ShapeDtypeStruct((m, n), out_dtype),
        scratch_shapes=[pltpu.VMEM((k, tn), BF16)],
        compiler_params=_cparams(("parallel", "arbitrary")),
        name="linear_" + (act or "id"),
    )(x, w, b)


def _prep_body(u_ref, up_ref, un_ref, mu_ref, w0_ref, a0_ref, wup_ref, aup_ref, kk_ref, rk_ref,
               r_o, k_o, v_o, kkn_o, lw_o, a_o, bonus_o, *, tm, seq):
    i = pl.program_id(0)
    u = u_ref[...]
    row = lax.broadcasted_iota(jnp.int32, (8, 1), 0)
    t0 = (i * tm) % seq
    prev_edge = jnp.where(t0 == 0, 0.0, up_ref[7:8, :])
    next_edge = jnp.where(t0 + tm == seq, 0.0, un_ref[0:1, :])
    prev = pltpu.roll(u, 1, 0)
    prev = jnp.concatenate([jnp.where(row == 0, prev_edge, prev[0:8]), prev[8:]], axis=0)
    nxt = pltpu.roll(u, tm - 1, 0)
    nxt = jnp.concatenate([nxt[:tm - 8], jnp.where(row == 7, next_edge, nxt[tm - 8:])], axis=0)
    mu_prev = mu_ref[0:1, :]
    mu_next = mu_ref[1:2, :]
    u = (1.0 - mu_prev - mu_next) * u + mu_prev * prev + mu_next * nxt

    r = u[:, 0:D_A]
    k = u[:, D_A:2 * D_A]
    v = u[:, 2 * D_A:3 * D_A]
    wd = jnp.tanh(u[:, 3 * D_A:3 * D_A + 2 * LORA])
    ad = u[:, 3 * D_A + 2 * LORA:N_SHIFT]
    r_o[...] = r.astype(r_o.dtype)
    k_o[...] = k.astype(k_o.dtype)
    v_o[...] = v.astype(v_o.dtype)
    for z in range(2):
        warg = w0_ref[z:z + 1, :] + _bdot(wd, wup_ref[z])
        lw_o[z] = LOG_DECAY_SCALE * _sigmoid(warg)
        a_o[z] = _sigmoid(a0_ref[z:z + 1, :] + _bdot(ad, aup_ref[z])).astype(a_o.dtype)

    ones = _head_ones(LANES)
    for cb in range(D_A // LANES):
        sl = slice(cb * LANES, (cb + 1) * LANES)
        kk = k[:, sl] * kk_ref[:, sl]
        ss = _bdot(kk * kk, ones)
        kkn_o[:, sl] = (kk * lax.rsqrt(ss + 1e-12)).astype(kkn_o.dtype)
        rk = _bdot(r[:, sl] * k[:, sl] * rk_ref[:, sl], ones)
        bonus_o[:, sl] = (rk * v[:, sl]).astype(bonus_o.dtype)


def _rwkv_prep(u, mu, w0, a0, wup, aup, k_k, r_k, seq):
    m = u.shape[0]
    tm = min(512, seq)
    nb8 = m // 8
    full = lambda shape: pl.BlockSpec(shape, lambda i: (0,) * len(shape))
    row_spec = pl.BlockSpec((tm, D_A), lambda i: (i, 0))
    dir_spec = pl.BlockSpec((2, tm, D_A), lambda i: (0, i, 0))
    act = jax.ShapeDtypeStruct((m, D_A), BF16)
    act2 = jax.ShapeDtypeStruct((2, m, D_A), BF16)
    logw = jax.ShapeDtypeStruct((2, m, D_A), F32)
    return pl.pallas_call(
        functools.partial(_prep_body, tm=tm, seq=seq),
        grid=(m // tm,),
        in_specs=[pl.BlockSpec((tm, N_SHIFT), lambda i: (i, 0)),
                  pl.BlockSpec((8, N_SHIFT), lambda i: (jnp.maximum(i * (tm // 8) - 1, 0), 0)),
                  pl.BlockSpec((8, N_SHIFT), lambda i: (jnp.minimum((i + 1) * (tm // 8), nb8 - 1), 0)),
                  full((2, N_SHIFT)), full((2, D_A)), full((2, D_A)),
                  full((2, LANES, D_A)), full((2, LANES, D_A)), full((1, D_A)), full((1, D_A))],
        out_specs=[row_spec, row_spec, row_spec, row_spec, dir_spec, dir_spec, row_spec],
        out_shape=[act, act, act, act, logw, act2, act],
        compiler_params=_cparams(("parallel",)),
        name="rwkv_prep",
    )(u, u, u, mu, w0, a0, wup, aup, k_k, r_k)


SCAN_CHUNKS = 8
SCAN_GROUPS = 2
SCAN_STAGGER = 11
SCAN_SPLIT = 2


def _bmm(a, b):
    return lax.dot_general(a.astype(BF16), b.astype(BF16), (((2,), (1,)), ((0,), (0,))),
                           preferred_element_type=F32)


def _bmm_nt(a, b):
    return lax.dot_general(a.astype(BF16), b.astype(BF16), (((2,), (2,)), ((0,), (0,))),
                           preferred_element_type=F32)


def _stack_heads(t):
    lane = lax.broadcasted_iota(jnp.int32, t.shape, 2)
    first = lane < HEAD_DIM
    return jnp.concatenate([jnp.where(first, t, 0.0), jnp.where(first, 0.0, t)], axis=1)


def _chunk_terms(load, k_a, nfwd, finish):
    r, k, v, kk, lw, a = load()
    n, c, _ = r.shape
    n2 = 2 * c
    ri = lax.broadcasted_iota(jnp.int32, (c, 3 * c), 0)
    ci = lax.broadcasted_iota(jnp.int32, (c, 3 * c), 1) % c
    rr = lax.broadcasted_iota(jnp.int32, (n2, n2), 0)
    cc = lax.broadcasted_iota(jnp.int32, (n2, n2), 1)
    same = (rr // c) == (cc // c)
    diag = (rr == cc)[None]
    before = (same & (cc % c < rr % c))[None]
    after = (same & (cc % c > rr % c))[None]

    def scores(lo, hi, done):
        rev = lo >= nfwd
        sl = slice(lo, hi)
        pieces = []
        rem = lw[sl]
        for _ in range(3):
            pieces.append(rem.astype(BF16))
            rem = rem - pieces[-1].astype(F32)
        pieces = jnp.concatenate(pieces, axis=1)
        tri = ((ci >= ri) if rev else (ci <= ri)).astype(BF16)
        wide = jnp.dot(tri, jnp.concatenate([pieces[i] for i in range(hi - lo)], axis=1),
                       preferred_element_type=F32)
        yield
        cum = jnp.stack([wide[:, i * LANES:(i + 1) * LANES] for i in range(hi - lo)])
        total = cum[:, 0:1, :] if rev else cum[:, c - 1:c, :]
        kd = k[sl] * (1.0 + (a[sl] - 1.0) * k_a)
        bb = kk[sl] * a[sl]
        e_neg = jnp.exp(-cum)
        bt = bb * e_neg
        kt = kd * e_neg
        xn = _stack_heads(-kk[sl] * jnp.exp(cum - lw[sl])).astype(BF16)
        xr = _stack_heads(r[sl] * jnp.exp(cum))
        aa = _bmm_nt(jnp.concatenate([xn, xr.astype(BF16)], axis=1), jnp.concatenate([bt, kt], axis=1))
        yield
        low = lax.broadcasted_iota(jnp.int32, (1, n2), 1) < c
        strict = (after if rev else before)[0]
        causal = strict | diag[0]
        masked = []
        for i in range(hi - lo):
            swapped = pltpu.roll(aa[i], c, 1)
            ab = jnp.where(low, aa[i], swapped)
            ak = jnp.where(low, swapped, aa[i])
            masked.append((jnp.where(strict, ab[:n2], 0.0), jnp.where(strict, ak[:n2], 0.0).astype(BF16),
                           jnp.where(causal, ab[n2:], 0.0).astype(BF16),
                           jnp.where(causal, ak[n2:], 0.0).astype(BF16)))
        a_ab, a_ak, a_rb, a_rk = (jnp.stack([m[j] for m in masked]) for j in range(4))
        apow = _bmm(a_ab, a_ab)
        yield
        e_rem = jnp.exp(total - cum)
        vd = _stack_heads(v[sl]).astype(BF16)
        b2d = _stack_heads(bb * e_rem)
        k2d = _stack_heads(kd * e_rem)
        done.append((xn, xr, a_ab, a_ak, a_rb, a_rk, apow, vd, b2d, k2d, total))

    sub = n // SCAN_SPLIT
    parts = []
    waiting = [scores(lo, lo + sub, parts) for lo in range(0, n, sub)]
    running = []
    while waiting or running:
        if waiting:
            running.append(waiting.pop(0))
        running = [gen for gen in running if next(gen, True) is None]
        yield
    xn, xr, a_ab, a_ak, a_rb, a_rk, apow, vd, b2d, k2d, total = (
        jnp.concatenate([p[i] for p in parts], axis=0) for i in range(len(parts[0])))

    tinv = jnp.where(diag, 1.0, a_ab)
    for step in range(max(1, (c - 1).bit_length()) - 2):
        both = _bmm(apow, jnp.concatenate([apow, tinv], axis=2))
        apow = both[:, :, :n2]
        tinv = tinv + both[:, :, n2:]
        yield
    tinv = tinv + _bmm(apow, tinv)

    rhs_q = _bmm(a_ak, vd).astype(BF16)
    pq = _bmm(tinv, jnp.concatenate([xn, rhs_q], axis=2)).astype(BF16)
    yield
    b2t = jnp.swapaxes(b2d, 1, 2).astype(BF16)
    k2t = jnp.swapaxes(k2d, 1, 2).astype(BF16)
    lhs = jnp.concatenate([jnp.concatenate([a_rb, a_rk], axis=2), jnp.concatenate([b2t, k2t], axis=2)], axis=1)
    rhs = jnp.concatenate([pq, jnp.concatenate([jnp.zeros_like(vd), vd], axis=2)], axis=1)
    out = _bmm(lhs, rhs)
    r2 = xr + out[:, :n2, :LANES]
    y0 = out[:, :n2, LANES:]
    gm = out[:, n2:, :LANES]
    h0 = out[:, n2:, LANES:]
    gcol = jnp.sum(jnp.where(diag, jnp.exp(total), 0.0), axis=2, keepdims=True)
    finish(jnp.concatenate([r2, gm], axis=1), y0, gcol, h0)


def _scan_body(rf, kf, vf, kkf, lwf, af, rb, kb, vb, kkb, lwb, ab, ka_ref, yf_o, yb_o,
               st_ref, rg_ref, y0_ref, gc_ref, h0_ref, *, steps_per_seq):
    g = pl.program_id(0)

    @pl.when(g == 0)
    def _():
        st_ref[...] = jnp.zeros_like(st_ref)
        rg_ref[...] = jnp.zeros_like(rg_ref)
        y0_ref[...] = jnp.zeros_like(y0_ref)
        gc_ref[...] = jnp.zeros_like(gc_ref)
        h0_ref[...] = jnp.zeros_like(h0_ref)

    nch = SCAN_CHUNKS
    rows = nch * CHUNK
    n2 = 2 * CHUNK
    rd = (g + 1) % 2
    wr = g % 2
    fresh = (g % steps_per_seq) == (1 % steps_per_seq)
    state = [jnp.where(fresh, 0.0, st_ref[0]), jnp.where(fresh, 0.0, st_ref[1])]
    pending = [(grp, j) for grp in range(SCAN_GROUPS) for j in range(nch)]

    def chain_step():
        if not pending:
            return
        grp, j = pending.pop(0)
        f0 = grp * rows
        b0 = (SCAN_GROUPS - 1 - grp) * rows
        for d, jc, row0 in ((0, j, f0 + j * CHUNK), (1, 2 * nch - 1 - j, b0 + (nch - 1 - j) * CHUNK)):
            out = jnp.dot(rg_ref[rd, grp, jc], state[d].astype(BF16), preferred_element_type=F32)
            y = out[:n2] + y0_ref[rd, grp, jc]
            state[d] = state[d] * gc_ref[rd, grp, jc] + out[n2:] + h0_ref[rd, grp, jc]
            o_ref = yf_o if d == 0 else yb_o
            o_ref[row0:row0 + CHUNK, :] = (y[:CHUNK] + y[CHUNK:]).astype(o_ref.dtype)

    def group_terms(grp):
        f0 = grp * rows
        b0 = (SCAN_GROUPS - 1 - grp) * rows

        def both(f, b):
            return jnp.concatenate([f[f0:f0 + rows, :].astype(F32).reshape(nch, CHUNK, LANES),
                                    b[b0:b0 + rows, :].astype(F32).reshape(nch, CHUNK, LANES)], axis=0)

        def load():
            return (both(rf, rb), both(kf, kb), both(vf, vb), both(kkf, kkb), both(lwf, lwb), both(af, ab))

        def finish(rg, y0, gcol, h0):
            rg_ref[wr, grp] = rg.astype(BF16)
            y0_ref[wr, grp] = y0
            gc_ref[wr, grp] = jnp.broadcast_to(gcol, h0.shape)
            h0_ref[wr, grp] = h0

        return _chunk_terms(load, ka_ref[...], nch, finish)

    waiting = list(range(SCAN_GROUPS))
    running = []
    tick = 0
    while waiting or running or pending:
        if waiting and tick % SCAN_STAGGER == 0:
            running.append(group_terms(waiting.pop(0)))
        running = [gen for gen in running if next(gen, True) is None]
        chain_step()
        tick += 1
    st_ref[0] = state[0]
    st_ref[1] = state[1]


def _rwkv_scan(r, k, v, kkn, lw, a, k_a, batch, seq):
    m = r.shape[0]
    rows = SCAN_GROUPS * SCAN_CHUNKS * CHUNK
    ns = seq // rows
    npair = D_A // LANES
    nblocks = batch * npair * ns

    def place(blk_id, rev):
        s = blk_id % ns
        p = (blk_id // ns) % npair
        b = blk_id // (ns * npair)
        return (b * ns + (ns - 1 - s if rev else s), p)

    cur = lambda g: jnp.minimum(g, nblocks - 1)
    prev = lambda g: jnp.maximum(g - 1, 0)
    blk = (rows, LANES)
    fspec = pl.BlockSpec(blk, lambda g: place(cur(g), False))
    bspec = pl.BlockSpec(blk, lambda g: place(cur(g), True))
    fdir = pl.BlockSpec((None,) + blk, lambda g: (0,) + place(cur(g), False))
    bdir = pl.BlockSpec((None,) + blk, lambda g: (1,) + place(cur(g), True))
    out = jax.ShapeDtypeStruct((m, D_A), BF16)
    nchain = 2 * SCAN_CHUNKS
    terms = lambda rws, dt: pltpu.VMEM((2, SCAN_GROUPS, nchain, rws, LANES), dt)
    return pl.pallas_call(
        functools.partial(_scan_body, steps_per_seq=ns),
        grid=(nblocks + 1,),
        in_specs=[fspec, fspec, fspec, fspec, fdir, fdir, bspec, bspec, bspec, bspec, bdir, bdir,
                  pl.BlockSpec((1, LANES), lambda g: (0, place(cur(g), False)[1]))],
        out_specs=[pl.BlockSpec(blk, lambda g: place(prev(g), False)),
                   pl.BlockSpec(blk, lambda g: place(prev(g), True))],
        out_shape=[out, out],
        scratch_shapes=[pltpu.VMEM((2, LANES, LANES), F32), terms(2 * CHUNK + LANES, BF16),
                        terms(2 * CHUNK, F32), terms(LANES, F32), terms(LANES, F32)],
        compiler_params=_cparams(("arbitrary",)),
        name="rwkv_scan",
    )(r, k, v, kkn, lw, a, r, k, v, kkn, lw, a, k_a)


POOL_PAD = 16
POOL_COLS = 256
POOL_ROWS = 256


def _pool_body(p_ref, w_ref, inv_ref, z_ref, b_ref, s_ref, o_ref, pad_ref, acc_ref, *, seq):
    cb = pl.program_id(1)
    ncb = pl.num_programs(1)
    pad_ref[0:POOL_PAD, :] = jnp.zeros((POOL_PAD, POOL_COLS), BF16)
    pad_ref[POOL_PAD + seq:, :] = jnp.zeros((POOL_PAD, POOL_COLS), BF16)
    pad_ref[POOL_PAD:POOL_PAD + seq, :] = p_ref[...]

    @pl.when(cb == 0)
    def _():
        acc_ref[...] = jnp.zeros_like(acc_ref)

    nwin = POOL_ROWS + 2 * POOL_PAD
    lane = lax.broadcasted_iota(jnp.int32, (1, POOL_COLS), 1) + cb * POOL_COLS
    group = lane // POOL_CH
    g_lo = (cb * POOL_COLS) // POOL_CH
    g_hi = (cb * POOL_COLS + POOL_COLS - 1) // POOL_CH
    offset = jnp.abs(lax.broadcasted_iota(jnp.int32, (POOL_ROWS, nwin), 1) - POOL_PAD
                     - lax.broadcasted_iota(jnp.int32, (POOL_ROWS, nwin), 0))

    def band(grp):
        half = sum(jnp.where(grp == i, h, 0) for i, h in enumerate(POOL_HALF))
        return (offset <= half).astype(BF16)

    band_lo = band(g_lo)
    band_hi = band(g_hi)
    w = w_ref[...]

    def rows(i, carry):
        base = pl.multiple_of(i * POOL_ROWS, POOL_ROWS)
        win = pad_ref[pl.ds(base, nwin), :]
        x = win[POOL_PAD:POOL_PAD + POOL_ROWS].astype(F32)
        total = jnp.where(group == g_lo, jnp.dot(band_lo, win, preferred_element_type=F32),
                          jnp.dot(band_hi, win, preferred_element_type=F32))
        mean = total * inv_ref[pl.ds(base, POOL_ROWS), :]
        acc_ref[pl.ds(base, POOL_ROWS), :] += _bdot(mean - x, w)
        return carry

    lax.fori_loop(0, seq // POOL_ROWS, rows, 0, unroll=4)

    @pl.when(cb == ncb - 1)
    def _():
        o_ref[...] = ((acc_ref[...] + b_ref[...]) * s_ref[...] * z_ref[...].astype(F32)).astype(o_ref.dtype)


def _pool_inv_count(seq):
    half = jnp.repeat(jnp.array(POOL_HALF, jnp.int32), POOL_CH)[None, :]
    t = jnp.arange(seq, dtype=jnp.int32)[:, None]
    count = jnp.minimum(t + half, seq - 1) - jnp.maximum(t - half, 0) + 1
    return 1.0 / count.astype(F32)


def _pool(p, zb, w_bd, inv_count, b, scale, batch, seq):
    m = p.shape[0]
    ncb = D_B // POOL_COLS
    full = pl.BlockSpec((seq, D_B), lambda bi, cb: (bi, 0))
    vec = pl.BlockSpec((1, D_B), lambda bi, cb: (0, 0))
    return pl.pallas_call(
        functools.partial(_pool_body, seq=seq),
        grid=(batch, ncb),
        in_specs=[pl.BlockSpec((seq, POOL_COLS), lambda bi, cb: (bi, cb)),
                  pl.BlockSpec((POOL_COLS, D_B), lambda bi, cb: (cb, 0)),
                  pl.BlockSpec((seq, POOL_COLS), lambda bi, cb: (0, cb)),
                  full, vec, vec],
        out_specs=full,
        out_shape=jax.ShapeDtypeStruct((m, D_B), BF16),
        scratch_shapes=[pltpu.VMEM((seq + 2 * POOL_PAD, POOL_COLS), BF16), pltpu.VMEM((seq, D_B), F32)],
        compiler_params=_cparams(("parallel", "arbitrary")),
        name="pool_mixer",
    )(p, w_bd, inv_count, zb, b, scale)


def _rope_matrix():
    half = HEAD_DIM // 2
    src = lax.broadcasted_iota(jnp.int32, (LANES, LANES), 0)
    dst = lax.broadcasted_iota(jnp.int32, (LANES, LANES), 1)
    low = (dst % HEAD_DIM) < half
    return jnp.where(low & (src == dst + half), -1.0, jnp.where(~low & (src == dst - half), 1.0, 0.0)).astype(BF16)


def _rope(t, cos, sin, rot_matrix):
    return t * cos + _split_dot(t, rot_matrix, 2) * sin


ATTN_SIDE = 64
ATTN_TILES = 4
COMBINE_ROWS = 256


def _attn_body(q0, q1, q2, k0, k1, k2, v0, v1, v2, cos_ref, sin_ref, zc_ref, o_ref,
               qp_ref, kp_ref, vp_ref, op_ref, lp_ref, on_ref, ln_ref, *, seq):
    q_refs = (q0, q1, q2)
    k_refs = (k0, k1, k2)
    v_refs = (v0, v1, v2)
    scale = HEAD_DIM ** -0.5
    rot_matrix = _rope_matrix()
    for g, (half_span, dil) in enumerate(ATTN_GROUPS):
        cls = seq // dil
        for r in range(dil):
            rows = pl.ds(r, cls, stride=dil) if dil > 1 else pl.ds(0, seq)
            dst = pl.ds(r * cls, cls)
            cos = cos_ref[rows, :]
            sin = sin_ref[rows, :]
            qp_ref[g, dst, :] = (_rope(q_refs[g][rows, :], cos, sin, rot_matrix) * scale).astype(BF16)
            kp_ref[g, dst, :] = _rope(k_refs[g][rows, :], cos, sin, rot_matrix).astype(BF16)
            vp_ref[g, dst, :] = v_refs[g][rows, :].astype(BF16)

    nt = ATTN_TILES
    first = lax.broadcasted_iota(jnp.int32, (nt, Q_TILE, LANES), 2) < HEAD_DIM
    for g, (half_span, dil) in enumerate(ATTN_GROUPS):
        cls = seq // dil
        klen = min(cls, Q_TILE + 2 * ATTN_SIDE)
        rel = (lax.broadcasted_iota(jnp.int32, (Q_TILE, klen), 1)
               - lax.broadcasted_iota(jnp.int32, (Q_TILE, klen), 0))

        def tiles(it, carry, g=g, cls=cls, klen=klen, rel=rel):
            row0 = pl.multiple_of(it * (nt * Q_TILE), nt * Q_TILE)
            keys, vals, valid = [], [], []
            for j in range(nt):
                qrow = row0 + j * Q_TILE
                base = (qrow // cls) * cls
                krow = pl.multiple_of(base + jnp.clip(qrow - base - ATTN_SIDE, 0, cls - klen), ATTN_SIDE)
                keys.append(kp_ref[g, pl.ds(krow, klen), :])
                vals.append(vp_ref[g, pl.ds(krow, klen), :])
                valid.append(jnp.abs(rel + (krow - qrow)) <= ATTN_SIDE)
            q = qp_ref[g, pl.ds(row0, nt * Q_TILE), :].reshape(nt, Q_TILE, LANES)
            zero = jnp.zeros_like(q)
            qm = jnp.concatenate([jnp.where(first, q, zero), jnp.where(first, zero, q)], axis=0)
            keys = jnp.stack(keys + keys)
            vals = jnp.stack(vals + vals)
            valid = jnp.stack(valid + valid)
            s = jnp.where(valid, _bmm_nt(qm, keys), NEG_BIG)
            mx = jnp.max(s, axis=2, keepdims=True)
            e = jnp.exp(s - mx)
            den = jnp.sum(e, axis=2, keepdims=True)
            o = _bmm(e, vals) / den
            lse = jnp.broadcast_to(mx + jnp.log(den), o.shape)
            rows = pl.ds(row0, nt * Q_TILE)
            op_ref[g, rows, :] = jnp.where(first, o[:nt], o[nt:]).reshape(nt * Q_TILE, LANES)
            lp_ref[g, rows, :] = jnp.where(first, lse[:nt], lse[nt:]).reshape(nt * Q_TILE, LANES)
            return carry

        lax.fori_loop(0, seq // (nt * Q_TILE), tiles, 0, unroll=2)
        if dil > 1:
            for r in range(dil):
                src = pl.ds(r * cls, cls)
                on_ref[g - 1, pl.ds(r, cls, stride=dil), :] = op_ref[g, src, :]
                ln_ref[g - 1, pl.ds(r, cls, stride=dil), :] = lp_ref[g, src, :]

    def combine(i, carry):
        rows = pl.ds(pl.multiple_of(i * COMBINE_ROWS, COMBINE_ROWS), COMBINE_ROWS)
        ls = [lp_ref[0, rows, :], ln_ref[0, rows, :], ln_ref[1, rows, :]]
        os_ = [op_ref[0, rows, :], on_ref[0, rows, :], on_ref[1, rows, :]]
        top = jnp.maximum(jnp.maximum(ls[0], ls[1]), ls[2])
        ws = [jnp.exp(l - top) for l in ls]
        tot = ws[0] + ws[1] + ws[2]
        out = (ws[0] * os_[0] + ws[1] * os_[1] + ws[2] * os_[2]) / tot * zc_ref[rows, :].astype(F32)
        o_ref[rows, :] = out.astype(o_ref.dtype)
        return carry

    lax.fori_loop(0, seq // COMBINE_ROWS, combine, 0)


def _attention(qkv, zc, cos, sin, batch, seq):
    m = qkv.shape[0]
    ncol = D_C // LANES
    per_group = D_C_OUT // LANES
    ngroup = len(ATTN_GROUPS)
    spec = lambda sec, g: pl.BlockSpec((seq, LANES), lambda b, jp: (b, sec * ncol + g * per_group + jp))
    tab = pl.BlockSpec((seq, LANES), lambda b, jp: (0, 0))
    oz = pl.BlockSpec((seq, LANES), lambda b, jp: (b, jp))
    perm_bf = pltpu.VMEM((ngroup, seq, LANES), BF16)
    perm_f32 = pltpu.VMEM((ngroup, seq, LANES), F32)
    nat_f32 = pltpu.VMEM((ngroup - 1, seq, LANES), F32)
    return pl.pallas_call(
        functools.partial(_attn_body, seq=seq),
        grid=(batch, per_group),
        in_specs=[spec(0, 0), spec(0, 1), spec(0, 2), spec(1, 0), spec(1, 1), spec(1, 2),
                  spec(2, 0), spec(2, 1), spec(2, 2), tab, tab, oz],
        out_specs=oz,
        out_shape=jax.ShapeDtypeStruct((m, D_C_OUT), BF16),
        scratch_shapes=[perm_bf, perm_bf, perm_bf, perm_f32, perm_f32, nat_f32, nat_f32],
        compiler_params=_cparams(("parallel", "parallel")),
        name="dilated_attention",
    )(qkv, qkv, qkv, qkv, qkv, qkv, qkv, qkv, qkv, cos, sin, zc)


def _merge_body(x_ref, yf_ref, yb_ref, bonus_ref, za_ref, ob_ref, oc_ref, g_ref,
                gng_ref, gnb_ref, pa_ref, pb_ref, pc_ref, wo_ref, lng_ref, lnb_ref, o_ref, ob16_ref):
    ones = _head_ones(LANES)
    y = yf_ref[...].astype(F32) + yb_ref[...].astype(F32)
    parts = []
    for cb in range(D_A // LANES):
        sl = slice(cb * LANES, (cb + 1) * LANES)
        yc = y[:, sl]
        mu = _bdot(yc, ones) * (1.0 / HEAD_DIM)
        d = yc - mu
        var = _bdot(d * d, ones) * (1.0 / HEAD_DIM)
        parts.append(d * lax.rsqrt(var + RWKV_GN_EPS))
    yn = jnp.concatenate(parts, axis=1)
    o_a = (yn * gng_ref[...] + gnb_ref[...] + bonus_ref[...].astype(F32)) * za_ref[...].astype(F32)
    merged = (g_ref[:, 0:D_MODEL].astype(F32) * _bdot(o_a, pa_ref[...])
              + g_ref[:, D_MODEL:2 * D_MODEL].astype(F32) * _bdot(ob_ref[...], pb_ref[...])
              + g_ref[:, 2 * D_MODEL:3 * D_MODEL].astype(F32) * _bdot(oc_ref[...], pc_ref[...]))
    t = DEEPNORM_ALPHA * x_ref[...] + _bdot(merged, wo_ref[...])
    mean = jnp.mean(t, axis=1, keepdims=True)
    d = t - mean
    var = jnp.mean(d * d, axis=1, keepdims=True)
    out = d * lax.rsqrt(var + LN_EPS) * lng_ref[...] + lnb_ref[...]
    o_ref[...] = out
    ob16_ref[...] = out.astype(BF16)


def _merge(x, yf, yb, bonus, za, ob, oc, gates, gn_g, gn_b, pa, pb, pc, wo, ln_g, ln_b):
    m = x.shape[0]
    tm = min(512, m)
    rows = lambda n: pl.BlockSpec((tm, n), lambda i: (i, 0))
    full = lambda a: pl.BlockSpec(a.shape, lambda i: (0, 0))
    return pl.pallas_call(
        _merge_body,
        grid=(m // tm,),
        in_specs=[rows(D_MODEL), rows(D_A), rows(D_A), rows(D_A), rows(D_A), rows(D_B), rows(D_C_OUT),
                  rows(3 * D_MODEL), full(gn_g), full(gn_b), full(pa), full(pb), full(pc), full(wo),
                  full(ln_g), full(ln_b)],
        out_specs=[rows(D_MODEL), rows(D_MODEL)],
        out_shape=[jax.ShapeDtypeStruct((m, D_MODEL), F32), jax.ShapeDtypeStruct((m, D_MODEL), BF16)],
        compiler_params=_cparams(("parallel",)),
        name="merge_norm",
    )(x, yf, yb, bonus, za, ob, oc, gates, gn_g, gn_b, pa, pb, pc, wo, ln_g, ln_b)


def _rope_tables(seq):
    inv = jnp.power(ROPE_THETA, -jnp.arange(0, HEAD_DIM, 2, dtype=F32) / HEAD_DIM)
    ang = jnp.arange(seq, dtype=F32)[:, None] * inv[None, :]
    ang = jnp.concatenate([ang, ang, ang, ang], axis=-1)
    return jnp.cos(ang), jnp.sin(ang)


def _pad_lora(w_up):
    z = jnp.zeros_like(w_up[0])
    return jnp.stack([jnp.concatenate([w_up[0], z], axis=0), jnp.concatenate([z, w_up[1]], axis=0)])


def _block_diag(w_g):
    g, c, _ = w_g.shape
    eye = jnp.eye(g, dtype=w_g.dtype)
    return (eye[:, None, :, None] * w_g[:, :, None, :]).reshape(g * c, g * c)


def kernel(x, w_in, b_in, rwkv_mu, rwkv_w0, rwkv_w_up, rwkv_a0, rwkv_a_up, rwkv_k_k, rwkv_k_a, rwkv_r_k, rwkv_gn_g, rwkv_gn_b, pool_w, pool_b, pool_scale, proj_a, proj_b, proj_c, w_out, ln_g, ln_b):
    batch, seq, _ = x.shape
    m = batch * seq
    cos, sin = _rope_tables(seq)
    inv_count = _pool_inv_count(seq)
    row = lambda t: t.reshape(1, -1)
    bounds = [0]
    for width in (N_SHIFT, D_A, D_B, D_B, 3 * D_C, D_C_OUT, 3 * D_MODEL):
        bounds.append(bounds[-1] + width)
    acts = (None, "silu", None, "silu", None, "silu", "sigmoid")
    dtypes = (F32, BF16, BF16, BF16, F32, BF16, BF16)

    xf = x.reshape(m, D_MODEL)
    xb = xf.astype(BF16)
    for l in range(DEPTH):
        sec = [_linear(xb, w_in, b_in, l, lo, hi - lo, act, dt)
               for lo, hi, act, dt in zip(bounds[:-1], bounds[1:], acts, dtypes)]
        u, za, p, zb, qkv, zc, gates = sec

        r, k, v, kkn, lw, a, bonus = _rwkv_prep(
            u, rwkv_mu[l], rwkv_w0[l], rwkv_a0[l],
            _pad_lora(rwkv_w_up[l]).astype(BF16), _pad_lora(rwkv_a_up[l]).astype(BF16),
            row(rwkv_k_k[l]), row(rwkv_r_k[l]), seq)
        yf, yb = _rwkv_scan(r, k, v, kkn, lw, a, row(rwkv_k_a[l]), batch, seq)
        ob = _pool(p, zb, _block_diag(pool_w[l]).astype(BF16), inv_count, row(pool_b[l]), row(pool_scale[l]),
                   batch, seq)
        oc = _attention(qkv, zc, cos, sin, batch, seq)
        xf, xb = _merge(xf, yf, yb, bonus, za, ob, oc, gates, row(rwkv_gn_g[l]), row(rwkv_gn_b[l]),
                        proj_a[l].astype(BF16), proj_b[l].astype(BF16), proj_c[l].astype(BF16),
                        w_out[l].astype(BF16), row(ln_g[l]), row(ln_b[l]))
    return xf.reshape(batch, seq, D_MODEL)
```

```python
import functools
import math

import jax
import jax.numpy as jnp
from jax import lax
from jax.experimental import pallas as pl
from jax.experimental.pallas import tpu as pltpu

F32 = jnp.float32
BF16 = jnp.bfloat16

D_MODEL = 1024
DEPTH = 4
HEAD_DIM = 64
LANES = 128
SUBLANES = 8
D_A = 1024
LORA = 64
N_SHIFT = 3 * D_A + 4 * LORA
RWKV_GN_EPS = 64e-5
LOG_DECAY_SCALE = -math.exp(-0.5)
POOL_HALF = (1, 2, 4, 8)
POOL_CH = 192
D_B = 768
D_C = 768
D_C_OUT = 256
ATTN_GROUPS = ((64, 1), (256, 4), (1024, 16))
ROPE_THETA = 10000.0
DEEPNORM_ALPHA = (2 * DEPTH) ** 0.25
LN_EPS = 1e-5
CHUNK = 64
Q_TILE = 128
NEG_BIG = -1e30
VMEM_LIMIT = 52 * 1024 * 1024
LINEAR_ROWS = 1024
LINEAR_COLS_MAX = 1664
ROW_TILE = 512


def _cparams(sem):
    return pltpu.CompilerParams(dimension_semantics=sem, vmem_limit_bytes=VMEM_LIMIT)


def _bdot(a, b):
    return jnp.dot(a.astype(BF16), b.astype(BF16), preferred_element_type=F32)


def _split_dot(a, b_exact, terms):
    acc = None
    rem = a
    for _ in range(terms):
        piece = rem.astype(BF16)
        part = jnp.dot(piece, b_exact, preferred_element_type=F32)
        acc = part if acc is None else acc + part
        rem = rem - piece.astype(F32)
    return acc


def _head_ones(n):
    r = lax.broadcasted_iota(jnp.int32, (n, n), 0) // HEAD_DIM
    c = lax.broadcasted_iota(jnp.int32, (n, n), 1) // HEAD_DIM
    return (r == c).astype(BF16)


def _sigmoid(x):
    return 0.5 + 0.5 * jnp.tanh(0.5 * x)


def _linear_body(x_ref, w_ref, b_ref, o_ref, wb_ref, *, act, layer):
    @pl.when(pl.program_id(1) == 0)
    def _():
        wb_ref[...] = w_ref[0].astype(BF16)

    acc = jnp.dot(x_ref[...], wb_ref[...], preferred_element_type=F32) + b_ref[layer:layer + 1, :]
    if act == "silu":
        acc = acc * _sigmoid(acc)
    elif act == "sigmoid":
        acc = _sigmoid(acc)
    o_ref[...] = acc.astype(o_ref.dtype)


def _pick_tile(n, cap):
    best = LANES
    for t in range(LANES, cap + 1, LANES):
        if n % t == 0:
            best = t
    return best


def _linear(x, w, b, layer, lo, n, act, out_dtype):
    m, k = x.shape
    tm = min(LINEAR_ROWS, m)
    tn = _pick_tile(n, LINEAR_COLS_MAX)
    col = lambda j: pl.multiple_of(lo + j * tn, LANES)
    return pl.pallas_call(
        functools.partial(_linear_body, act=act, layer=layer),
        grid=(n // tn, m // tm),
        in_specs=[pl.BlockSpec((tm, k), lambda j, i: (i, 0)),
                  pl.BlockSpec((pl.Element(1), pl.Element(k), pl.Element(tn)), lambda j, i: (layer, 0, col(j))),
                  pl.BlockSpec((pl.Element(b.shape[0]), pl.Element(tn)), lambda j, i: (0, col(j)))],
        out_specs=pl.BlockSpec((tm, tn), lambda j, i: (i, j)),
        out_shape=jax.ShapeDtypeStruct((m, n), out_dtype),
        scratch_shapes=[pltpu.VMEM((k, tn), BF16)],
        compiler_params=_cparams(("parallel", "arbitrary")),
        name="linear_" + (act or "id"),
    )(x, w, b)


def _prep_body(u_ref, up_ref, un_ref, mu_ref, w0_ref, a0_ref, wup_ref, aup_ref, kk_ref, rk_ref,
               r_o, k_o, v_o, kkn_o, lw_o, a_o, bonus_o, *, tm, seq):
    i = pl.program_id(0)
    u = u_ref[...]
    sub = SUBLANES
    row = lax.broadcasted_iota(jnp.int32, (sub, 1), 0)
    t0 = (i * tm) % seq
    prev_edge = jnp.where(t0 == 0, 0.0, up_ref[sub - 1:sub, :])
    next_edge = jnp.where(t0 + tm == seq, 0.0, un_ref[0:1, :])
    prev = pltpu.roll(u, 1, 0)
    prev = jnp.concatenate([jnp.where(row == 0, prev_edge, prev[0:sub]), prev[sub:]], axis=0)
    nxt = pltpu.roll(u, tm - 1, 0)
    nxt = jnp.concatenate([nxt[:tm - sub], jnp.where(row == sub - 1, next_edge, nxt[tm - sub:])], axis=0)
    mu_prev = mu_ref[0:1, :]
    mu_next = mu_ref[1:2, :]
    u = (1.0 - mu_prev - mu_next) * u + mu_prev * prev + mu_next * nxt

    r = u[:, 0:D_A]
    k = u[:, D_A:2 * D_A]
    v = u[:, 2 * D_A:3 * D_A]
    wd = jnp.tanh(u[:, 3 * D_A:3 * D_A + 2 * LORA])
    ad = u[:, 3 * D_A + 2 * LORA:N_SHIFT]
    r_o[...] = r.astype(r_o.dtype)
    k_o[...] = k.astype(k_o.dtype)
    v_o[...] = v.astype(v_o.dtype)
    for z in range(2):
        warg = w0_ref[z:z + 1, :] + _bdot(wd, wup_ref[z])
        lw_o[z] = LOG_DECAY_SCALE * _sigmoid(warg)
        a_o[z] = _sigmoid(a0_ref[z:z + 1, :] + _bdot(ad, aup_ref[z])).astype(a_o.dtype)

    ones = _head_ones(LANES)
    for cb in range(D_A // LANES):
        sl = slice(cb * LANES, (cb + 1) * LANES)
        kk = k[:, sl] * kk_ref[:, sl]
        ss = _bdot(kk * kk, ones)
        kkn_o[:, sl] = (kk * lax.rsqrt(ss + 1e-12)).astype(kkn_o.dtype)
        rk = _bdot(r[:, sl] * k[:, sl] * rk_ref[:, sl], ones)
        bonus_o[:, sl] = (rk * v[:, sl]).astype(bonus_o.dtype)


def _rwkv_prep(u, mu, w0, a0, wup, aup, k_k, r_k, seq):
    m = u.shape[0]
    tm = min(ROW_TILE, seq)
    halo = SUBLANES
    nhalo = m // halo
    full = lambda shape: pl.BlockSpec(shape, lambda i: (0,) * len(shape))
    row_spec = pl.BlockSpec((tm, D_A), lambda i: (i, 0))
    dir_spec = pl.BlockSpec((2, tm, D_A), lambda i: (0, i, 0))
    act = jax.ShapeDtypeStruct((m, D_A), BF16)
    act2 = jax.ShapeDtypeStruct((2, m, D_A), BF16)
    logw = jax.ShapeDtypeStruct((2, m, D_A), F32)
    return pl.pallas_call(
        functools.partial(_prep_body, tm=tm, seq=seq),
        grid=(m // tm,),
        in_specs=[pl.BlockSpec((tm, N_SHIFT), lambda i: (i, 0)),
                  pl.BlockSpec((halo, N_SHIFT), lambda i: (jnp.maximum(i * (tm // halo) - 1, 0), 0)),
                  pl.BlockSpec((halo, N_SHIFT), lambda i: (jnp.minimum((i + 1) * (tm // halo), nhalo - 1), 0)),
                  full((2, N_SHIFT)), full((2, D_A)), full((2, D_A)),
                  full((2, LANES, D_A)), full((2, LANES, D_A)), full((1, D_A)), full((1, D_A))],
        out_specs=[row_spec, row_spec, row_spec, row_spec, dir_spec, dir_spec, row_spec],
        out_shape=[act, act, act, act, logw, act2, act],
        compiler_params=_cparams(("parallel",)),
        name="rwkv_prep",
    )(u, u, u, mu, w0, a0, wup, aup, k_k, r_k)


SCAN_CHUNKS = 8
SCAN_GROUPS = 2
SCAN_SPLIT = 2


def _bmm(a, b):
    return lax.dot_general(a.astype(BF16), b.astype(BF16), (((2,), (1,)), ((0,), (0,))),
                           preferred_element_type=F32)


def _bmm_nt(a, b):
    return lax.dot_general(a.astype(BF16), b.astype(BF16), (((2,), (2,)), ((0,), (0,))),
                           preferred_element_type=F32)


def _stack_heads(t):
    lane = lax.broadcasted_iota(jnp.int32, t.shape, 2)
    first = lane < HEAD_DIM
    return jnp.concatenate([jnp.where(first, t, 0.0), jnp.where(first, 0.0, t)], axis=1)


def _chunk_terms(load, k_a, nfwd, finish):
    r, k, v, kk, lw, a = load()
    n, c, _ = r.shape
    n2 = 2 * c
    ri = lax.broadcasted_iota(jnp.int32, (c, 3 * c), 0)
    ci = lax.broadcasted_iota(jnp.int32, (c, 3 * c), 1) % c
    rr = lax.broadcasted_iota(jnp.int32, (n2, n2), 0)
    cc = lax.broadcasted_iota(jnp.int32, (n2, n2), 1)
    same = (rr // c) == (cc // c)
    diag = (rr == cc)[None]
    before = (same & (cc % c < rr % c))[None]
    after = (same & (cc % c > rr % c))[None]

    def scores(lo, hi, done):
        rev = lo >= nfwd
        sl = slice(lo, hi)
        pieces = []
        rem = lw[sl]
        for _ in range(3):
            pieces.append(rem.astype(BF16))
            rem = rem - pieces[-1].astype(F32)
        pieces = jnp.concatenate(pieces, axis=1)
        tri = ((ci >= ri) if rev else (ci <= ri)).astype(BF16)
        wide = jnp.dot(tri, jnp.concatenate([pieces[i] for i in range(hi - lo)], axis=1),
                       preferred_element_type=F32)
        yield
        cum = jnp.stack([wide[:, i * LANES:(i + 1) * LANES] for i in range(hi - lo)])
        total = cum[:, 0:1, :] if rev else cum[:, c - 1:c, :]
        kd = k[sl] * (1.0 + (a[sl] - 1.0) * k_a)
        bb = kk[sl] * a[sl]
        e_neg = jnp.exp(-cum)
        bt = bb * e_neg
        kt = kd * e_neg
        xn = _stack_heads(-kk[sl] * jnp.exp(cum - lw[sl])).astype(BF16)
        xr = _stack_heads(r[sl] * jnp.exp(cum))
        aa = _bmm_nt(jnp.concatenate([xn, xr.astype(BF16)], axis=1), jnp.concatenate([bt, kt], axis=1))
        yield
        low = lax.broadcasted_iota(jnp.int32, (1, n2), 1) < c
        strict = (after if rev else before)[0]
        causal = strict | diag[0]
        masked = []
        for i in range(hi - lo):
            swapped = pltpu.roll(aa[i], c, 1)
            ab = jnp.where(low, aa[i], swapped)
            ak = jnp.where(low, swapped, aa[i])
            masked.append((jnp.where(strict, ab[:n2], 0.0), jnp.where(strict, ak[:n2], 0.0).astype(BF16),
                           jnp.where(causal, ab[n2:], 0.0).astype(BF16),
                           jnp.where(causal, ak[n2:], 0.0).astype(BF16)))
        a_ab, a_ak, a_rb, a_rk = (jnp.stack([m[j] for m in masked]) for j in range(4))
        apow = _bmm(a_ab, a_ab)
        yield
        e_rem = jnp.exp(total - cum)
        vd = _stack_heads(v[sl]).astype(BF16)
        b2d = _stack_heads(bb * e_rem)
        k2d = _stack_heads(kd * e_rem)
        done.append((xn, xr, a_ab, a_ak, a_rb, a_rk, apow, vd, b2d, k2d, total))

    sub = n // SCAN_SPLIT
    parts = []
    waiting = [scores(lo, lo + sub, parts) for lo in range(0, n, sub)]
    running = []
    while waiting or running:
        if waiting:
            running.append(waiting.pop(0))
        running = [gen for gen in running if next(gen, True) is None]
        yield
    xn, xr, a_ab, a_ak, a_rb, a_rk, apow, vd, b2d, k2d, total = (
        jnp.concatenate([p[i] for p in parts], axis=0) for i in range(len(parts[0])))

    tinv = jnp.where(diag, 1.0, a_ab)
    for step in range(max(1, (c - 1).bit_length()) - 2):
        both = _bmm(apow, jnp.concatenate([apow, tinv], axis=2))
        apow = both[:, :, :n2]
        tinv = tinv + both[:, :, n2:]
        yield
    tinv = tinv + _bmm(apow, tinv)

    rhs_q = _bmm(a_ak, vd).astype(BF16)
    pq = _bmm(tinv, jnp.concatenate([xn, rhs_q], axis=2)).astype(BF16)
    yield
    b2t = jnp.swapaxes(b2d, 1, 2).astype(BF16)
    k2t = jnp.swapaxes(k2d, 1, 2).astype(BF16)
    lhs = jnp.concatenate([jnp.concatenate([a_rb, a_rk], axis=2), jnp.concatenate([b2t, k2t], axis=2)], axis=1)
    rhs = jnp.concatenate([pq, jnp.concatenate([jnp.zeros_like(vd), vd], axis=2)], axis=1)
    out = _bmm(lhs, rhs)
    r2 = xr + out[:, :n2, :LANES]
    y0 = out[:, :n2, LANES:]
    gm = out[:, n2:, :LANES]
    h0 = out[:, n2:, LANES:]
    gcol = jnp.sum(jnp.where(diag, jnp.exp(total), 0.0), axis=2, keepdims=True)
    finish(jnp.concatenate([r2, gm], axis=1), y0, gcol, h0)


def _scan_body(rf, kf, vf, kkf, lwf, af, rb, kb, vb, kkb, lwb, ab, ka_ref, yf_o, yb_o,
               st_ref, rg_ref, y0_ref, gc_ref, h0_ref, *, steps_per_seq):
    g = pl.program_id(0)

    @pl.when(g == 0)
    def _():
        st_ref[...] = jnp.zeros_like(st_ref)
        rg_ref[...] = jnp.zeros_like(rg_ref)
        y0_ref[...] = jnp.zeros_like(y0_ref)
        gc_ref[...] = jnp.zeros_like(gc_ref)
        h0_ref[...] = jnp.zeros_like(h0_ref)

    nch = SCAN_CHUNKS
    rows = nch * CHUNK
    n2 = 2 * CHUNK
    rd = (g + 1) % 2
    wr = g % 2
    fresh = (g % steps_per_seq) == (1 % steps_per_seq)
    state = [jnp.where(fresh, 0.0, st_ref[0]), jnp.where(fresh, 0.0, st_ref[1])]
    pending = [(grp, j) for grp in range(SCAN_GROUPS) for j in range(nch)]

    def chain_step():
        if not pending:
            return
        grp, j = pending.pop(0)
        f0 = grp * rows
        b0 = (SCAN_GROUPS - 1 - grp) * rows
        for d, jc, row0 in ((0, j, f0 + j * CHUNK), (1, 2 * nch - 1 - j, b0 + (nch - 1 - j) * CHUNK)):
            out = jnp.dot(rg_ref[rd, grp, jc], state[d].astype(BF16), preferred_element_type=F32)
            y = out[:n2] + y0_ref[rd, grp, jc]
            state[d] = state[d] * gc_ref[rd, grp, jc] + out[n2:] + h0_ref[rd, grp, jc]
            o_ref = yf_o if d == 0 else yb_o
            o_ref[row0:row0 + CHUNK, :] = (y[:CHUNK] + y[CHUNK:]).astype(o_ref.dtype)

    def group_terms(grp):
        f0 = grp * rows
        b0 = (SCAN_GROUPS - 1 - grp) * rows

        def both(f, b):
            return jnp.concatenate([f[f0:f0 + rows, :].astype(F32).reshape(nch, CHUNK, LANES),
                                    b[b0:b0 + rows, :].astype(F32).reshape(nch, CHUNK, LANES)], axis=0)

        def load():
            return (both(rf, rb), both(kf, kb), both(vf, vb), both(kkf, kkb), both(lwf, lwb), both(af, ab))

        def finish(rg, y0, gcol, h0):
            rg_ref[wr, grp] = rg.astype(BF16)
            y0_ref[wr, grp] = y0
            gc_ref[wr, grp] = jnp.broadcast_to(gcol, h0.shape)
            h0_ref[wr, grp] = h0

        return _chunk_terms(load, ka_ref[...], nch, finish)

    for grp in range(SCAN_GROUPS):
        for _ in group_terms(grp):
            chain_step()
    while pending:
        chain_step()
    st_ref[0] = state[0]
    st_ref[1] = state[1]


def _rwkv_scan(r, k, v, kkn, lw, a, k_a, batch, seq):
    m = r.shape[0]
    rows = SCAN_GROUPS * SCAN_CHUNKS * CHUNK
    assert seq % rows == 0 and 2 * CHUNK == LANES and SCAN_CHUNKS % SCAN_SPLIT == 0, (seq, rows)
    ns = seq // rows
    npair = D_A // LANES
    nblocks = batch * npair * ns

    def place(blk_id, rev):
        s = blk_id % ns
        p = (blk_id // ns) % npair
        b = blk_id // (ns * npair)
        return (b * ns + (ns - 1 - s if rev else s), p)

    cur = lambda g: jnp.minimum(g, nblocks - 1)
    prev = lambda g: jnp.maximum(g - 1, 0)
    blk = (rows, LANES)
    fspec = pl.BlockSpec(blk, lambda g: place(cur(g), False))
    bspec = pl.BlockSpec(blk, lambda g: place(cur(g), True))
    fdir = pl.BlockSpec((None,) + blk, lambda g: (0,) + place(cur(g), False))
    bdir = pl.BlockSpec((None,) + blk, lambda g: (1,) + place(cur(g), True))
    out = jax.ShapeDtypeStruct((m, D_A), BF16)
    nchain = 2 * SCAN_CHUNKS
    terms = lambda rws, dt: pltpu.VMEM((2, SCAN_GROUPS, nchain, rws, LANES), dt)
    return pl.pallas_call(
        functools.partial(_scan_body, steps_per_seq=ns),
        grid=(nblocks + 1,),
        in_specs=[fspec, fspec, fspec, fspec, fdir, fdir, bspec, bspec, bspec, bspec, bdir, bdir,
                  pl.BlockSpec((1, LANES), lambda g: (0, place(cur(g), False)[1]))],
        out_specs=[pl.BlockSpec(blk, lambda g: place(prev(g), False)),
                   pl.BlockSpec(blk, lambda g: place(prev(g), True))],
        out_shape=[out, out],
        scratch_shapes=[pltpu.VMEM((2, LANES, LANES), F32), terms(2 * CHUNK + LANES, BF16),
                        terms(2 * CHUNK, F32), terms(LANES, F32), terms(LANES, F32)],
        compiler_params=_cparams(("arbitrary",)),
        name="rwkv_scan",
    )(r, k, v, kkn, lw, a, r, k, v, kkn, lw, a, k_a)


POOL_PAD = 16
POOL_COLS = 256
POOL_ROWS = 256


def _pool_body(p_ref, w_ref, inv_ref, z_ref, b_ref, s_ref, o_ref, pad_ref, acc_ref, *, seq):
    cb = pl.program_id(1)
    ncb = pl.num_programs(1)
    pad_ref[0:POOL_PAD, :] = jnp.zeros((POOL_PAD, POOL_COLS), BF16)
    pad_ref[POOL_PAD + seq:, :] = jnp.zeros((POOL_PAD, POOL_COLS), BF16)
    pad_ref[POOL_PAD:POOL_PAD + seq, :] = p_ref[...]

    @pl.when(cb == 0)
    def _():
        acc_ref[...] = jnp.zeros_like(acc_ref)

    nwin = POOL_ROWS + 2 * POOL_PAD
    lane = lax.broadcasted_iota(jnp.int32, (1, POOL_COLS), 1) + cb * POOL_COLS
    group = lane // POOL_CH
    g_lo = (cb * POOL_COLS) // POOL_CH
    g_hi = (cb * POOL_COLS + POOL_COLS - 1) // POOL_CH
    offset = jnp.abs(lax.broadcasted_iota(jnp.int32, (POOL_ROWS, nwin), 1) - POOL_PAD
                     - lax.broadcasted_iota(jnp.int32, (POOL_ROWS, nwin), 0))

    def band(grp):
        half = sum(jnp.where(grp == i, h, 0) for i, h in enumerate(POOL_HALF))
        return (offset <= half).astype(BF16)

    band_lo = band(g_lo)
    band_hi = band(g_hi)
    w = w_ref[...]

    def rows(i, carry):
        base = pl.multiple_of(i * POOL_ROWS, POOL_ROWS)
        win = pad_ref[pl.ds(base, nwin), :]
        x = win[POOL_PAD:POOL_PAD + POOL_ROWS].astype(F32)
        total = jnp.where(group == g_lo, jnp.dot(band_lo, win, preferred_element_type=F32),
                          jnp.dot(band_hi, win, preferred_element_type=F32))
        mean = total * inv_ref[pl.ds(base, POOL_ROWS), :]
        acc_ref[pl.ds(base, POOL_ROWS), :] += _bdot(mean - x, w)
        return carry

    lax.fori_loop(0, seq // POOL_ROWS, rows, 0, unroll=4)

    @pl.when(cb == ncb - 1)
    def _():
        o_ref[...] = ((acc_ref[...] + b_ref[...]) * s_ref[...] * z_ref[...].astype(F32)).astype(o_ref.dtype)


def _pool_inv_count(seq):
    half = jnp.repeat(jnp.array(POOL_HALF, jnp.int32), POOL_CH)[None, :]
    t = jnp.arange(seq, dtype=jnp.int32)[:, None]
    count = jnp.minimum(t + half, seq - 1) - jnp.maximum(t - half, 0) + 1
    return 1.0 / count.astype(F32)


def _pool(p, zb, w_bd, inv_count, b, scale, batch, seq):
    m = p.shape[0]
    ncb = D_B // POOL_COLS
    assert seq % POOL_ROWS == 0 and max(POOL_HALF) <= POOL_PAD
    for cb in range(ncb):
        assert (cb * POOL_COLS + POOL_COLS - 1) // POOL_CH - (cb * POOL_COLS) // POOL_CH <= 1
    full = pl.BlockSpec((seq, D_B), lambda bi, cb: (bi, 0))
    vec = pl.BlockSpec((1, D_B), lambda bi, cb: (0, 0))
    return pl.pallas_call(
        functools.partial(_pool_body, seq=seq),
        grid=(batch, ncb),
        in_specs=[pl.BlockSpec((seq, POOL_COLS), lambda bi, cb: (bi, cb)),
                  pl.BlockSpec((POOL_COLS, D_B), lambda bi, cb: (cb, 0)),
                  pl.BlockSpec((seq, POOL_COLS), lambda bi, cb: (0, cb)),
                  full, vec, vec],
        out_specs=full,
        out_shape=jax.ShapeDtypeStruct((m, D_B), BF16),
        scratch_shapes=[pltpu.VMEM((seq + 2 * POOL_PAD, POOL_COLS), BF16), pltpu.VMEM((seq, D_B), F32)],
        compiler_params=_cparams(("parallel", "arbitrary")),
        name="pool_mixer",
    )(p, w_bd, inv_count, zb, b, scale)


def _rope_matrix():
    half = HEAD_DIM // 2
    src = lax.broadcasted_iota(jnp.int32, (LANES, LANES), 0)
    dst = lax.broadcasted_iota(jnp.int32, (LANES, LANES), 1)
    low = (dst % HEAD_DIM) < half
    return jnp.where(low & (src == dst + half), -1.0, jnp.where(~low & (src == dst - half), 1.0, 0.0)).astype(BF16)


def _rope(t, cos, sin, rot_matrix):
    return t * cos + _split_dot(t, rot_matrix, 2) * sin


ATTN_SIDE = 64
ATTN_TILES = 4
COMBINE_ROWS = 256


def _attn_body(q0, q1, q2, k0, k1, k2, v0, v1, v2, cos_ref, sin_ref, zc_ref, o_ref,
               qp_ref, kp_ref, vp_ref, op_ref, lp_ref, on_ref, ln_ref, *, seq):
    q_refs = (q0, q1, q2)
    k_refs = (k0, k1, k2)
    v_refs = (v0, v1, v2)
    scale = HEAD_DIM ** -0.5
    rot_matrix = _rope_matrix()
    for g, (half_span, dil) in enumerate(ATTN_GROUPS):
        cls = seq // dil
        for r in range(dil):
            rows = pl.ds(r, cls, stride=dil) if dil > 1 else pl.ds(0, seq)
            dst = pl.ds(r * cls, cls)
            cos = cos_ref[g, dst, :]
            sin = sin_ref[g, dst, :]
            qp_ref[g, dst, :] = (_rope(q_refs[g][rows, :], cos, sin, rot_matrix) * scale).astype(BF16)
            kp_ref[g, dst, :] = _rope(k_refs[g][rows, :], cos, sin, rot_matrix).astype(BF16)
            vp_ref[g, dst, :] = v_refs[g][rows, :].astype(BF16)

    nt = ATTN_TILES
    first = lax.broadcasted_iota(jnp.int32, (nt, Q_TILE, LANES), 2) < HEAD_DIM
    for g, (half_span, dil) in enumerate(ATTN_GROUPS):
        cls = seq // dil
        klen = min(cls, Q_TILE + 2 * ATTN_SIDE)
        rel = (lax.broadcasted_iota(jnp.int32, (Q_TILE, klen), 1)
               - lax.broadcasted_iota(jnp.int32, (Q_TILE, klen), 0))

        def tiles(it, carry, g=g, cls=cls, klen=klen, rel=rel):
            row0 = pl.multiple_of(it * (nt * Q_TILE), nt * Q_TILE)
            keys, vals, valid = [], [], []
            for j in range(nt):
                qrow = row0 + j * Q_TILE
                base = (qrow // cls) * cls
                krow = pl.multiple_of(base + jnp.clip(qrow - base - ATTN_SIDE, 0, cls - klen), ATTN_SIDE)
                keys.append(kp_ref[g, pl.ds(krow, klen), :])
                vals.append(vp_ref[g, pl.ds(krow, klen), :])
                valid.append(jnp.abs(rel + (krow - qrow)) <= ATTN_SIDE)
            q = qp_ref[g, pl.ds(row0, nt * Q_TILE), :].reshape(nt, Q_TILE, LANES)
            zero = jnp.zeros_like(q)
            qm = jnp.concatenate([jnp.where(first, q, zero), jnp.where(first, zero, q)], axis=0)
            keys = jnp.stack(keys + keys)
            vals = jnp.stack(vals + vals)
            valid = jnp.stack(valid + valid)
            s = jnp.where(valid, _bmm_nt(qm, keys), NEG_BIG)
            mx = jnp.max(s, axis=2, keepdims=True)
            e = jnp.exp(s - mx)
            den = jnp.sum(e, axis=2, keepdims=True)
            o = _bmm(e, vals) / den
            lse = jnp.broadcast_to(mx + jnp.log(den), o.shape)
            rows = pl.ds(row0, nt * Q_TILE)
            op_ref[g, rows, :] = jnp.where(first, o[:nt], o[nt:]).reshape(nt * Q_TILE, LANES)
            lp_ref[g, rows, :] = jnp.where(first, lse[:nt], lse[nt:]).reshape(nt * Q_TILE, LANES)
            return carry

        lax.fori_loop(0, seq // (nt * Q_TILE), tiles, 0, unroll=2)
        if dil > 1:
            for r in range(dil):
                src = pl.ds(r * cls, cls)
                on_ref[g - 1, pl.ds(r, cls, stride=dil), :] = op_ref[g, src, :]
                ln_ref[g - 1, pl.ds(r, cls, stride=dil), :] = lp_ref[g, src, :]

    def combine(i, carry):
        rows = pl.ds(pl.multiple_of(i * COMBINE_ROWS, COMBINE_ROWS), COMBINE_ROWS)
        ls = [lp_ref[0, rows, :], ln_ref[0, rows, :], ln_ref[1, rows, :]]
        os_ = [op_ref[0, rows, :], on_ref[0, rows, :], on_ref[1, rows, :]]
        top = jnp.maximum(jnp.maximum(ls[0], ls[1]), ls[2])
        ws = [jnp.exp(l - top) for l in ls]
        tot = ws[0] + ws[1] + ws[2]
        out = (ws[0] * os_[0] + ws[1] * os_[1] + ws[2] * os_[2]) / tot * zc_ref[rows, :].astype(F32)
        o_ref[rows, :] = out.astype(o_ref.dtype)
        return carry

    lax.fori_loop(0, seq // COMBINE_ROWS, combine, 0)


def _attention(qkv, zc, cos, sin, batch, seq):
    m = qkv.shape[0]
    assert seq % (ATTN_TILES * Q_TILE) == 0 and seq % COMBINE_ROWS == 0
    for half_span, dil in ATTN_GROUPS:
        assert half_span == ATTN_SIDE * dil and (seq // dil) % Q_TILE == 0, (half_span, dil, seq)
    ncol = D_C // LANES
    per_group = D_C_OUT // LANES
    ngroup = len(ATTN_GROUPS)
    spec = lambda sec, g: pl.BlockSpec((seq, LANES), lambda b, jp: (b, sec * ncol + g * per_group + jp))
    tab = pl.BlockSpec((ngroup, seq, LANES), lambda b, jp: (0, 0, 0))
    oz = pl.BlockSpec((seq, LANES), lambda b, jp: (b, jp))
    perm_bf = pltpu.VMEM((ngroup, seq, LANES), BF16)
    perm_f32 = pltpu.VMEM((ngroup, seq, LANES), F32)
    nat_f32 = pltpu.VMEM((ngroup - 1, seq, LANES), F32)
    return pl.pallas_call(
        functools.partial(_attn_body, seq=seq),
        grid=(batch, per_group),
        in_specs=[spec(0, 0), spec(0, 1), spec(0, 2), spec(1, 0), spec(1, 1), spec(1, 2),
                  spec(2, 0), spec(2, 1), spec(2, 2), tab, tab, oz],
        out_specs=oz,
        out_shape=jax.ShapeDtypeStruct((m, D_C_OUT), BF16),
        scratch_shapes=[perm_bf, perm_bf, perm_bf, perm_f32, perm_f32, nat_f32, nat_f32],
        compiler_params=_cparams(("parallel", "parallel")),
        name="dilated_attention",
    )(qkv, qkv, qkv, qkv, qkv, qkv, qkv, qkv, qkv, cos, sin, zc)


def _merge_body(x_ref, yf_ref, yb_ref, bonus_ref, za_ref, ob_ref, oc_ref, g_ref,
                gng_ref, gnb_ref, pa_ref, pb_ref, pc_ref, wo_ref, lng_ref, lnb_ref, o_ref, ob16_ref):
    ones = _head_ones(LANES)
    y = yf_ref[...].astype(F32) + yb_ref[...].astype(F32)
    parts = []
    for cb in range(D_A // LANES):
        sl = slice(cb * LANES, (cb + 1) * LANES)
        yc = y[:, sl]
        mu = _bdot(yc, ones) * (1.0 / HEAD_DIM)
        d = yc - mu
        var = _bdot(d * d, ones) * (1.0 / HEAD_DIM)
        parts.append(d * lax.rsqrt(var + RWKV_GN_EPS))
    yn = jnp.concatenate(parts, axis=1)
    o_a = (yn * gng_ref[...] + gnb_ref[...] + bonus_ref[...].astype(F32)) * za_ref[...].astype(F32)
    merged = (g_ref[:, 0:D_MODEL].astype(F32) * _bdot(o_a, pa_ref[...])
              + g_ref[:, D_MODEL:2 * D_MODEL].astype(F32) * _bdot(ob_ref[...], pb_ref[...])
              + g_ref[:, 2 * D_MODEL:3 * D_MODEL].astype(F32) * _bdot(oc_ref[...], pc_ref[...]))
    t = DEEPNORM_ALPHA * x_ref[...] + _bdot(merged, wo_ref[...])
    mean = jnp.mean(t, axis=1, keepdims=True)
    d = t - mean
    var = jnp.mean(d * d, axis=1, keepdims=True)
    out = d * lax.rsqrt(var + LN_EPS) * lng_ref[...] + lnb_ref[...]
    o_ref[...] = out
    ob16_ref[...] = out.astype(BF16)


def _merge(x, yf, yb, bonus, za, ob, oc, gates, gn_g, gn_b, pa, pb, pc, wo, ln_g, ln_b):
    m = x.shape[0]
    tm = min(ROW_TILE, m)
    rows = lambda n: pl.BlockSpec((tm, n), lambda i: (i, 0))
    full = lambda a: pl.BlockSpec(a.shape, lambda i: (0, 0))
    return pl.pallas_call(
        _merge_body,
        grid=(m // tm,),
        in_specs=[rows(D_MODEL), rows(D_A), rows(D_A), rows(D_A), rows(D_A), rows(D_B), rows(D_C_OUT),
                  rows(3 * D_MODEL), full(gn_g), full(gn_b), full(pa), full(pb), full(pc), full(wo),
                  full(ln_g), full(ln_b)],
        out_specs=[rows(D_MODEL), rows(D_MODEL)],
        out_shape=[jax.ShapeDtypeStruct((m, D_MODEL), F32), jax.ShapeDtypeStruct((m, D_MODEL), BF16)],
        compiler_params=_cparams(("parallel",)),
        name="merge_norm",
    )(x, yf, yb, bonus, za, ob, oc, gates, gn_g, gn_b, pa, pb, pc, wo, ln_g, ln_b)


def _rope_tables(seq):
    inv = jnp.power(ROPE_THETA, -jnp.arange(0, HEAD_DIM, 2, dtype=F32) / HEAD_DIM)
    ang = jnp.arange(seq, dtype=F32)[:, None] * inv[None, :]
    ang = jnp.concatenate([ang, ang, ang, ang], axis=-1)
    by_class = lambda t, d: t.reshape(seq // d, d, LANES).transpose(1, 0, 2).reshape(seq, LANES)
    stack = lambda t: jnp.stack([by_class(t, dil) for _, dil in ATTN_GROUPS])
    return stack(jnp.cos(ang)), stack(jnp.sin(ang))


def _pad_lora(w_up):
    z = jnp.zeros_like(w_up[0])
    return jnp.stack([jnp.concatenate([w_up[0], z], axis=0), jnp.concatenate([z, w_up[1]], axis=0)])


def _block_diag(w_g):
    g, c, _ = w_g.shape
    eye = jnp.eye(g, dtype=w_g.dtype)
    return (eye[:, None, :, None] * w_g[:, :, None, :]).reshape(g * c, g * c)


def kernel(x, w_in, b_in, rwkv_mu, rwkv_w0, rwkv_w_up, rwkv_a0, rwkv_a_up, rwkv_k_k, rwkv_k_a, rwkv_r_k, rwkv_gn_g, rwkv_gn_b, pool_w, pool_b, pool_scale, proj_a, proj_b, proj_c, w_out, ln_g, ln_b):
    batch, seq, _ = x.shape
    m = batch * seq
    cos, sin = _rope_tables(seq)
    inv_count = _pool_inv_count(seq)
    row = lambda t: t.reshape(1, -1)
    bounds = [0]
    for width in (N_SHIFT, D_A, D_B, D_B, 3 * D_C, D_C_OUT, 3 * D_MODEL):
        bounds.append(bounds[-1] + width)
    acts = (None, "silu", None, "silu", None, "silu", "sigmoid")
    dtypes = (F32, BF16, BF16, BF16, F32, BF16, BF16)

    xf = x.reshape(m, D_MODEL)
    xb = xf.astype(BF16)
    for l in range(DEPTH):
        sec = [_linear(xb, w_in, b_in, l, lo, hi - lo, act, dt)
               for lo, hi, act, dt in zip(bounds[:-1], bounds[1:], acts, dtypes)]
        u, za, p, zb, qkv, zc, gates = sec

        r, k, v, kkn, lw, a, bonus = _rwkv_prep(
            u, rwkv_mu[l], rwkv_w0[l], rwkv_a0[l],
            _pad_lora(rwkv_w_up[l]).astype(BF16), _pad_lora(rwkv_a_up[l]).astype(BF16),
            row(rwkv_k_k[l]), row(rwkv_r_k[l]), seq)
        yf, yb = _rwkv_scan(r, k, v, kkn, lw, a, row(rwkv_k_a[l]), batch, seq)
        ob = _pool(p, zb, _block_diag(pool_w[l]).astype(BF16), inv_count, row(pool_b[l]), row(pool_scale[l]),
                   batch, seq)
        oc = _attention(qkv, zc, cos, sin, batch, seq)
        xf, xb = _merge(xf, yf, yb, bonus, za, ob, oc, gates, row(rwkv_gn_g[l]), row(rwkv_gn_b[l]),
                        proj_a[l].astype(BF16), proj_b[l].astype(BF16), proj_c[l].astype(BF16),
                        w_out[l].astype(BF16), row(ln_g[l]), row(ln_b[l]))
    return xf.reshape(batch, seq, D_MODEL)
```
